```python
import math
import jax
import jax.numpy as jnp
from jax import lax
import numpy as np

D_MODEL = 2048
BATCH = 1
SEQ = 16384
DEPTH = 4

S5_GROUP = 16
S5_W = 3 * D_MODEL // 8
S5_GROUPS = S5_W // S5_GROUP
S5_STATE = 64
S5_CHUNK = 128
S5_DT_MIN = 0.001
S5_DT_MAX = 0.1
RWKV_HEAD = 64
RWKV_W = 5 * D_MODEL // 16
RWKV_HEADS = RWKV_W // RWKV_HEAD
RWKV_DECAY_LORA = 96
RWKV_AAA_LORA = 128
RWKV_MV_LORA = 64
RWKV_GATE_LORA = 256
RWKV_COLS = 3 * RWKV_W + RWKV_DECAY_LORA + RWKV_AAA_LORA + RWKV_GATE_LORA
RWKV_LNX_EPS = 64e-5
GLA_DV = 128
GLA_DK = 64
GLA_V = 5 * D_MODEL // 16
GLA_HEADS = GLA_V // GLA_DV
GLA_K = GLA_HEADS * GLA_DK
GLA_LORA = 16
GLA_TAU = 16.0
GLA_CHUNK = 64
GLA_COLS = 2 * GLA_K + 2 * GLA_V + GLA_LORA
N_BRANCH = 3
GATE_COLS = N_BRANCH * D_MODEL
IN_COLS = S5_W + RWKV_COLS + GLA_COLS + GATE_COLS
MIX_W = S5_W + RWKV_W + GLA_V
D_FF = 4 * D_MODEL
NORM_EPS = 1e-6

kernel_name = "hybrid_s5_rwkv7_gla_gated_trunk"


def rmsnorm(x, g):
    xf = x.astype(jnp.float32)
    y = xf * lax.rsqrt(jnp.mean(xf * xf, axis=-1, keepdims=True) + NORM_EPS)
    return (y * g.astype(jnp.float32)).astype(x.dtype)


def token_shift(z, mu):
    prev = jnp.pad(z, ((0, 0), (1, 0), (0, 0)))[:, :-1]
    return z + (prev - z) * mu


def s5_branch(xa, lam_re, lam_im, log_step, b_re, b_im, c_re, c_im, d, glu_w, glu_b):
    bsz, length, _ = xa.shape
    f32 = jnp.float32
    xf = xa.astype(f32)
    lam = lax.complex(jnp.minimum(lam_re.astype(f32), -1e-4), lam_im.astype(f32))
    dt = jnp.exp(log_step.astype(f32))[:, None]
    lam_dt = lam * dt
    lam_bar = jnp.exp(lam_dt)
    b_mat = lax.complex(b_re.astype(f32), b_im.astype(f32))
    b_bar = ((lam_bar - 1.0) / lam)[..., None] * b_mat
    c_mat = lax.complex(c_re.astype(f32), c_im.astype(f32))
    n_chunks = length // S5_CHUNK
    u = xf.reshape(bsz, n_chunks, S5_CHUNK, S5_GROUPS, S5_GROUP).transpose(1, 0, 2, 3, 4)
    pows = jnp.exp(lam_dt[None] * jnp.arange(1, S5_CHUNK + 1, dtype=f32)[:, None, None])

    def binop(e1, e2):
        a1, b1 = e1
        a2, b2 = e2
        return a1 * a2, a2 * b1 + b2

    def step(s, u_c):
        bu = jnp.einsum('bcgi,gpi->bcgp', u_c.astype(jnp.complex64), b_bar)
        a = jnp.broadcast_to(lam_bar, bu.shape)
        _, local = lax.associative_scan(binop, (a, bu), axis=1)
        states = local + pows[None] * s[:, None]
        y = jnp.einsum('bcgp,gip->bcgi', states, c_mat).real
        return states[:, -1], y

    s0 = jnp.zeros((bsz, S5_GROUPS, S5_STATE), jnp.complex64)
    _, ys = lax.scan(step, s0, u)
    y = ys.transpose(1, 0, 2, 3, 4).reshape(bsz, length, S5_W) + d.astype(f32) * xf
    y = jax.nn.gelu(y)
    y = y * jax.nn.sigmoid(y @ glu_w.astype(f32) + glu_b.astype(f32))
    return y.astype(xa.dtype)


def rwkv7_branch(z, mu, w_lora, w0, a_lora, a0, g_lora, k_k, k_a, r_k, lnx_w, lnx_b,
                 v_first, v_gate):
    bsz, length, _ = z.shape
    f32 = jnp.float32
    hh, nn = RWKV_HEADS, RWKV_HEAD
    zs = token_shift(z, mu).astype(f32)
    o1 = 3 * RWKV_W
    r, k, v, w_in, a_in, g_in = jnp.split(
        zs, [RWKV_W, 2 * RWKV_W, o1, o1 + RWKV_DECAY_LORA, o1 + RWKV_DECAY_LORA + RWKV_AAA_LORA], axis=-1)
    w = -jax.nn.softplus(-(w0.astype(f32) + jnp.tanh(w_in) @ w_lora.astype(f32))) - 0.5
    decay = jnp.exp(-jnp.exp(w))
    a = jax.nn.sigmoid(a0.astype(f32) + a_in @ a_lora.astype(f32))
    g = jax.nn.sigmoid(g_in) @ g_lora.astype(f32)
    if v_gate is not None:
        v = v + (v_first - v) * v_gate.astype(f32)
    kk = (k * k_k.astype(f32)).reshape(bsz, length, hh, nn)
    kk = kk * lax.rsqrt(jnp.maximum(jnp.sum(kk * kk, axis=-1, keepdims=True), 1e-24))
    k = k * (1.0 + (a - 1.0) * k_a.astype(f32))

    def heads(t):
        return t.reshape(bsz, length, hh, nn)

    rh, kh, vh, ah, wh = heads(r), heads(k), heads(v), heads(a), heads(decay)

    def tm(t):
        return t.transpose(1, 0, 2, 3)

    def step(s, inp):
        r_t, w_t, k_t, v_t, aa_t, bb_t = inp
        sa = jnp.einsum('bhvk,bhk->bhv', s, aa_t)
        s = s * w_t[:, :, None, :] + sa[..., None] * bb_t[:, :, None, :] + v_t[..., None] * k_t[:, :, None, :]
        return s, jnp.einsum('bhvk,bhk->bhv', s, r_t)

    s0 = jnp.zeros((bsz, hh, nn, nn), f32)
    _, ys = lax.scan(step, s0, (tm(rh), tm(wh), tm(kh), tm(vh), tm(-kk), tm(kk * ah)))
    y = ys.transpose(1, 0, 2, 3)
    mean = jnp.mean(y, axis=-1, keepdims=True)
    var = jnp.mean(jnp.square(y - mean), axis=-1, keepdims=True)
    y = ((y - mean) * lax.rsqrt(var + RWKV_LNX_EPS)).reshape(bsz, length, RWKV_W)
    y = y * lnx_w.astype(f32) + lnx_b.astype(f32)
    bonus = jnp.sum(rh * kh * r_k.astype(f32).reshape(hh, nn), axis=-1, keepdims=True) * vh
    y = (y + bonus.reshape(bsz, length, RWKV_W)) * g
    return y.astype(z.dtype), v


def gla_branch(z, alpha_lora, alpha_bias, norm_g):
    bsz, length, _ = z.shape
    f32 = jnp.float32
    zf = z.astype(f32)
    q, k, v, g, a_in = jnp.split(zf, [GLA_K, 2 * GLA_K, 2 * GLA_K + GLA_V, 2 * GLA_K + 2 * GLA_V], axis=-1)
    log_a = jax.nn.log_sigmoid(a_in @ alpha_lora.astype(f32) + alpha_bias.astype(f32)) / GLA_TAU
    n_chunks = length // GLA_CHUNK

    def heads(t, dh):
        return t.reshape(bsz, n_chunks, GLA_CHUNK, GLA_HEADS, dh).transpose(1, 0, 3, 2, 4)

    qh = heads(q * (GLA_DK ** -0.5), GLA_DK)
    kh, lh, vh = heads(k, GLA_DK), heads(log_a, GLA_DK), heads(v, GLA_DV)
    causal = jnp.tril(jnp.ones((GLA_CHUNK, GLA_CHUNK), dtype=bool))[:, :, None]

    def step(s, inp):
        q_c, k_c, v_c, l_c = inp
        b = jnp.cumsum(l_c, axis=2)
        inter = jnp.einsum('bhtk,bhkv->bhtv', q_c * jnp.exp(b), s)
        diff = b[:, :, :, None, :] - b[:, :, None, :, :]
        dec = jnp.exp(jnp.where(causal, diff, -jnp.inf))
        att = jnp.einsum('bhtk,bhsk,bhtsk->bhts', q_c, k_c, dec)
        intra = jnp.einsum('bhts,bhsv->bhtv', att, v_c)
        b_last = b[:, :, -1:, :]
        s = s * jnp.exp(b_last[:, :, 0, :])[..., None] + jnp.einsum(
            'bhsk,bhsv->bhkv', k_c * jnp.exp(b_last - b), v_c)
        return s, inter + intra

    s0 = jnp.zeros((bsz, GLA_HEADS, GLA_DK, GLA_DV), f32)
    _, os_ = lax.scan(step, s0, (qh, kh, vh, lh))
    o = os_.transpose(1, 0, 3, 2, 4).reshape(bsz, length, GLA_HEADS, GLA_DV)
    o = o * lax.rsqrt(jnp.mean(o * o, axis=-1, keepdims=True) + NORM_EPS)
    o = o.reshape(bsz, length, GLA_V) * norm_g.astype(f32) * jax.nn.silu(g)
    return o.astype(z.dtype)


def setup_inputs(seed: int = 0) -> dict:
    key = jax.random.key(seed)
    ks = iter(jax.random.split(key, 64))
    f32 = jnp.float32

    def nrm(shape, scale):
        return jax.random.normal(next(ks), shape, f32) * scale

    def unif(shape, lo, hi):
        return jax.random.uniform(next(ks), shape, f32, lo, hi)

    d = D_MODEL
    x = nrm((BATCH, SEQ, d), 1.0)
    norm_mix = 1.0 + nrm((DEPTH, d), 0.02)
    w_in = nrm((DEPTH, d, IN_COLS), d ** -0.5)
    gate_bias = nrm((DEPTH, GATE_COLS), 0.1)
    n_state = jnp.arange(S5_STATE, dtype=f32)
    s5_lambda_re = -0.5 + nrm((DEPTH, S5_GROUPS, S5_STATE), 0.01)
    s5_lambda_im = math.pi * n_state + nrm((DEPTH, S5_GROUPS, S5_STATE), 0.01)
    s5_log_step = unif((DEPTH, S5_GROUPS), math.log(S5_DT_MIN), math.log(S5_DT_MAX))
    s5_b_re = nrm((DEPTH, S5_GROUPS, S5_STATE, S5_GROUP), (2 * S5_GROUP) ** -0.5)
    s5_b_im = nrm((DEPTH, S5_GROUPS, S5_STATE, S5_GROUP), (2 * S5_GROUP) ** -0.5)
    s5_c_re = nrm((DEPTH, S5_GROUPS, S5_GROUP, S5_STATE), S5_STATE ** -0.5)
    s5_c_im = nrm((DEPTH, S5_GROUPS, S5_GROUP, S5_STATE), S5_STATE ** -0.5)
    s5_d = nrm((DEPTH, S5_W), 1.0)
    s5_glu_w = nrm((DEPTH, S5_W, S5_W), S5_W ** -0.5)
    s5_glu_b = nrm((DEPTH, S5_W), 0.02)
    rwkv_mu = unif((DEPTH, RWKV_COLS), 0.0, 1.0)
    rwkv_w_lora = nrm((DEPTH, RWKV_DECAY_LORA, RWKV_W), 0.5 * RWKV_DECAY_LORA ** -0.5)
    ratio = jnp.arange(DEPTH, dtype=f32) / max(DEPTH - 1, 1)
    nch = jnp.arange(RWKV_W, dtype=f32) / (RWKV_W - 1)
    decay_speed = -7.0 + 5.0 * nch[None, :] ** (0.85 + jnp.sqrt(ratio)[:, None])
    rwkv_w0 = decay_speed + 0.5 + nrm((DEPTH, RWKV_W), 0.05)
    rwkv_a_lora = nrm((DEPTH, RWKV_AAA_LORA, RWKV_W), 0.5 * RWKV_AAA_LORA ** -0.5)
    rwkv_a0 = nrm((DEPTH, RWKV_W), 0.1)
    rwkv_g_lora = nrm((DEPTH, RWKV_GATE_LORA, RWKV_W), RWKV_GATE_LORA ** -0.5)
    rwkv_k_k = 0.85 + nrm((DEPTH, RWKV_W), 0.02)
    rwkv_k_a = 1.0 + nrm((DEPTH, RWKV_W), 0.02)
    rwkv_r_k = -0.04 + nrm((DEPTH, RWKV_W), 0.02)
    rwkv_lnx_w = 1.0 + nrm((DEPTH, RWKV_W), 0.02)
    rwkv_lnx_b = nrm((DEPTH, RWKV_W), 0.02)
    rwkv_vres_a = nrm((DEPTH - 1, d, RWKV_MV_LORA), d ** -0.5)
    rwkv_vres_mu = unif((DEPTH - 1, RWKV_MV_LORA), 0.0, 1.0)
    rwkv_vres_b = nrm((DEPTH - 1, RWKV_MV_LORA, RWKV_W), RWKV_MV_LORA ** -0.5)
    rwkv_vres_bias = 1.0 + nrm((DEPTH - 1, RWKV_W), 0.1)
    gla_alpha_lora = nrm((DEPTH, GLA_LORA, GLA_K), GLA_LORA ** -0.5)
    gla_alpha_bias = nrm((DEPTH, GLA_K), 0.1)
    gla_norm_g = 1.0 + nrm((DEPTH, GLA_V), 0.02)
    w_up = jnp.concatenate([nrm((DEPTH, S5_W, d), S5_W ** -0.5),
                            nrm((DEPTH, RWKV_W, d), RWKV_W ** -0.5),
                            nrm((DEPTH, GLA_V, d), GLA_V ** -0.5)], axis=1)
    w_out = nrm((DEPTH, d, d), d ** -0.5)
    norm_mlp = 1.0 + nrm((DEPTH, d), 0.02)
    mlp_w1 = nrm((DEPTH, d, D_FF), d ** -0.5)
    mlp_w2 = nrm((DEPTH, D_FF, d), D_FF ** -0.5)
    final_norm = 1.0 + nrm((d,), 0.02)
    return {"x": x, "norm_mix": norm_mix, "w_in": w_in, "gate_bias": gate_bias,
            "s5_lambda_re": s5_lambda_re, "s5_lambda_im": s5_lambda_im, "s5_log_step": s5_log_step,
            "s5_b_re": s5_b_re, "s5_b_im": s5_b_im, "s5_c_re": s5_c_re, "s5_c_im": s5_c_im,
            "s5_d": s5_d, "s5_glu_w": s5_glu_w, "s5_glu_b": s5_glu_b,
            "rwkv_mu": rwkv_mu, "rwkv_w_lora": rwkv_w_lora, "rwkv_w0": rwkv_w0,
            "rwkv_a_lora": rwkv_a_lora, "rwkv_a0": rwkv_a0, "rwkv_g_lora": rwkv_g_lora,
            "rwkv_k_k": rwkv_k_k, "rwkv_k_a": rwkv_k_a, "rwkv_r_k": rwkv_r_k,
            "rwkv_lnx_w": rwkv_lnx_w, "rwkv_lnx_b": rwkv_lnx_b,
            "rwkv_vres_a": rwkv_vres_a, "rwkv_vres_mu": rwkv_vres_mu, "rwkv_vres_b": rwkv_vres_b,
            "rwkv_vres_bias": rwkv_vres_bias,
            "gla_alpha_lora": gla_alpha_lora, "gla_alpha_bias": gla_alpha_bias, "gla_norm_g": gla_norm_g,
            "w_up": w_up, "w_out": w_out, "norm_mlp": norm_mlp, "mlp_w1": mlp_w1, "mlp_w2": mlp_w2,
            "final_norm": final_norm}


def reference(x, norm_mix, w_in, gate_bias,
              s5_lambda_re, s5_lambda_im, s5_log_step, s5_b_re, s5_b_im, s5_c_re, s5_c_im,
              s5_d, s5_glu_w, s5_glu_b,
              rwkv_mu, rwkv_w_lora, rwkv_w0, rwkv_a_lora, rwkv_a0, rwkv_g_lora,
              rwkv_k_k, rwkv_k_a, rwkv_r_k, rwkv_lnx_w, rwkv_lnx_b,
              rwkv_vres_a, rwkv_vres_mu, rwkv_vres_b, rwkv_vres_bias,
              gla_alpha_lora, gla_alpha_bias, gla_norm_g,
              w_up, w_out, norm_mlp, mlp_w1, mlp_w2, final_norm):
    o_rw = S5_W
    o_gla = o_rw + RWKV_COLS
    o_gate = o_gla + GLA_COLS
    r_b = S5_W
    r_c = S5_W + RWKV_W
    v_first = None
    for l in range(DEPTH):
        u = rmsnorm(x, norm_mix[l])
        z = u @ w_in[l]
        y_a = s5_branch(z[..., :o_rw], s5_lambda_re[l], s5_lambda_im[l], s5_log_step[l],
                        s5_b_re[l], s5_b_im[l], s5_c_re[l], s5_c_im[l], s5_d[l],
                        s5_glu_w[l], s5_glu_b[l])
        if l == 0:
            v_gate = None
        else:
            v_gate = jax.nn.sigmoid(rwkv_vres_bias[l - 1] + token_shift(
                u @ rwkv_vres_a[l - 1], rwkv_vres_mu[l - 1]) @ rwkv_vres_b[l - 1])
        y_b, v_l = rwkv7_branch(z[..., o_rw:o_gla], rwkv_mu[l], rwkv_w_lora[l], rwkv_w0[l],
                                rwkv_a_lora[l], rwkv_a0[l], rwkv_g_lora[l], rwkv_k_k[l],
                                rwkv_k_a[l], rwkv_r_k[l], rwkv_lnx_w[l], rwkv_lnx_b[l],
                                v_first, v_gate)
        if l == 0:
            v_first = v_l
        y_c = gla_branch(z[..., o_gla:o_gate], gla_alpha_lora[l], gla_alpha_bias[l], gla_norm_g[l])
        gates = jax.nn.sigmoid(z[..., o_gate:] + gate_bias[l])
        g_a, g_b, g_c = jnp.split(gates, N_BRANCH, axis=-1)
        wu = w_up[l]
        merged = (g_a * (y_a @ wu[:r_b]) + g_b * (y_b @ wu[r_b:r_c]) + g_c * (y_c @ wu[r_c:]))
        x = x + merged @ w_out[l]
        h = rmsnorm(x, norm_mlp[l])
        x = x + jnp.square(jax.nn.relu(h @ mlp_w1[l])) @ mlp_w2[l]
    return rmsnorm(x, final_norm)
```

```python
import functools
import math

import jax
import jax.numpy as jnp
from jax import lax
from jax.experimental import pallas as pl
from jax.experimental.pallas import tpu as pltpu

F32 = jnp.float32
BF16 = jnp.bfloat16

LANE = 128
NORM_EPS = 1e-6

S5_GROUP = 16
S5_STATE = 64
S5_SLAB = 256
S5_SLAB_STATES = (S5_SLAB // S5_GROUP) * S5_STATE
S5_T = 128
S5_LEVELS = 7

RWKV_HEAD = 64
RWKV_T = 64
RWKV_DECAY_LORA = 96
RWKV_AAA_LORA = 128
RWKV_GATE_LORA = 256
RWKV_MV_LORA = 64
RWKV_LNX_EPS = 64e-5
RWKV_PASSES = 3

GLA_DK = 64
GLA_DV = 128
GLA_LORA = 16
GLA_TAU = 16.0
GLA_T = 64

VMEM_LIMIT = 56 * 1024 * 1024


_NN = (((1,), (0,)), ((), ()))
_NT = (((1,), (1,)), ((), ()))
_TN = (((0,), (0,)), ((), ()))


def _hi_lo(x):
    hi = x.astype(BF16)
    return hi, (x - hi.astype(F32)).astype(BF16)


def _dot_dims(a, b, dims, passes):
    mm = lambda p, q: lax.dot_general(p, q, dims, preferred_element_type=F32)
    if passes == 1:
        return mm(a.astype(BF16), b.astype(BF16))
    if passes == 3:
        a_hi, a_lo = _hi_lo(a)
        b_hi, b_lo = _hi_lo(b)
        return mm(a_hi, b_hi) + (mm(a_lo, b_hi) + mm(a_hi, b_lo))
    a3 = _split3(a)
    b3 = _split3(b)
    return (mm(a3[0], b3[0]) + (mm(a3[1], b3[0]) + mm(a3[0], b3[1]))
            + ((mm(a3[2], b3[0]) + mm(a3[0], b3[2])) + mm(a3[1], b3[1])))


def _dot(a, b, passes=1):
    return _dot_dims(a, b, _NN, passes)


def _dot_nt(a, b, passes=1):
    return _dot_dims(a, b, _NT, passes)


def _dot_tn(a, b, passes=1):
    return _dot_dims(a, b, _TN, passes)


def _split3(x):
    hi = x.astype(BF16)
    r1 = x - hi.astype(F32)
    mid = r1.astype(BF16)
    lo = (r1 - mid.astype(F32)).astype(BF16)
    return hi, mid, lo


def _dot_exact_lhs(m_bf16, x):
    hi, mid, lo = _split3(x)
    return (jnp.dot(m_bf16, hi, preferred_element_type=F32)
            + jnp.dot(m_bf16, mid, preferred_element_type=F32)
            + jnp.dot(m_bf16, lo, preferred_element_type=F32))


def _dot_exact_rhs(x, m_bf16):
    hi, mid, lo = _split3(x)
    return (jnp.dot(hi, m_bf16, preferred_element_type=F32)
            + jnp.dot(mid, m_bf16, preferred_element_type=F32)
            + jnp.dot(lo, m_bf16, preferred_element_type=F32))


def _dot_hp(a, b):
    a_hi = a.astype(BF16)
    a_lo = (a - a_hi.astype(F32)).astype(BF16)
    b_hi = b.astype(BF16)
    b_lo = (b - b_hi.astype(F32)).astype(BF16)
    return (jnp.dot(a_hi, b_hi, preferred_element_type=F32)
            + jnp.dot(a_lo, b_hi, preferred_element_type=F32)
            + jnp.dot(a_hi, b_lo, preferred_element_type=F32))


def _sigmoid(x):
    return 1.0 / (1.0 + jnp.exp(-x))


def _softplus(x):
    return jnp.maximum(x, 0.0) + jnp.log(1.0 + jnp.exp(-jnp.abs(x)))


def _rms_rows(x, g):
    return x * lax.rsqrt(jnp.mean(x * x, axis=-1, keepdims=True) + NORM_EPS) * g


def _shift_rows(x, d):
    n = x.shape[0]
    if d % 8 == 0:
        return jnp.concatenate([jnp.zeros((d, x.shape[1]), x.dtype), x[:n - d]], axis=0)
    rolled = pltpu.roll(x, d, 0)
    row = lax.broadcasted_iota(jnp.int32, x.shape, 0)
    return jnp.where(row >= d, rolled, 0.0)


def _params(sem):
    return pltpu.CompilerParams(dimension_semantics=sem, vmem_limit_bytes=VMEM_LIMIT)


def _norm_kernel(x_ref, g_ref, o_ref):
    o_ref[...] = _rms_rows(x_ref[...], g_ref[...]).astype(o_ref.dtype)


def _norm_call(x, g, tm=512):
    m, d = x.shape
    return pl.pallas_call(
        _norm_kernel,
        out_shape=jax.ShapeDtypeStruct((m, d), BF16),
        grid=(m // tm,),
        in_specs=[pl.BlockSpec((tm, d), lambda i: (i, 0)),
                  pl.BlockSpec((1, d), lambda i: (0, 0))],
        out_specs=pl.BlockSpec((tm, d), lambda i: (i, 0)),
        compiler_params=_params(("parallel",)),
        name="rmsnorm",
    )(x, g)


def _mm_kernel(a_ref, w_ref, o_ref):
    o_ref[...] = jnp.dot(a_ref[...], w_ref[...], preferred_element_type=F32).astype(o_ref.dtype)


def _mm_gate_kernel(a_ref, w_ref, b_ref, o_ref):
    z = jnp.dot(a_ref[...], w_ref[...], preferred_element_type=F32) + b_ref[...]
    o_ref[...] = _sigmoid(z).astype(o_ref.dtype)


def _mm_call(a, w, tn, bias=None, tm=1024, name="in_proj"):
    m, k = a.shape
    n = w.shape[1]
    in_specs = [pl.BlockSpec((tm, k), lambda i, j: (i, 0)),
                pl.BlockSpec((k, tn), lambda i, j: (0, j))]
    args = [a, w]
    if bias is None:
        body, out_dtype = _mm_kernel, F32
    else:
        body, out_dtype = _mm_gate_kernel, BF16
        in_specs.append(pl.BlockSpec((1, tn), lambda i, j: (0, j)))
        args.append(bias)
    return pl.pallas_call(
        body,
        out_shape=jax.ShapeDtypeStruct((m, n), out_dtype),
        grid=(m // tm, n // tn),
        in_specs=in_specs,
        out_specs=pl.BlockSpec((tm, tn), lambda i, j: (i, j)),
        compiler_params=_params(("parallel", "parallel")),
        name=name,
    )(*args)


def _s5_kernel(z_ref, bre_ref, bim_ref, pre_ref, pim_ref, cre_ref, cim_ref, d_ref, gw_ref, gb_ref,
               o_ref, sre_ref, sim_ref, carry_re, carry_im):
    n_slab = bre_ref.shape[0]
    tiles_per_slab = S5_SLAB_STATES // LANE
    n_tiles = n_slab * tiles_per_slab
    t = z_ref.shape[0]

    @pl.when(pl.program_id(0) == 0)
    def _():
        carry_re[...] = jnp.zeros_like(carry_re)
        carry_im[...] = jnp.zeros_like(carry_im)

    u = z_ref[...]
    ub = u.astype(BF16)
    for j in range(n_slab):
        uj = ub[:, j * S5_SLAB:(j + 1) * S5_SLAB]
        bu_re = jnp.dot(uj, bre_ref[j], preferred_element_type=F32)
        bu_im = jnp.dot(uj, bim_ref[j], preferred_element_type=F32)
        for c in range(tiles_per_slab):
            sre_ref[j * tiles_per_slab + c] = bu_re[:, c * LANE:(c + 1) * LANE]
            sim_ref[j * tiles_per_slab + c] = bu_im[:, c * LANE:(c + 1) * LANE]

    row = lax.broadcasted_iota(jnp.int32, (t, LANE), 0)

    def tile_body(c, _):
        sr = sre_ref[c]
        si = sim_ref[c]
        pr = pre_ref[c]
        pi = pim_ref[c]
        cr = carry_re[c]
        ci = carry_im[c]
        first = row == 0
        sr = sr + jnp.where(first, pr[0:1] * cr - pi[0:1] * ci, 0.0)
        si = si + jnp.where(first, pr[0:1] * ci + pi[0:1] * cr, 0.0)
        for k in range(S5_LEVELS):
            ar = pr[k:k + 1]
            ai = pi[k:k + 1]
            qr = _shift_rows(sr, 1 << k)
            qi = _shift_rows(si, 1 << k)
            sr, si = sr + (ar * qr - ai * qi), si + (ar * qi + ai * qr)
        sre_ref[c] = sr
        sim_ref[c] = si
        carry_re[c] = sr[t - 1:t]
        carry_im[c] = si[t - 1:t]
        return 0

    lax.fori_loop(0, n_tiles, tile_body, 0)

    ys = []
    for j in range(n_slab):
        s_re = jnp.concatenate([sre_ref[j * tiles_per_slab + c] for c in range(tiles_per_slab)], axis=1)
        s_im = jnp.concatenate([sim_ref[j * tiles_per_slab + c] for c in range(tiles_per_slab)], axis=1)
        ys.append(_dot(s_re, cre_ref[j]) - _dot(s_im, cim_ref[j]))
    y = jnp.concatenate(ys, axis=1) + d_ref[...] * u
    y = y * (0.5 * (1.0 + jnp.tanh(math.sqrt(2.0 / math.pi) * (y + 0.044715 * (y * y * y)))))
    y = y * _sigmoid(_dot(y, gw_ref[...]) + gb_ref[...])
    o_ref[...] = y.astype(o_ref.dtype)


def _s5_call(z, bre, bim, pre, pim, cre, cim, d, gw, gb):
    length, width = z.shape
    n_slab = bre.shape[0]
    n_tiles = n_slab * (S5_SLAB_STATES // LANE)
    full = lambda a: pl.BlockSpec(a.shape, lambda i: (0,) * a.ndim)
    return pl.pallas_call(
        _s5_kernel,
        out_shape=jax.ShapeDtypeStruct((length, width), BF16),
        grid=(length // S5_T,),
        in_specs=[pl.BlockSpec((S5_T, width), lambda i: (i, 0)),
                  full(bre), full(bim), full(pre), full(pim), full(cre), full(cim),
                  full(d), full(gw), full(gb)],
        out_specs=pl.BlockSpec((S5_T, width), lambda i: (i, 0)),
        scratch_shapes=[pltpu.VMEM((n_tiles, S5_T, LANE), F32),
                        pltpu.VMEM((n_tiles, S5_T, LANE), F32),
                        pltpu.VMEM((n_tiles, 1, LANE), F32),
                        pltpu.VMEM((n_tiles, 1, LANE), F32)],
        compiler_params=_params(("arbitrary",)),
        name="s5_mixer",
    )(z, bre, bim, pre, pim, cre, cim, d, gw, gb)


def _s5_tables(lam_re, lam_im, log_step, b_re, b_im, c_re, c_im):
    groups = lam_re.shape[0]
    n_slab = groups * S5_GROUP // S5_SLAB
    gps = S5_SLAB // S5_GROUP
    lr = jnp.minimum(lam_re.astype(F32), -1e-4)
    li = lam_im.astype(F32)
    dt = jnp.exp(log_step.astype(F32))[:, None]
    e = jnp.exp(lr * dt)
    lb_re = e * jnp.cos(li * dt)
    lb_im = e * jnp.sin(li * dt)
    den = lr * lr + li * li
    f_re = ((lb_re - 1.0) * lr + lb_im * li) / den
    f_im = (lb_im * lr - (lb_re - 1.0) * li) / den
    bb_re = f_re[..., None] * b_re - f_im[..., None] * b_im
    bb_im = f_re[..., None] * b_im + f_im[..., None] * b_re
    eye = jnp.eye(gps, dtype=F32)

    def bd_b(m):
        m = m.reshape(n_slab, gps, S5_STATE, S5_GROUP)
        return jnp.einsum('jgpi,gh->jgihp', m, eye).reshape(n_slab, S5_SLAB, S5_SLAB_STATES).astype(BF16)

    def bd_c(m):
        m = m.reshape(n_slab, gps, S5_GROUP, S5_STATE)
        return jnp.einsum('jgip,gh->jgphi', m, eye).reshape(n_slab, S5_SLAB_STATES, S5_SLAB).astype(BF16)

    ks = (2.0 ** jnp.arange(8, dtype=F32))[:, None, None]
    ek = jnp.exp(lr * dt * ks)
    pw_re = (ek * jnp.cos(li * dt * ks)).reshape(8, -1)
    pw_im = (ek * jnp.sin(li * dt * ks)).reshape(8, -1)
    n_tiles = pw_re.shape[1] // LANE
    pw_re = pw_re.reshape(8, n_tiles, LANE).transpose(1, 0, 2)
    pw_im = pw_im.reshape(8, n_tiles, LANE).transpose(1, 0, 2)
    return (bd_b(bb_re), bd_b(bb_im), pw_re, pw_im,
            bd_c(c_re.astype(F32)), bd_c(c_im.astype(F32)))


def _rwkv_kernel(has_vres, *refs):
    if has_vres:
        (z_ref, vf_ref, mu_ref, wl_ref, w0_ref, al_ref, a0_ref, gl_ref, kk_ref, ka_ref, rk_ref,
         lw_ref, lb_ref, ones_ref, vb_ref, vbias_ref, o_ref, prev_ref, h_ref) = refs
    else:
        (z_ref, mu_ref, wl_ref, w0_ref, al_ref, a0_ref, gl_ref, kk_ref, ka_ref, rk_ref,
         lw_ref, lb_ref, ones_ref, o_ref, vo_ref, prev_ref, h_ref) = refs
    t = z_ref.shape[0]
    width = o_ref.shape[1]
    n_pair = width // LANE

    @pl.when(pl.program_id(0) == 0)
    def _():
        prev_ref[...] = jnp.zeros_like(prev_ref)
        h_ref[...] = jnp.zeros_like(h_ref)

    z = z_ref[...]
    rowz = lax.broadcasted_iota(jnp.int32, z.shape, 0)
    prev = jnp.where(rowz == 0, prev_ref[...], pltpu.roll(z, 1, 0))
    prev_ref[...] = z[t - 1:t]
    zs = z + (prev - z) * mu_ref[...]

    o1 = 3 * width
    r = zs[:, 0:width]
    k = zs[:, width:2 * width]
    v = zs[:, 2 * width:o1]
    w_in = zs[:, o1:o1 + LANE]
    a_in = zs[:, o1 + LANE:o1 + 2 * LANE]
    g_in = zs[:, o1 + 2 * LANE:o1 + 2 * LANE + RWKV_GATE_LORA]

    wpre = w0_ref[...] + _dot_hp(jnp.tanh(w_in), wl_ref[...])
    logw = -jnp.exp(-_softplus(-wpre) - 0.5)
    a = _sigmoid(a0_ref[...] + _dot(a_in, al_ref[...]))
    g = _dot(_sigmoid(g_in), gl_ref[...])
    if has_vres:
        vr = zs[:, o1 + 2 * LANE + RWKV_GATE_LORA:]
        vg = _sigmoid(vbias_ref[...] + _dot(vr, vb_ref[...]))
        v = v + (vf_ref[...] - v) * vg
    else:
        vo_ref[...] = v

    ones_bd = ones_ref[...]
    kk = k * kk_ref[...]
    kk = kk * lax.rsqrt(jnp.maximum(_dot_exact_rhs(kk * kk, ones_bd), 1e-24))
    k2 = k * (1.0 + (a - 1.0) * ka_ref[...])
    kka = kk * a

    ri = lax.broadcasted_iota(jnp.int32, (t, t), 0)
    ci = lax.broadcasted_iota(jnp.int32, (t, t), 1)
    tri = jnp.where(ci <= ri, 1.0, 0.0).astype(BF16)
    lc = _dot_exact_lhs(tri, logw)

    gi = lax.broadcasted_iota(jnp.int32, (4 * t, 4 * t), 0)
    gj = lax.broadcasted_iota(jnp.int32, (4 * t, 4 * t), 1)
    ti = gi & (t - 1)
    sj = gj & (t - 1)
    keep = sj + jnp.where(gi < 2 * t, 1, 0) <= ti
    lane = lax.broadcasted_iota(jnp.int32, (t, LANE), 1)
    head0 = lane < RWKV_HEAD

    def stack2(x):
        return jnp.concatenate([jnp.where(head0, x, 0.0), jnp.where(head0, 0.0, x)], axis=0)

    outs = []
    for p in range(n_pair):
        sl = slice(p * LANE, (p + 1) * LANE)
        lcp = lc[:, sl]
        e_pos = jnp.exp(lcp)
        e_neg = jnp.exp(-lcp)
        e_prev = jnp.exp(lcp - logw[:, sl])
        e_last = e_pos[t - 1:t]
        e_rem = jnp.exp(lcp[t - 1:t] - lcp)
        rt = r[:, sl] * e_pos
        at = -kk[:, sl] * e_prev
        bt = kka[:, sl] * e_neg
        kt = k2[:, sl] * e_neg
        bh = kka[:, sl] * e_rem
        kh = k2[:, sl] * e_rem
        vp = v[:, sl]

        lhs = jnp.concatenate([stack2(at), stack2(rt)], axis=0)
        rhs = jnp.concatenate([stack2(bt), stack2(kt)], axis=0)
        gm = jnp.where(keep, _dot_nt(lhs, rhs, RWKV_PASSES), 0.0)
        n1 = gm[0:2 * t, 0:2 * t]
        a_ak = gm[0:2 * t, 2 * t:4 * t]
        a_r = gm[2 * t:4 * t, :]

        ht = h_ref[p]
        ph = _dot_nt(jnp.concatenate([at, rt], axis=0), ht, RWKV_PASSES)
        vbd = stack2(vp)
        u = stack2(ph[0:t]) + _dot(a_ak, vbd, RWKV_PASSES)
        u = u + _dot(n1, u, RWKV_PASSES)
        npow = n1
        for _ in range(5):
            npow = _dot(npow, npow, RWKV_PASSES)
            u = u + _dot(npow, u, RWKV_PASSES)
        uv = jnp.concatenate([u, vbd], axis=0)
        opk = stack2(ph[t:2 * t]) + _dot(a_r, uv, RWKV_PASSES)
        outs.append(opk[0:t] + opk[t:2 * t])
        bk = jnp.concatenate([stack2(bh), stack2(kh)], axis=0)
        h_ref[p] = ht * e_last + _dot_tn(uv, bk, RWKV_PASSES)

    y = jnp.concatenate(outs, axis=1)
    inv_n = 1.0 / RWKV_HEAD
    mean = _dot_exact_rhs(y, ones_bd) * inv_n
    yc = y - mean
    var = _dot_exact_rhs(yc * yc, ones_bd) * inv_n
    yn = yc * lax.rsqrt(var + RWKV_LNX_EPS) * lw_ref[...] + lb_ref[...]
    bonus = _dot_exact_rhs(r * k2 * rk_ref[...], ones_bd) * v
    o_ref[...] = ((yn + bonus) * g).astype(o_ref.dtype)


def _rwkv_call(z, v_first, mu, wl, w0, al, a0, gl, k_k, k_a, r_k, lnx_w, lnx_b, ones_bd, vb, vbias):
    length, zc = z.shape
    width = w0.shape[1]
    has_vres = v_first is not None
    full = lambda a: pl.BlockSpec(a.shape, lambda i: (0,) * a.ndim)
    rows = lambda c: pl.BlockSpec((RWKV_T, c), lambda i: (i, 0))
    common = [mu, wl, w0, al, a0, gl, k_k, k_a, r_k, lnx_w, lnx_b, ones_bd]
    if has_vres:
        args = [z, v_first] + common + [vb, vbias]
        in_specs = [rows(zc), rows(width)] + [full(a) for a in common + [vb, vbias]]
        out_shape = jax.ShapeDtypeStruct((length, width), BF16)
        out_specs = rows(width)
    else:
        args = [z] + common
        in_specs = [rows(zc)] + [full(a) for a in common]
        out_shape = (jax.ShapeDtypeStruct((length, width), BF16),
                     jax.ShapeDtypeStruct((length, width), F32))
        out_specs = (rows(width), rows(width))
    return pl.pallas_call(
        functools.partial(_rwkv_kernel, has_vres),
        out_shape=out_shape,
        grid=(length // RWKV_T,),
        in_specs=in_specs,
        out_specs=out_specs,
        scratch_shapes=[pltpu.VMEM((1, zc), F32),
                        pltpu.VMEM((width // LANE, LANE, LANE), F32)],
        compiler_params=_params(("arbitrary",)),
        name="rwkv7_mixer",
    )(*args)


def _gla_kernel(z_ref, al_ref, ab_ref, ng_ref, o_ref, st_ref):
    t = z_ref.shape[0]
    width = o_ref.shape[1]
    n_head = width // GLA_DV

    @pl.when(pl.program_id(0) == 0)
    def _():
        st_ref[...] = jnp.zeros_like(st_ref)

    z = z_ref[...]
    q = z[:, 0:width] * (GLA_DK ** -0.5)
    k = z[:, width:2 * width]
    v = z[:, 2 * width:3 * width]
    g = z[:, 3 * width:4 * width]
    a_in = z[:, 4 * width:]
    x = _dot_hp(a_in, al_ref[...]) + ab_ref[...]
    log_a = -_softplus(-x) * (1.0 / GLA_TAU)

    ri = lax.broadcasted_iota(jnp.int32, (t, t), 0)
    ci = lax.broadcasted_iota(jnp.int32, (t, t), 1)
    causal = ci <= ri
    tri = jnp.where(causal, 1.0, 0.0).astype(BF16)
    b = _dot_exact_lhs(tri, log_a)

    outs = []
    for h in range(n_head):
        sl = slice(h * GLA_DV, (h + 1) * GLA_DV)
        bh = b[:, sl]
        b_mid = bh[t // 2:t // 2 + 1]
        b_last = bh[t - 1:t]
        qh = q[:, sl]
        kh = k[:, sl]
        vh = v[:, sl]
        st = st_ref[h]
        inter = _dot_nt(qh * jnp.exp(bh), st)
        att = _dot_nt(qh * jnp.exp(bh - b_mid), kh * jnp.exp(b_mid - bh))
        att = jnp.where(causal, att, 0.0)
        o = inter + _dot(att, vh)
        st_ref[h] = st * jnp.exp(b_last) + _dot_tn(vh, kh * jnp.exp(b_last - bh))
        o = o * lax.rsqrt(jnp.mean(o * o, axis=-1, keepdims=True) + NORM_EPS)
        gh = g[:, sl]
        outs.append(o * ng_ref[:, sl] * (gh * _sigmoid(gh)))
    o_ref[...] = jnp.concatenate(outs, axis=1).astype(o_ref.dtype)


def _gla_call(z, al, ab, ng):
    length, zc = z.shape
    width = ng.shape[1]
    full = lambda a: pl.BlockSpec(a.shape, lambda i: (0,) * a.ndim)
    return pl.pallas_call(
        _gla_kernel,
        out_shape=jax.ShapeDtypeStruct((length, width), BF16),
        grid=(length // GLA_T,),
        in_specs=[pl.BlockSpec((GLA_T, zc), lambda i: (i, 0)), full(al), full(ab), full(ng)],
        out_specs=pl.BlockSpec((GLA_T, width), lambda i: (i, 0)),
        scratch_shapes=[pltpu.VMEM((width // GLA_DV, GLA_DV, LANE), F32)],
        compiler_params=_params(("arbitrary",)),
        name="gla_mixer",
    )(z, al, ab, ng)


def _merge_kernel(ya_ref, yb_ref, yc_ref, gate_ref, x_ref, wua_ref, wub_ref, wuc_ref, wo_ref, g_ref,
                  xo_ref, h_ref):
    d = x_ref.shape[1]
    gates = gate_ref[...].astype(F32)
    merged = (gates[:, 0:d] * jnp.dot(ya_ref[...], wua_ref[...], preferred_element_type=F32)
              + gates[:, d:2 * d] * jnp.dot(yb_ref[...], wub_ref[...], preferred_element_type=F32)
              + gates[:, 2 * d:3 * d] * jnp.dot(yc_ref[...], wuc_ref[...], preferred_element_type=F32))
    x = x_ref[...] + _dot(merged, wo_ref[...])
    xo_ref[...] = x
    h_ref[...] = _rms_rows(x, g_ref[...]).astype(h_ref.dtype)


def _merge_call(ya, yb, yc, gates, x, wua, wub, wuc, wo, g, tm=256):
    m, d = x.shape
    rows = lambda a: pl.BlockSpec((tm, a.shape[1]), lambda i: (i, 0))
    const = lambda a: pl.BlockSpec(a.shape, lambda i: (0,) * a.ndim, pipeline_mode=pl.Buffered(1))
    return pl.pallas_call(
        _merge_kernel,
        out_shape=(jax.ShapeDtypeStruct((m, d), F32), jax.ShapeDtypeStruct((m, d), BF16)),
        grid=(m // tm,),
        in_specs=[rows(ya), rows(yb), rows(yc), rows(gates), rows(x),
                  const(wua), const(wub), const(wuc), const(wo), const(g)],
        out_specs=(pl.BlockSpec((tm, d), lambda i: (i, 0)), pl.BlockSpec((tm, d), lambda i: (i, 0))),
        compiler_params=_params(("parallel",)),
        name="merge_out_proj",
    )(ya, yb, yc, gates, x, wua, wub, wuc, wo, g)


def _mlp_kernel(final, h_ref, w1_ref, w2_ref, x_ref, g_ref, xo_ref, *rest):
    f = pl.program_id(1)

    @pl.when(f == 0)
    def _():
        xo_ref[...] = x_ref[...]

    hid = jnp.maximum(jnp.dot(h_ref[...], w1_ref[...], preferred_element_type=F32), 0.0)
    xo_ref[...] += _dot(hid * hid, w2_ref[...])

    @pl.when(f == pl.num_programs(1) - 1)
    def _():
        y = _rms_rows(xo_ref[...], g_ref[...])
        if final:
            xo_ref[...] = y
        else:
            rest[0][...] = y.astype(rest[0].dtype)


def _mlp_call(h, w1, w2, x, g_next, final, tm=512, tf=512):
    m, d = x.shape
    ff = w1.shape[1]
    row_blk = pl.BlockSpec((tm, d), lambda i, f: (i, 0))
    out_shape = [jax.ShapeDtypeStruct((m, d), F32)]
    out_specs = [row_blk]
    if not final:
        out_shape.append(jax.ShapeDtypeStruct((m, d), BF16))
        out_specs.append(row_blk)
    return pl.pallas_call(
        functools.partial(_mlp_kernel, final),
        out_shape=tuple(out_shape),
        grid=(m // tm, ff // tf),
        in_specs=[row_blk,
                  pl.BlockSpec((d, tf), lambda i, f: (0, f)),
                  pl.BlockSpec((tf, d), lambda i, f: (f, 0)),
                  row_blk,
                  pl.BlockSpec((1, d), lambda i, f: (0, 0))],
        out_specs=tuple(out_specs),
        compiler_params=_params(("parallel", "arbitrary")),
        name="mlp",
    )(h, w1, w2, x, g_next)


def _pad_cols(a, n):
    return jnp.pad(a, ((0, 0), (0, n - a.shape[1])))


def _pad_rows(a, n):
    return jnp.pad(a, ((0, n - a.shape[0]), (0, 0)))


def _head_pad(a, heads, dh, to):
    lead = a.shape[:-1]
    a = a.reshape(lead + (heads, dh))
    a = jnp.pad(a, [(0, 0)] * len(lead) + [(0, 0), (0, to - dh)])
    return a.reshape(lead + (heads * to,))


def kernel(x, norm_mix, w_in, gate_bias, s5_lambda_re, s5_lambda_im, s5_log_step, s5_b_re, s5_b_im, s5_c_re, s5_c_im, s5_d, s5_glu_w, s5_glu_b, rwkv_mu, rwkv_w_lora, rwkv_w0, rwkv_a_lora, rwkv_a0, rwkv_g_lora, rwkv_k_k, rwkv_k_a, rwkv_r_k, rwkv_lnx_w, rwkv_lnx_b, rwkv_vres_a, rwkv_vres_mu, rwkv_vres_b, rwkv_vres_bias, gla_alpha_lora, gla_alpha_bias, gla_norm_g, w_up, w_out, norm_mlp, mlp_w1, mlp_w2, final_norm):
    bsz, length, d = x.shape
    depth = w_in.shape[0]
    s5_w = s5_d.shape[1]
    rw_w = rwkv_w0.shape[1]
    gla_v = gla_norm_g.shape[1]
    gla_heads = gla_v // GLA_DV
    gla_k = gla_heads * GLA_DK
    rw_cols = 3 * rw_w + RWKV_DECAY_LORA + RWKV_AAA_LORA + RWKV_GATE_LORA
    gla_cols = 2 * gla_k + 2 * gla_v + GLA_LORA
    o_rw = s5_w
    o_gla = o_rw + rw_cols
    o_gate = o_gla + gla_cols
    o1 = 3 * rw_w

    head_id = jnp.arange(rw_w) // RWKV_HEAD
    ones_bd = (head_id[:, None] == head_id[None, :]).astype(BF16)
    row1 = lambda a: a.reshape(1, -1).astype(F32)

    outs = []
    for b in range(bsz):
        xb = x[b].astype(F32)
        u = _norm_call(xb, row1(norm_mix[0]))
        v_first = None
        for l in range(depth):
            wl = w_in[l]
            w_s5 = wl[:, :o_rw].astype(BF16)
            wr = wl[:, o_rw:o_gla]
            pieces = [wr[:, :o1],
                      _pad_cols(wr[:, o1:o1 + RWKV_DECAY_LORA], LANE),
                      wr[:, o1 + RWKV_DECAY_LORA:o1 + RWKV_DECAY_LORA + RWKV_AAA_LORA],
                      wr[:, o1 + RWKV_DECAY_LORA + RWKV_AAA_LORA:]]
            mu = rwkv_mu[l]
            mu_pieces = [mu[:o1], jnp.pad(mu[o1:o1 + RWKV_DECAY_LORA], (0, LANE - RWKV_DECAY_LORA)),
                         mu[o1 + RWKV_DECAY_LORA:]]
            if l > 0:
                pieces.append(_pad_cols(rwkv_vres_a[l - 1], LANE))
                mu_pieces.append(jnp.pad(rwkv_vres_mu[l - 1], (0, LANE - RWKV_MV_LORA)))
            else:
                pieces.append(jnp.zeros((d, LANE), wl.dtype))
                mu_pieces.append(jnp.zeros((LANE,), mu.dtype))
            w_rw = jnp.concatenate(pieces, axis=1).astype(BF16)
            mu_rw = row1(jnp.concatenate(mu_pieces))
            wg = wl[:, o_gla:o_gate]
            w_gla = jnp.concatenate([
                _head_pad(wg[:, :gla_k], gla_heads, GLA_DK, GLA_DV),
                _head_pad(wg[:, gla_k:2 * gla_k], gla_heads, GLA_DK, GLA_DV),
                wg[:, 2 * gla_k:2 * gla_k + 2 * gla_v],
                _pad_cols(wg[:, 2 * gla_k + 2 * gla_v:], LANE)], axis=1).astype(BF16)
            w_gate = wl[:, o_gate:].astype(BF16)

            z_s5 = _mm_call(u, w_s5, tn=s5_w, name="in_proj_s5")
            z_rw = _mm_call(u, w_rw, tn=w_rw.shape[1] // 5, name="in_proj_rwkv")
            z_gla = _mm_call(u, w_gla, tn=w_gla.shape[1] // 3, name="in_proj_gla")
            gates = _mm_call(u, w_gate, tn=512, bias=row1(gate_bias[l]), name="in_proj_gate")

            tabs = _s5_tables(s5_lambda_re[l], s5_lambda_im[l], s5_log_step[l],
                              s5_b_re[l].astype(F32), s5_b_im[l].astype(F32), s5_c_re[l], s5_c_im[l])
            y_a = _s5_call(z_s5, *tabs, row1(s5_d[l]), s5_glu_w[l].astype(BF16), row1(s5_glu_b[l]))

            wlo = _pad_rows(rwkv_w_lora[l].astype(F32), LANE)
            if l > 0:
                vb = _pad_rows(rwkv_vres_b[l - 1], LANE).astype(BF16)
                vbias = row1(rwkv_vres_bias[l - 1])
            else:
                vb = vbias = None
            res = _rwkv_call(z_rw, v_first, mu_rw, wlo, row1(rwkv_w0[l]),
                             rwkv_a_lora[l].astype(BF16), row1(rwkv_a0[l]), rwkv_g_lora[l].astype(BF16),
                             row1(rwkv_k_k[l]), row1(rwkv_k_a[l]), row1(rwkv_r_k[l]),
                             row1(rwkv_lnx_w[l]), row1(rwkv_lnx_b[l]), ones_bd, vb, vbias)
            if l == 0:
                y_b, v_first = res
            else:
                y_b = res

            al = _pad_rows(_head_pad(gla_alpha_lora[l].astype(F32), gla_heads, GLA_DK, GLA_DV), LANE)
            ab = row1(_head_pad(gla_alpha_bias[l], gla_heads, GLA_DK, GLA_DV))
            y_c = _gla_call(z_gla, al, ab, row1(gla_norm_g[l]))

            wu = w_up[l].astype(BF16)
            x_mid, h = _merge_call(y_a, y_b, y_c, gates, xb,
                                   wu[:s5_w], wu[s5_w:s5_w + rw_w], wu[s5_w + rw_w:],
                                   w_out[l].astype(BF16), row1(norm_mlp[l]))
            final = l == depth - 1
            g_next = row1(final_norm if final else norm_mix[l + 1])
            res = _mlp_call(h, mlp_w1[l].astype(BF16), mlp_w2[l].astype(BF16), x_mid, g_next, final)
            if final:
                xb = res[0]
            else:
                xb, u = res
        outs.append(xb.astype(x.dtype))
    return jnp.stack(outs, axis=0)
```

```python
import functools
import math

import jax
import jax.numpy as jnp
from jax import lax
from jax.experimental import pallas as pl
from jax.experimental.pallas import tpu as pltpu

F32 = jnp.float32
BF16 = jnp.bfloat16

LANE = 128
NORM_EPS = 1e-6

S5_GROUP = 16
S5_STATE = 64
S5_SLAB = 256
S5_SLAB_STATES = (S5_SLAB // S5_GROUP) * S5_STATE
S5_T = 128
S5_LEVELS = 7

RWKV_HEAD = 64
RWKV_T = 64
RWKV_DECAY_LORA = 96
RWKV_AAA_LORA = 128
RWKV_GATE_LORA = 256
RWKV_MV_LORA = 64
RWKV_LNX_EPS = 64e-5

GLA_DK = 64
GLA_DV = 128
GLA_LORA = 16
GLA_TAU = 16.0
GLA_T = 64

VMEM_LIMIT = 56 * 1024 * 1024


def _dot(a, b):
    return jnp.dot(a.astype(BF16), b.astype(BF16), preferred_element_type=F32)


def _dot_nt(a, b):
    return lax.dot_general(a.astype(BF16), b.astype(BF16), (((1,), (1,)), ((), ())),
                           preferred_element_type=F32)


def _dot_tn(a, b):
    return lax.dot_general(a.astype(BF16), b.astype(BF16), (((0,), (0,)), ((), ())),
                           preferred_element_type=F32)


def _split3(x):
    hi = x.astype(BF16)
    r1 = x - hi.astype(F32)
    mid = r1.astype(BF16)
    lo = (r1 - mid.astype(F32)).astype(BF16)
    return hi, mid, lo


def _dot_exact_lhs(m_bf16, x):
    hi, mid, lo = _split3(x)
    return (jnp.dot(m_bf16, hi, preferred_element_type=F32)
            + jnp.dot(m_bf16, mid, preferred_element_type=F32)
            + jnp.dot(m_bf16, lo, preferred_element_type=F32))


def _dot_exact_rhs(x, m_bf16):
    hi, mid, lo = _split3(x)
    return (jnp.dot(hi, m_bf16, preferred_element_type=F32)
            + jnp.dot(mid, m_bf16, preferred_element_type=F32)
            + jnp.dot(lo, m_bf16, preferred_element_type=F32))


def _dot_hp(a, b):
    a_hi = a.astype(BF16)
    a_lo = (a - a_hi.astype(F32)).astype(BF16)
    b_hi = b.astype(BF16)
    b_lo = (b - b_hi.astype(F32)).astype(BF16)
    return (jnp.dot(a_hi, b_hi, preferred_element_type=F32)
            + jnp.dot(a_lo, b_hi, preferred_element_type=F32)
            + jnp.dot(a_hi, b_lo, preferred_element_type=F32))


def _sigmoid(x):
    return 1.0 / (1.0 + jnp.exp(-x))


def _softplus(x):
    return jnp.maximum(x, 0.0) + jnp.log(1.0 + jnp.exp(-jnp.abs(x)))


def _rms_rows(x, g):
    return x * lax.rsqrt(jnp.mean(x * x, axis=-1, keepdims=True) + NORM_EPS) * g


def _shift_rows(x, d):
    n = x.shape[0]
    if d % 8 == 0:
        return jnp.concatenate([jnp.zeros((d, x.shape[1]), x.dtype), x[:n - d]], axis=0)
    rolled = pltpu.roll(x, d, 0)
    row = lax.broadcasted_iota(jnp.int32, x.shape, 0)
    return jnp.where(row >= d, rolled, 0.0)


def _params(sem):
    return pltpu.CompilerParams(dimension_semantics=sem, vmem_limit_bytes=VMEM_LIMIT)


def _norm_kernel(x_ref, g_ref, o_ref):
    o_ref[...] = _rms_rows(x_ref[...], g_ref[...]).astype(o_ref.dtype)


def _norm_call(x, g, tm=512):
    m, d = x.shape
    return pl.pallas_call(
        _norm_kernel,
        out_shape=jax.ShapeDtypeStruct((m, d), BF16),
        grid=(m // tm,),
        in_specs=[pl.BlockSpec((tm, d), lambda i: (i, 0)),
                  pl.BlockSpec((1, d), lambda i: (0, 0))],
        out_specs=pl.BlockSpec((tm, d), lambda i: (i, 0)),
        compiler_params=_params(("parallel",)),
        name="rmsnorm",
    )(x, g)


def _mm_kernel(a_ref, w_ref, o_ref):
    o_ref[...] = jnp.dot(a_ref[...], w_ref[...], preferred_element_type=F32).astype(o_ref.dtype)


def _mm_gate_kernel(a_ref, w_ref, b_ref, o_ref):
    z = jnp.dot(a_ref[...], w_ref[...], preferred_element_type=F32) + b_ref[...]
    o_ref[...] = _sigmoid(z).astype(o_ref.dtype)


def _mm_call(a, w, tn, bias=None, tm=1024, name="in_proj"):
    m, k = a.shape
    n = w.shape[1]
    in_specs = [pl.BlockSpec((tm, k), lambda i, j: (i, 0)),
                pl.BlockSpec((k, tn), lambda i, j: (0, j))]
    args = [a, w]
    if bias is None:
        body, out_dtype = _mm_kernel, F32
    else:
        body, out_dtype = _mm_gate_kernel, BF16
        in_specs.append(pl.BlockSpec((1, tn), lambda i, j: (0, j)))
        args.append(bias)
    return pl.pallas_call(
        body,
        out_shape=jax.ShapeDtypeStruct((m, n), out_dtype),
        grid=(m // tm, n // tn),
        in_specs=in_specs,
        out_specs=pl.BlockSpec((tm, tn), lambda i, j: (i, j)),
        compiler_params=_params(("parallel", "parallel")),
        name=name,
    )(*args)


def _s5_kernel(z_ref, bre_ref, bim_ref, pre_ref, pim_ref, cre_ref, cim_ref, d_ref, gw_ref, gb_ref,
               o_ref, sre_ref, sim_ref, carry_re, carry_im):
    n_slab = bre_ref.shape[0]
    tiles_per_slab = S5_SLAB_STATES // LANE
    n_tiles = n_slab * tiles_per_slab
    t = z_ref.shape[0]

    @pl.when(pl.program_id(0) == 0)
    def _():
        carry_re[...] = jnp.zeros_like(carry_re)
        carry_im[...] = jnp.zeros_like(carry_im)

    u = z_ref[...]
    ub = u.astype(BF16)
    for j in range(n_slab):
        uj = ub[:, j * S5_SLAB:(j + 1) * S5_SLAB]
        bu_re = jnp.dot(uj, bre_ref[j], preferred_element_type=F32)
        bu_im = jnp.dot(uj, bim_ref[j], preferred_element_type=F32)
        for c in range(tiles_per_slab):
            sre_ref[j * tiles_per_slab + c] = bu_re[:, c * LANE:(c + 1) * LANE]
            sim_ref[j * tiles_per_slab + c] = bu_im[:, c * LANE:(c + 1) * LANE]

    row = lax.broadcasted_iota(jnp.int32, (t, LANE), 0)

    def tile_body(c, _):
        sr = sre_ref[c]
        si = sim_ref[c]
        pr = pre_ref[c]
        pi = pim_ref[c]
        cr = carry_re[c]
        ci = carry_im[c]
        first = row == 0
        sr = sr + jnp.where(first, pr[0:1] * cr - pi[0:1] * ci, 0.0)
        si = si + jnp.where(first, pr[0:1] * ci + pi[0:1] * cr, 0.0)
        for k in range(S5_LEVELS):
            ar = pr[k:k + 1]
            ai = pi[k:k + 1]
            qr = _shift_rows(sr, 1 << k)
            qi = _shift_rows(si, 1 << k)
            sr, si = sr + (ar * qr - ai * qi), si + (ar * qi + ai * qr)
        sre_ref[c] = sr
        sim_ref[c] = si
        carry_re[c] = sr[t - 1:t]
        carry_im[c] = si[t - 1:t]
        return 0

    lax.fori_loop(0, n_tiles, tile_body, 0)

    ys = []
    for j in range(n_slab):
        s_re = jnp.concatenate([sre_ref[j * tiles_per_slab + c] for c in range(tiles_per_slab)], axis=1)
        s_im = jnp.concatenate([sim_ref[j * tiles_per_slab + c] for c in range(tiles_per_slab)], axis=1)
        ys.append(_dot(s_re, cre_ref[j]) - _dot(s_im, cim_ref[j]))
    y = jnp.concatenate(ys, axis=1) + d_ref[...] * u
    y = y * (0.5 * (1.0 + jnp.tanh(math.sqrt(2.0 / math.pi) * (y + 0.044715 * (y * y * y)))))
    y = y * _sigmoid(_dot(y, gw_ref[...]) + gb_ref[...])
    o_ref[...] = y.astype(o_ref.dtype)


def _s5_call(z, bre, bim, pre, pim, cre, cim, d, gw, gb):
    length, width = z.shape
    n_slab = bre.shape[0]
    n_tiles = n_slab * (S5_SLAB_STATES // LANE)
    full = lambda a: pl.BlockSpec(a.shape, lambda i: (0,) * a.ndim)
    return pl.pallas_call(
        _s5_kernel,
        out_shape=jax.ShapeDtypeStruct((length, width), BF16),
        grid=(length // S5_T,),
        in_specs=[pl.BlockSpec((S5_T, width), lambda i: (i, 0)),
                  full(bre), full(bim), full(pre), full(pim), full(cre), full(cim),
                  full(d), full(gw), full(gb)],
        out_specs=pl.BlockSpec((S5_T, width), lambda i: (i, 0)),
        scratch_shapes=[pltpu.VMEM((n_tiles, S5_T, LANE), F32),
                        pltpu.VMEM((n_tiles, S5_T, LANE), F32),
                        pltpu.VMEM((n_tiles, 1, LANE), F32),
                        pltpu.VMEM((n_tiles, 1, LANE), F32)],
        compiler_params=_params(("arbitrary",)),
        name="s5_mixer",
    )(z, bre, bim, pre, pim, cre, cim, d, gw, gb)


def _s5_tables(lam_re, lam_im, log_step, b_re, b_im, c_re, c_im):
    groups = lam_re.shape[0]
    n_slab = groups * S5_GROUP // S5_SLAB
    gps = S5_SLAB // S5_GROUP
    lr = jnp.minimum(lam_re.astype(F32), -1e-4)
    li = lam_im.astype(F32)
    dt = jnp.exp(log_step.astype(F32))[:, None]
    e = jnp.exp(lr * dt)
    lb_re = e * jnp.cos(li * dt)
    lb_im = e * jnp.sin(li * dt)
    den = lr * lr + li * li
    f_re = ((lb_re - 1.0) * lr + lb_im * li) / den
    f_im = (lb_im * lr - (lb_re - 1.0) * li) / den
    bb_re = f_re[..., None] * b_re - f_im[..., None] * b_im
    bb_im = f_re[..., None] * b_im + f_im[..., None] * b_re
    eye = jnp.eye(gps, dtype=F32)

    def bd_b(m):
        m = m.reshape(n_slab, gps, S5_STATE, S5_GROUP)
        return jnp.einsum('jgpi,gh->jgihp', m, eye).reshape(n_slab, S5_SLAB, S5_SLAB_STATES).astype(BF16)

    def bd_c(m):
        m = m.reshape(n_slab, gps, S5_GROUP, S5_STATE)
        return jnp.einsum('jgip,gh->jgphi', m, eye).reshape(n_slab, S5_SLAB_STATES, S5_SLAB).astype(BF16)

    ks = (2.0 ** jnp.arange(8, dtype=F32))[:, None, None]
    ek = jnp.exp(lr * dt * ks)
    pw_re = (ek * jnp.cos(li * dt * ks)).reshape(8, -1)
    pw_im = (ek * jnp.sin(li * dt * ks)).reshape(8, -1)
    n_tiles = pw_re.shape[1] // LANE
    pw_re = pw_re.reshape(8, n_tiles, LANE).transpose(1, 0, 2)
    pw_im = pw_im.reshape(8, n_tiles, LANE).transpose(1, 0, 2)
    return (bd_b(bb_re), bd_b(bb_im), pw_re, pw_im,
            bd_c(c_re.astype(F32)), bd_c(c_im.astype(F32)))


def _rwkv_kernel(has_vres, *refs):
    if has_vres:
        (z_ref, vf_ref, mu_ref, wl_ref, w0_ref, al_ref, a0_ref, gl_ref, kk_ref, ka_ref, rk_ref,
         lw_ref, lb_ref, ones_ref, vb_ref, vbias_ref, o_ref, prev_ref, h_ref) = refs
    else:
        (z_ref, mu_ref, wl_ref, w0_ref, al_ref, a0_ref, gl_ref, kk_ref, ka_ref, rk_ref,
         lw_ref, lb_ref, ones_ref, o_ref, vo_ref, prev_ref, h_ref) = refs
    t = z_ref.shape[0]
    width = o_ref.shape[1]
    n_pair = width // LANE

    @pl.when(pl.program_id(0) == 0)
    def _():
        prev_ref[...] = jnp.zeros_like(prev_ref)
        h_ref[...] = jnp.zeros_like(h_ref)

    z = z_ref[...]
    rowz = lax.broadcasted_iota(jnp.int32, z.shape, 0)
    prev = jnp.where(rowz == 0, prev_ref[...], pltpu.roll(z, 1, 0))
    prev_ref[...] = z[t - 1:t]
    zs = z + (prev - z) * mu_ref[...]

    o1 = 3 * width
    r = zs[:, 0:width]
    k = zs[:, width:2 * width]
    v = zs[:, 2 * width:o1]
    w_in = zs[:, o1:o1 + LANE]
    a_in = zs[:, o1 + LANE:o1 + 2 * LANE]
    g_in = zs[:, o1 + 2 * LANE:o1 + 2 * LANE + RWKV_GATE_LORA]

    wpre = w0_ref[...] + _dot_hp(jnp.tanh(w_in), wl_ref[...])
    logw = -jnp.exp(-_softplus(-wpre) - 0.5)
    a = _sigmoid(a0_ref[...] + _dot(a_in, al_ref[...]))
    g = _dot(_sigmoid(g_in), gl_ref[...])
    if has_vres:
        vr = zs[:, o1 + 2 * LANE + RWKV_GATE_LORA:]
        vg = _sigmoid(vbias_ref[...] + _dot(vr, vb_ref[...]))
        v = v + (vf_ref[...] - v) * vg
    else:
        vo_ref[...] = v

    ones_bd = ones_ref[...]
    kk = k * kk_ref[...]
    kk = kk * lax.rsqrt(jnp.maximum(_dot_exact_rhs(kk * kk, ones_bd), 1e-24))
    k2 = k * (1.0 + (a - 1.0) * ka_ref[...])
    kka = kk * a

    ri = lax.broadcasted_iota(jnp.int32, (t, t), 0)
    ci = lax.broadcasted_iota(jnp.int32, (t, t), 1)
    tri = jnp.where(ci <= ri, 1.0, 0.0).astype(BF16)
    lc = _dot_exact_lhs(tri, logw)

    gi = lax.broadcasted_iota(jnp.int32, (4 * t, 4 * t), 0)
    gj = lax.broadcasted_iota(jnp.int32, (4 * t, 4 * t), 1)
    ti = gi & (t - 1)
    sj = gj & (t - 1)
    keep = sj + jnp.where(gi < 2 * t, 1, 0) <= ti
    lane = lax.broadcasted_iota(jnp.int32, (t, LANE), 1)
    head0 = lane < RWKV_HEAD

    def stack2(x):
        return jnp.concatenate([jnp.where(head0, x, 0.0), jnp.where(head0, 0.0, x)], axis=0)

    pairs = range(n_pair)
    cols = [slice(p * LANE, (p + 1) * LANE) for p in pairs]
    lcs = [lc[:, sl] for sl in cols]
    e_pos = [jnp.exp(x) for x in lcs]
    e_neg = [jnp.exp(-x) for x in lcs]
    e_prev = [jnp.exp(x - logw[:, sl]) for x, sl in zip(lcs, cols)]
    e_rem = [jnp.exp(x[t - 1:t] - x) for x in lcs]
    rt = [r[:, sl] * e for sl, e in zip(cols, e_pos)]
    at = [-kk[:, sl] * e for sl, e in zip(cols, e_prev)]
    bt = [kka[:, sl] * e for sl, e in zip(cols, e_neg)]
    kt = [k2[:, sl] * e for sl, e in zip(cols, e_neg)]
    bh = [kka[:, sl] * e for sl, e in zip(cols, e_rem)]
    kh = [k2[:, sl] * e for sl, e in zip(cols, e_rem)]
    vbd = [stack2(v[:, sl]) for sl in cols]
    hts = [h_ref[p] for p in pairs]

    gm = [jnp.where(keep, _dot_nt(jnp.concatenate([stack2(a_), stack2(r_)], axis=0),
                                  jnp.concatenate([stack2(b_), stack2(k_)], axis=0)), 0.0)
          for a_, r_, b_, k_ in zip(at, rt, bt, kt)]
    ph = [_dot_nt(jnp.concatenate([a_, r_], axis=0), h_)
          for a_, r_, h_ in zip(at, rt, hts)]
    n1 = [g_[0:2 * t, 0:2 * t] for g_ in gm]
    u = [stack2(p_[0:t]) + _dot(g_[0:2 * t, 2 * t:4 * t], v_) for p_, g_, v_ in zip(ph, gm, vbd)]
    qi = lax.broadcasted_iota(jnp.int32, (2 * t, 2 * t), 0)
    qj = lax.broadcasted_iota(jnp.int32, (2 * t, 2 * t), 1)
    xinv = [jnp.where(qi == qj, 1.0, jnp.where((qi ^ qj) == 1, n_, 0.0)) for n_ in n1]
    for lvl in range(1, t.bit_length() - 1):
        couple = ((qi >> lvl) ^ (qj >> lvl)) == 1
        xm = [_dot(x_, jnp.where(couple, n_, 0.0)) for x_, n_ in zip(xinv, n1)]
        xinv = [x_ + _dot(m_, x_) for x_, m_ in zip(xinv, xm)]
    u = [_dot(x_, u_) for x_, u_ in zip(xinv, u)]
    uv = [jnp.concatenate([u_, v_], axis=0) for u_, v_ in zip(u, vbd)]
    opk = [stack2(p_[t:2 * t]) + _dot(g_[2 * t:4 * t, :], uv_) for p_, g_, uv_ in zip(ph, gm, uv)]
    outs = [o_[0:t] + o_[t:2 * t] for o_ in opk]
    for p in pairs:
        bk = jnp.concatenate([stack2(bh[p]), stack2(kh[p])], axis=0)
        h_ref[p] = hts[p] * e_pos[p][t - 1:t] + _dot_tn(uv[p], bk)

    y = jnp.concatenate(outs, axis=1)
    inv_n = 1.0 / RWKV_HEAD
    mean = _dot_exact_rhs(y, ones_bd) * inv_n
    yc = y - mean
    var = _dot_exact_rhs(yc * yc, ones_bd) * inv_n
    yn = yc * lax.rsqrt(var + RWKV_LNX_EPS) * lw_ref[...] + lb_ref[...]
    bonus = _dot_exact_rhs(r * k2 * rk_ref[...], ones_bd) * v
    o_ref[...] = ((yn + bonus) * g).astype(o_ref.dtype)


def _rwkv_call(z, v_first, mu, wl, w0, al, a0, gl, k_k, k_a, r_k, lnx_w, lnx_b, ones_bd, vb, vbias):
    length, zc = z.shape
    width = w0.shape[1]
    has_vres = v_first is not None
    full = lambda a: pl.BlockSpec(a.shape, lambda i: (0,) * a.ndim)
    rows = lambda c: pl.BlockSpec((RWKV_T, c), lambda i: (i, 0))
    common = [mu, wl, w0, al, a0, gl, k_k, k_a, r_k, lnx_w, lnx_b, ones_bd]
    if has_vres:
        args = [z, v_first] + common + [vb, vbias]
        in_specs = [rows(zc), rows(width)] + [full(a) for a in common + [vb, vbias]]
        out_shape = jax.ShapeDtypeStruct((length, width), BF16)
        out_specs = rows(width)
    else:
        args = [z] + common
        in_specs = [rows(zc)] + [full(a) for a in common]
        out_shape = (jax.ShapeDtypeStruct((length, width), BF16),
                     jax.ShapeDtypeStruct((length, width), F32))
        out_specs = (rows(width), rows(width))
    return pl.pallas_call(
        functools.partial(_rwkv_kernel, has_vres),
        out_shape=out_shape,
        grid=(length // RWKV_T,),
        in_specs=in_specs,
        out_specs=out_specs,
        scratch_shapes=[pltpu.VMEM((1, zc), F32),
                        pltpu.VMEM((width // LANE, LANE, LANE), F32)],
        compiler_params=_params(("arbitrary",)),
        name="rwkv7_mixer",
    )(*args)


def _gla_kernel(z_ref, al_ref, ab_ref, ng_ref, o_ref, st_ref):
    t = z_ref.shape[0]
    width = o_ref.shape[1]
    n_head = width // GLA_DV

    @pl.when(pl.program_id(0) == 0)
    def _():
        st_ref[...] = jnp.zeros_like(st_ref)

    z = z_ref[...]
    q = z[:, 0:width] * (GLA_DK ** -0.5)
    k = z[:, width:2 * width]
    v = z[:, 2 * width:3 * width]
    g = z[:, 3 * width:4 * width]
    a_in = z[:, 4 * width:]
    x = _dot_hp(a_in, al_ref[...]) + ab_ref[...]
    log_a = -_softplus(-x) * (1.0 / GLA_TAU)

    ri = lax.broadcasted_iota(jnp.int32, (t, t), 0)
    ci = lax.broadcasted_iota(jnp.int32, (t, t), 1)
    causal = ci <= ri
    tri = jnp.where(causal, 1.0, 0.0).astype(BF16)
    b = _dot_exact_lhs(tri, log_a)

    outs = []
    for h in range(n_head):
        sl = slice(h * GLA_DV, (h + 1) * GLA_DV)
        bh = b[:, sl]
        b_mid = bh[t // 2:t // 2 + 1]
        b_last = bh[t - 1:t]
        qh = q[:, sl]
        kh = k[:, sl]
        vh = v[:, sl]
        st = st_ref[h]
        inter = _dot_nt(qh * jnp.exp(bh), st)
        att = _dot_nt(qh * jnp.exp(bh - b_mid), kh * jnp.exp(b_mid - bh))
        att = jnp.where(causal, att, 0.0)
        o = inter + _dot(att, vh)
        st_ref[h] = st * jnp.exp(b_last) + _dot_tn(vh, kh * jnp.exp(b_last - bh))
        o = o * lax.rsqrt(jnp.mean(o * o, axis=-1, keepdims=True) + NORM_EPS)
        gh = g[:, sl]
        outs.append(o * ng_ref[:, sl] * (gh * _sigmoid(gh)))
    o_ref[...] = jnp.concatenate(outs, axis=1).astype(o_ref.dtype)


def _gla_call(z, al, ab, ng):
    length, zc = z.shape
    width = ng.shape[1]
    full = lambda a: pl.BlockSpec(a.shape, lambda i: (0,) * a.ndim)
    return pl.pallas_call(
        _gla_kernel,
        out_shape=jax.ShapeDtypeStruct((length, width), BF16),
        grid=(length // GLA_T,),
        in_specs=[pl.BlockSpec((GLA_T, zc), lambda i: (i, 0)), full(al), full(ab), full(ng)],
        out_specs=pl.BlockSpec((GLA_T, width), lambda i: (i, 0)),
        scratch_shapes=[pltpu.VMEM((width // GLA_DV, GLA_DV, LANE), F32)],
        compiler_params=_params(("arbitrary",)),
        name="gla_mixer",
    )(z, al, ab, ng)


def _merge_kernel(ya_ref, yb_ref, yc_ref, gate_ref, x_ref, wua_ref, wub_ref, wuc_ref, wo_ref, g_ref,
                  xo_ref, h_ref):
    d = x_ref.shape[1]
    gates = gate_ref[...].astype(F32)
    merged = (gates[:, 0:d] * jnp.dot(ya_ref[...], wua_ref[...], preferred_element_type=F32)
              + gates[:, d:2 * d] * jnp.dot(yb_ref[...], wub_ref[...], preferred_element_type=F32)
              + gates[:, 2 * d:3 * d] * jnp.dot(yc_ref[...], wuc_ref[...], preferred_element_type=F32))
    x = x_ref[...] + _dot(merged, wo_ref[...])
    xo_ref[...] = x
    h_ref[...] = _rms_rows(x, g_ref[...]).astype(h_ref.dtype)


def _merge_call(ya, yb, yc, gates, x, wua, wub, wuc, wo, g, tm=256):
    m, d = x.shape
    rows = lambda a: pl.BlockSpec((tm, a.shape[1]), lambda i: (i, 0))
    const = lambda a: pl.BlockSpec(a.shape, lambda i: (0,) * a.ndim, pipeline_mode=pl.Buffered(1))
    return pl.pallas_call(
        _merge_kernel,
        out_shape=(jax.ShapeDtypeStruct((m, d), F32), jax.ShapeDtypeStruct((m, d), BF16)),
        grid=(m // tm,),
        in_specs=[rows(ya), rows(yb), rows(yc), rows(gates), rows(x),
                  const(wua), const(wub), const(wuc), const(wo), const(g)],
        out_specs=(pl.BlockSpec((tm, d), lambda i: (i, 0)), pl.BlockSpec((tm, d), lambda i: (i, 0))),
        compiler_params=_params(("parallel",)),
        name="merge_out_proj",
    )(ya, yb, yc, gates, x, wua, wub, wuc, wo, g)


def _mlp_kernel(final, h_ref, w1_ref, w2_ref, x_ref, g_ref, xo_ref, *rest):
    f = pl.program_id(1)

    @pl.when(f == 0)
    def _():
        xo_ref[...] = x_ref[...]

    hid = jnp.maximum(jnp.dot(h_ref[...], w1_ref[...], preferred_element_type=F32), 0.0)
    xo_ref[...] += _dot(hid * hid, w2_ref[...])

    @pl.when(f == pl.num_programs(1) - 1)
    def _():
        y = _rms_rows(xo_ref[...], g_ref[...])
        if final:
            xo_ref[...] = y
        else:
            rest[0][...] = y.astype(rest[0].dtype)


def _mlp_call(h, w1, w2, x, g_next, final, tm=512, tf=512):
    m, d = x.shape
    ff = w1.shape[1]
    row_blk = pl.BlockSpec((tm, d), lambda i, f: (i, 0))
    out_shape = [jax.ShapeDtypeStruct((m, d), F32)]
    out_specs = [row_blk]
    if not final:
        out_shape.append(jax.ShapeDtypeStruct((m, d), BF16))
        out_specs.append(row_blk)
    return pl.pallas_call(
        functools.partial(_mlp_kernel, final),
        out_shape=tuple(out_shape),
        grid=(m // tm, ff // tf),
        in_specs=[row_blk,
                  pl.BlockSpec((d, tf), lambda i, f: (0, f)),
                  pl.BlockSpec((tf, d), lambda i, f: (f, 0)),
                  row_blk,
                  pl.BlockSpec((1, d), lambda i, f: (0, 0))],
        out_specs=tuple(out_specs),
        compiler_params=_params(("parallel", "arbitrary")),
        name="mlp",
    )(h, w1, w2, x, g_next)


def _pad_cols(a, n):
    return jnp.pad(a, ((0, 0), (0, n - a.shape[1])))


def _pad_rows(a, n):
    return jnp.pad(a, ((0, n - a.shape[0]), (0, 0)))


def _head_pad(a, heads, dh, to):
    lead = a.shape[:-1]
    a = a.reshape(lead + (heads, dh))
    a = jnp.pad(a, [(0, 0)] * len(lead) + [(0, 0), (0, to - dh)])
    return a.reshape(lead + (heads * to,))


def kernel(x, norm_mix, w_in, gate_bias, s5_lambda_re, s5_lambda_im, s5_log_step, s5_b_re, s5_b_im, s5_c_re, s5_c_im, s5_d, s5_glu_w, s5_glu_b, rwkv_mu, rwkv_w_lora, rwkv_w0, rwkv_a_lora, rwkv_a0, rwkv_g_lora, rwkv_k_k, rwkv_k_a, rwkv_r_k, rwkv_lnx_w, rwkv_lnx_b, rwkv_vres_a, rwkv_vres_mu, rwkv_vres_b, rwkv_vres_bias, gla_alpha_lora, gla_alpha_bias, gla_norm_g, w_up, w_out, norm_mlp, mlp_w1, mlp_w2, final_norm):
    bsz, length, d = x.shape
    depth = w_in.shape[0]
    s5_w = s5_d.shape[1]
    rw_w = rwkv_w0.shape[1]
    gla_v = gla_norm_g.shape[1]
    gla_heads = gla_v // GLA_DV
    gla_k = gla_heads * GLA_DK
    rw_cols = 3 * rw_w + RWKV_DECAY_LORA + RWKV_AAA_LORA + RWKV_GATE_LORA
    gla_cols = 2 * gla_k + 2 * gla_v + GLA_LORA
    o_rw = s5_w
    o_gla = o_rw + rw_cols
    o_gate = o_gla + gla_cols
    o1 = 3 * rw_w

    head_id = jnp.arange(rw_w) // RWKV_HEAD
    ones_bd = (head_id[:, None] == head_id[None, :]).astype(BF16)
    row1 = lambda a: a.reshape(1, -1).astype(F32)

    outs = []
    for b in range(bsz):
        xb = x[b].astype(F32)
        u = _norm_call(xb, row1(norm_mix[0]))
        v_first = None
        for l in range(depth):
            wl = w_in[l]
            w_s5 = wl[:, :o_rw].astype(BF16)
            wr = wl[:, o_rw:o_gla]
            pieces = [wr[:, :o1],
                      _pad_cols(wr[:, o1:o1 + RWKV_DECAY_LORA], LANE),
                      wr[:, o1 + RWKV_DECAY_LORA:o1 + RWKV_DECAY_LORA + RWKV_AAA_LORA],
                      wr[:, o1 + RWKV_DECAY_LORA + RWKV_AAA_LORA:]]
            mu = rwkv_mu[l]
            mu_pieces = [mu[:o1], jnp.pad(mu[o1:o1 + RWKV_DECAY_LORA], (0, LANE - RWKV_DECAY_LORA)),
                         mu[o1 + RWKV_DECAY_LORA:]]
            if l > 0:
                pieces.append(_pad_cols(rwkv_vres_a[l - 1], LANE))
                mu_pieces.append(jnp.pad(rwkv_vres_mu[l - 1], (0, LANE - RWKV_MV_LORA)))
            else:
                pieces.append(jnp.zeros((d, LANE), wl.dtype))
                mu_pieces.append(jnp.zeros((LANE,), mu.dtype))
            w_rw = jnp.concatenate(pieces, axis=1).astype(BF16)
            mu_rw = row1(jnp.concatenate(mu_pieces))
            wg = wl[:, o_gla:o_gate]
            w_gla = jnp.concatenate([
                _head_pad(wg[:, :gla_k], gla_heads, GLA_DK, GLA_DV),
                _head_pad(wg[:, gla_k:2 * gla_k], gla_heads, GLA_DK, GLA_DV),
                wg[:, 2 * gla_k:2 * gla_k + 2 * gla_v],
                _pad_cols(wg[:, 2 * gla_k + 2 * gla_v:], LANE)], axis=1).astype(BF16)
            w_gate = wl[:, o_gate:].astype(BF16)

            z_s5 = _mm_call(u, w_s5, tn=s5_w, name="in_proj_s5")
            z_rw = _mm_call(u, w_rw, tn=w_rw.shape[1] // 5, name="in_proj_rwkv")
            z_gla = _mm_call(u, w_gla, tn=w_gla.shape[1] // 3, name="in_proj_gla")
            gates = _mm_call(u, w_gate, tn=512, bias=row1(gate_bias[l]), name="in_proj_gate")

            tabs = _s5_tables(s5_lambda_re[l], s5_lambda_im[l], s5_log_step[l],
                              s5_b_re[l].astype(F32), s5_b_im[l].astype(F32), s5_c_re[l], s5_c_im[l])
            y_a = _s5_call(z_s5, *tabs, row1(s5_d[l]), s5_glu_w[l].astype(BF16), row1(s5_glu_b[l]))

            wlo = _pad_rows(rwkv_w_lora[l].astype(F32), LANE)
            if l > 0:
                vb = _pad_rows(rwkv_vres_b[l - 1], LANE).astype(BF16)
                vbias = row1(rwkv_vres_bias[l - 1])
            else:
                vb = vbias = None
            res = _rwkv_call(z_rw, v_first, mu_rw, wlo, row1(rwkv_w0[l]),
                             rwkv_a_lora[l].astype(BF16), row1(rwkv_a0[l]), rwkv_g_lora[l].astype(BF16),
                             row1(rwkv_k_k[l]), row1(rwkv_k_a[l]), row1(rwkv_r_k[l]),
                             row1(rwkv_lnx_w[l]), row1(rwkv_lnx_b[l]), ones_bd, vb, vbias)
            if l == 0:
                y_b, v_first = res
            else:
                y_b = res

            al = _pad_rows(_head_pad(gla_alpha_lora[l].astype(F32), gla_heads, GLA_DK, GLA_DV), LANE)
            ab = row1(_head_pad(gla_alpha_bias[l], gla_heads, GLA_DK, GLA_DV))
            y_c = _gla_call(z_gla, al, ab, row1(gla_norm_g[l]))

            wu = w_up[l].astype(BF16)
            x_mid, h = _merge_call(y_a, y_b, y_c, gates, xb,
                                   wu[:s5_w], wu[s5_w:s5_w + rw_w], wu[s5_w + rw_w:],
                                   w_out[l].astype(BF16), row1(norm_mlp[l]))
            final = l == depth - 1
            g_next = row1(final_norm if final else norm_mix[l + 1])
            res = _mlp_call(h, mlp_w1[l].astype(BF16), mlp_w2[l].astype(BF16), x_mid, g_next, final)
            if final:
                xb = res[0]
            else:
                xb, u = res
        outs.append(xb.astype(x.dtype))
    return jnp.stack(outs, axis=0)
```

```python
import functools
import math

import jax
import jax.numpy as jnp
from jax import lax
from jax.experimental import pallas as pl
from jax.experimental.pallas import tpu as pltpu

F32 = jnp.float32
BF16 = jnp.bfloat16

LANE = 128
NORM_EPS = 1e-6

S5_GROUP = 16
S5_STATE = 64
S5_SLAB = 256
S5_SLAB_STATES = (S5_SLAB // S5_GROUP) * S5_STATE
S5_T = 128
S5_LEVELS = 7

RWKV_HEAD = 64
RWKV_T = 64
RWKV_DECAY_LORA = 96
RWKV_AAA_LORA = 128
RWKV_GATE_LORA = 256
RWKV_MV_LORA = 64
RWKV_LNX_EPS = 64e-5

GLA_DK = 64
GLA_DV = 128
GLA_LORA = 16
GLA_TAU = 16.0
GLA_T = 64

VMEM_LIMIT = 56 * 1024 * 1024


def _dot(a, b):
    return jnp.dot(a.astype(BF16), b.astype(BF16), preferred_element_type=F32)


def _dot_nt(a, b):
    return lax.dot_general(a.astype(BF16), b.astype(BF16), (((1,), (1,)), ((), ())),
                           preferred_element_type=F32)


def _dot_tn(a, b):
    return lax.dot_general(a.astype(BF16), b.astype(BF16), (((0,), (0,)), ((), ())),
                           preferred_element_type=F32)


def _split3(x):
    hi = x.astype(BF16)
    r1 = x - hi.astype(F32)
    mid = r1.astype(BF16)
    lo = (r1 - mid.astype(F32)).astype(BF16)
    return hi, mid, lo


def _dot_exact_lhs(m_bf16, x):
    hi, mid, lo = _split3(x)
    return (jnp.dot(m_bf16, hi, preferred_element_type=F32)
            + jnp.dot(m_bf16, mid, preferred_element_type=F32)
            + jnp.dot(m_bf16, lo, preferred_element_type=F32))


def _hi_lo(x):
    hi = x.astype(BF16)
    return hi, (x - hi.astype(F32)).astype(BF16)


def _seg_sum(x, seg, seg_t):
    hi, lo = _hi_lo(x)
    s = jnp.dot(hi, seg, preferred_element_type=F32) + jnp.dot(lo, seg, preferred_element_type=F32)
    hi, lo = _hi_lo(s)
    return jnp.dot(hi, seg_t, preferred_element_type=F32) + jnp.dot(lo, seg_t, preferred_element_type=F32)


def _dot_hp(a, b):
    a_hi = a.astype(BF16)
    a_lo = (a - a_hi.astype(F32)).astype(BF16)
    b_hi = b.astype(BF16)
    b_lo = (b - b_hi.astype(F32)).astype(BF16)
    return (jnp.dot(a_hi, b_hi, preferred_element_type=F32)
            + jnp.dot(a_lo, b_hi, preferred_element_type=F32)
            + jnp.dot(a_hi, b_lo, preferred_element_type=F32))


def _sigmoid(x):
    return 1.0 / (1.0 + jnp.exp(-x))


def _softplus(x):
    return jnp.maximum(x, 0.0) + jnp.log(1.0 + jnp.exp(-jnp.abs(x)))


def _rms_rows(x, g):
    return x * lax.rsqrt(jnp.mean(x * x, axis=-1, keepdims=True) + NORM_EPS) * g


def _shift_rows(x, d):
    n = x.shape[0]
    if d % 8 == 0:
        return jnp.concatenate([jnp.zeros((d, x.shape[1]), x.dtype), x[:n - d]], axis=0)
    rolled = pltpu.roll(x, d, 0)
    row = lax.broadcasted_iota(jnp.int32, x.shape, 0)
    return jnp.where(row >= d, rolled, 0.0)


def _params(sem):
    return pltpu.CompilerParams(dimension_semantics=sem, vmem_limit_bytes=VMEM_LIMIT)


def _norm_kernel(x_ref, g_ref, o_ref):
    o_ref[...] = _rms_rows(x_ref[...], g_ref[...]).astype(o_ref.dtype)


def _norm_call(x, g, tm=512):
    m, d = x.shape
    return pl.pallas_call(
        _norm_kernel,
        out_shape=jax.ShapeDtypeStruct((m, d), BF16),
        grid=(m // tm,),
        in_specs=[pl.BlockSpec((tm, d), lambda i: (i, 0)),
                  pl.BlockSpec((1, d), lambda i: (0, 0))],
        out_specs=pl.BlockSpec((tm, d), lambda i: (i, 0)),
        compiler_params=_params(("parallel",)),
        name="rmsnorm",
    )(x, g)


def _mm_kernel(a_ref, w_ref, o_ref):
    o_ref[...] = jnp.dot(a_ref[...], w_ref[...], preferred_element_type=F32).astype(o_ref.dtype)


def _mm_gate_kernel(a_ref, w_ref, b_ref, o_ref):
    z = jnp.dot(a_ref[...], w_ref[...], preferred_element_type=F32) + b_ref[...]
    o_ref[...] = _sigmoid(z).astype(o_ref.dtype)


def _mm_call(a, w, tn, bias=None, tm=1024, name="in_proj"):
    m, k = a.shape
    n = w.shape[1]
    in_specs = [pl.BlockSpec((tm, k), lambda i, j: (i, 0)),
                pl.BlockSpec((k, tn), lambda i, j: (0, j))]
    args = [a, w]
    if bias is None:
        body, out_dtype = _mm_kernel, F32
    else:
        body, out_dtype = _mm_gate_kernel, BF16
        in_specs.append(pl.BlockSpec((1, tn), lambda i, j: (0, j)))
        args.append(bias)
    return pl.pallas_call(
        body,
        out_shape=jax.ShapeDtypeStruct((m, n), out_dtype),
        grid=(m // tm, n // tn),
        in_specs=in_specs,
        out_specs=pl.BlockSpec((tm, tn), lambda i, j: (i, j)),
        compiler_params=_params(("parallel", "parallel")),
        name=name,
    )(*args)


def _s5_kernel(z_ref, bre_ref, bim_ref, pre_ref, pim_ref, cre_ref, cim_ref, d_ref, gw_ref, gb_ref,
               o_ref, sre_ref, sim_ref, carry_re, carry_im):
    n_slab = bre_ref.shape[0]
    tiles_per_slab = S5_SLAB_STATES // LANE
    n_tiles = n_slab * tiles_per_slab
    t = z_ref.shape[0]

    @pl.when(pl.program_id(0) == 0)
    def _():
        carry_re[...] = jnp.zeros_like(carry_re)
        carry_im[...] = jnp.zeros_like(carry_im)

    u = z_ref[...]
    ub = u.astype(BF16)
    for j in range(n_slab):
        uj = ub[:, j * S5_SLAB:(j + 1) * S5_SLAB]
        bu_re = jnp.dot(uj, bre_ref[j], preferred_element_type=F32)
        bu_im = jnp.dot(uj, bim_ref[j], preferred_element_type=F32)
        for c in range(tiles_per_slab):
            sre_ref[j * tiles_per_slab + c] = bu_re[:, c * LANE:(c + 1) * LANE]
            sim_ref[j * tiles_per_slab + c] = bu_im[:, c * LANE:(c + 1) * LANE]

    row = lax.broadcasted_iota(jnp.int32, (t, LANE), 0)

    def tile_body(c, _):
        sr = sre_ref[c]
        si = sim_ref[c]
        pr = pre_ref[c]
        pi = pim_ref[c]
        cr = carry_re[c]
        ci = carry_im[c]
        first = row == 0
        sr = sr + jnp.where(first, pr[0:1] * cr - pi[0:1] * ci, 0.0)
        si = si + jnp.where(first, pr[0:1] * ci + pi[0:1] * cr, 0.0)
        for k in range(S5_LEVELS):
            ar = pr[k:k + 1]
            ai = pi[k:k + 1]
            qr = _shift_rows(sr, 1 << k)
            qi = _shift_rows(si, 1 << k)
            sr, si = sr + (ar * qr - ai * qi), si + (ar * qi + ai * qr)
        sre_ref[c] = sr
        sim_ref[c] = si
        carry_re[c] = sr[t - 1:t]
        carry_im[c] = si[t - 1:t]
        return 0

    lax.fori_loop(0, n_tiles, tile_body, 0)

    ys = []
    for j in range(n_slab):
        s_re = jnp.concatenate([sre_ref[j * tiles_per_slab + c] for c in range(tiles_per_slab)], axis=1)
        s_im = jnp.concatenate([sim_ref[j * tiles_per_slab + c] for c in range(tiles_per_slab)], axis=1)
        ys.append(_dot(s_re, cre_ref[j]) - _dot(s_im, cim_ref[j]))
    y = jnp.concatenate(ys, axis=1) + d_ref[...] * u
    y = y * (0.5 * (1.0 + jnp.tanh(math.sqrt(2.0 / math.pi) * (y + 0.044715 * (y * y * y)))))
    y = y * _sigmoid(_dot(y, gw_ref[...]) + gb_ref[...])
    o_ref[...] = y.astype(o_ref.dtype)


def _s5_call(z, bre, bim, pre, pim, cre, cim, d, gw, gb):
    length, width = z.shape
    n_slab = bre.shape[0]
    n_tiles = n_slab * (S5_SLAB_STATES // LANE)
    full = lambda a: pl.BlockSpec(a.shape, lambda i: (0,) * a.ndim)
    return pl.pallas_call(
        _s5_kernel,
        out_shape=jax.ShapeDtypeStruct((length, width), BF16),
        grid=(length // S5_T,),
        in_specs=[pl.BlockSpec((S5_T, width), lambda i: (i, 0)),
                  full(bre), full(bim), full(pre), full(pim), full(cre), full(cim),
                  full(d), full(gw), full(gb)],
        out_specs=pl.BlockSpec((S5_T, width), lambda i: (i, 0)),
        scratch_shapes=[pltpu.VMEM((n_tiles, S5_T, LANE), F32),
                        pltpu.VMEM((n_tiles, S5_T, LANE), F32),
                        pltpu.VMEM((n_tiles, 1, LANE), F32),
                        pltpu.VMEM((n_tiles, 1, LANE), F32)],
        compiler_params=_params(("arbitrary",)),
        name="s5_mixer",
    )(z, bre, bim, pre, pim, cre, cim, d, gw, gb)


def _s5_tables(lam_re, lam_im, log_step, b_re, b_im, c_re, c_im):
    groups = lam_re.shape[0]
    n_slab = groups * S5_GROUP // S5_SLAB
    gps = S5_SLAB // S5_GROUP
    lr = jnp.minimum(lam_re.astype(F32), -1e-4)
    li = lam_im.astype(F32)
    dt = jnp.exp(log_step.astype(F32))[:, None]
    e = jnp.exp(lr * dt)
    lb_re = e * jnp.cos(li * dt)
    lb_im = e * jnp.sin(li * dt)
    den = lr * lr + li * li
    f_re = ((lb_re - 1.0) * lr + lb_im * li) / den
    f_im = (lb_im * lr - (lb_re - 1.0) * li) / den
    bb_re = f_re[..., None] * b_re - f_im[..., None] * b_im
    bb_im = f_re[..., None] * b_im + f_im[..., None] * b_re
    eye = jnp.eye(gps, dtype=F32)

    def bd_b(m):
        m = m.reshape(n_slab, gps, S5_STATE, S5_GROUP)
        return jnp.einsum('jgpi,gh->jgihp', m, eye).reshape(n_slab, S5_SLAB, S5_SLAB_STATES).astype(BF16)

    def bd_c(m):
        m = m.reshape(n_slab, gps, S5_GROUP, S5_STATE)
        return jnp.einsum('jgip,gh->jgphi', m, eye).reshape(n_slab, S5_SLAB_STATES, S5_SLAB).astype(BF16)

    ks = (2.0 ** jnp.arange(8, dtype=F32))[:, None, None]
    ek = jnp.exp(lr * dt * ks)
    pw_re = (ek * jnp.cos(li * dt * ks)).reshape(8, -1)
    pw_im = (ek * jnp.sin(li * dt * ks)).reshape(8, -1)
    n_tiles = pw_re.shape[1] // LANE
    pw_re = pw_re.reshape(8, n_tiles, LANE).transpose(1, 0, 2)
    pw_im = pw_im.reshape(8, n_tiles, LANE).transpose(1, 0, 2)
    return (bd_b(bb_re), bd_b(bb_im), pw_re, pw_im,
            bd_c(c_re.astype(F32)), bd_c(c_im.astype(F32)))


def _rwkv_kernel(has_vres, *refs):
    if has_vres:
        (z_ref, vf_ref, mu_ref, wl_ref, w0_ref, al_ref, a0_ref, gl_ref, kk_ref, ka_ref, rk_ref,
         lw_ref, lb_ref, seg_ref, segt_ref, vb_ref, vbias_ref, o_ref, prev_ref, h_ref) = refs
    else:
        (z_ref, mu_ref, wl_ref, w0_ref, al_ref, a0_ref, gl_ref, kk_ref, ka_ref, rk_ref,
         lw_ref, lb_ref, seg_ref, segt_ref, o_ref, vo_ref, prev_ref, h_ref) = refs
    t = z_ref.shape[0]
    width = o_ref.shape[1]
    n_pair = width // LANE

    @pl.when(pl.program_id(0) == 0)
    def _():
        prev_ref[...] = jnp.zeros_like(prev_ref)
        h_ref[...] = jnp.zeros_like(h_ref)

    z = z_ref[...]
    rowz = lax.broadcasted_iota(jnp.int32, z.shape, 0)
    prev = jnp.where(rowz == 0, prev_ref[...], pltpu.roll(z, 1, 0))
    prev_ref[...] = z[t - 1:t]
    zs = z + (prev - z) * mu_ref[...]

    o1 = 3 * width
    r = zs[:, 0:width]
    k = zs[:, width:2 * width]
    v = zs[:, 2 * width:o1]
    w_in = zs[:, o1:o1 + LANE]
    a_in = zs[:, o1 + LANE:o1 + 2 * LANE]
    g_in = zs[:, o1 + 2 * LANE:o1 + 2 * LANE + RWKV_GATE_LORA]

    wpre = w0_ref[...] + _dot_hp(jnp.tanh(w_in), wl_ref[...])
    logw = -jnp.exp(-_softplus(-wpre) - 0.5)
    a = _sigmoid(a0_ref[...] + _dot(a_in, al_ref[...]))
    g = _dot(_sigmoid(g_in), gl_ref[...])
    if has_vres:
        vr = zs[:, o1 + 2 * LANE + RWKV_GATE_LORA:]
        vg = _sigmoid(vbias_ref[...] + _dot(vr, vb_ref[...]))
        v = v + (vf_ref[...] - v) * vg
    else:
        vo_ref[...] = v

    seg = seg_ref[...]
    seg_t = segt_ref[...]
    kk = k * kk_ref[...]
    kk = kk * lax.rsqrt(jnp.maximum(_seg_sum(kk * kk, seg, seg_t), 1e-24))
    k2 = k * (1.0 + (a - 1.0) * ka_ref[...])
    kka = kk * a

    ri = lax.broadcasted_iota(jnp.int32, (t, t), 0)
    ci = lax.broadcasted_iota(jnp.int32, (t, t), 1)
    tri = jnp.where(ci <= ri, 1.0, 0.0).astype(BF16)
    lc = _dot_exact_lhs(tri, logw)

    gi = lax.broadcasted_iota(jnp.int32, (4 * t, 4 * t), 0)
    gj = lax.broadcasted_iota(jnp.int32, (4 * t, 4 * t), 1)
    ti = gi & (t - 1)
    sj = gj & (t - 1)
    keep = sj + jnp.where(gi < 2 * t, 1, 0) <= ti
    lane = lax.broadcasted_iota(jnp.int32, (t, LANE), 1)
    head0 = lane < RWKV_HEAD

    def stack2(x):
        return jnp.concatenate([jnp.where(head0, x, 0.0), jnp.where(head0, 0.0, x)], axis=0)

    pairs = range(n_pair)
    cols = [slice(p * LANE, (p + 1) * LANE) for p in pairs]
    lcs = [lc[:, sl] for sl in cols]
    e_pos = [jnp.exp(x) for x in lcs]
    e_neg = [jnp.exp(-x) for x in lcs]
    e_prev = [jnp.exp(x - logw[:, sl]) for x, sl in zip(lcs, cols)]
    e_rem = [jnp.exp(x[t - 1:t] - x) for x in lcs]
    rt = [r[:, sl] * e for sl, e in zip(cols, e_pos)]
    at = [-kk[:, sl] * e for sl, e in zip(cols, e_prev)]
    bt = [kka[:, sl] * e for sl, e in zip(cols, e_neg)]
    kt = [k2[:, sl] * e for sl, e in zip(cols, e_neg)]
    bh = [kka[:, sl] * e for sl, e in zip(cols, e_rem)]
    kh = [k2[:, sl] * e for sl, e in zip(cols, e_rem)]
    vbd = [stack2(v[:, sl]) for sl in cols]
    hts = [h_ref[p] for p in pairs]

    gm = [jnp.where(keep, _dot_nt(jnp.concatenate([stack2(a_), stack2(r_)], axis=0),
                                  jnp.concatenate([stack2(b_), stack2(k_)], axis=0)), 0.0)
          for a_, r_, b_, k_ in zip(at, rt, bt, kt)]
    ph = [_dot_nt(jnp.concatenate([a_, r_], axis=0), h_)
          for a_, r_, h_ in zip(at, rt, hts)]
    n1 = [g_[0:2 * t, 0:2 * t] for g_ in gm]
    u = [stack2(p_[0:t]) + _dot(g_[0:2 * t, 2 * t:4 * t], v_) for p_, g_, v_ in zip(ph, gm, vbd)]
    qi = lax.broadcasted_iota(jnp.int32, (2 * t, 2 * t), 0)
    qj = lax.broadcasted_iota(jnp.int32, (2 * t, 2 * t), 1)
    xinv = [jnp.where(qi == qj, 1.0, jnp.where((qi ^ qj) == 1, n_, 0.0)) for n_ in n1]
    for lvl in range(1, t.bit_length() - 1):
        couple = ((qi >> lvl) ^ (qj >> lvl)) == 1
        xm = [_dot(x_, jnp.where(couple, n_, 0.0)) for x_, n_ in zip(xinv, n1)]
        xinv = [x_ + _dot(m_, x_) for x_, m_ in zip(xinv, xm)]
    u = [_dot(x_, u_) for x_, u_ in zip(xinv, u)]
    uv = [jnp.concatenate([u_, v_], axis=0) for u_, v_ in zip(u, vbd)]
    opk = [stack2(p_[t:2 * t]) + _dot(g_[2 * t:4 * t, :], uv_) for p_, g_, uv_ in zip(ph, gm, uv)]
    outs = [o_[0:t] + o_[t:2 * t] for o_ in opk]
    for p in pairs:
        bk = jnp.concatenate([stack2(bh[p]), stack2(kh[p])], axis=0)
        h_ref[p] = hts[p] * e_pos[p][t - 1:t] + _dot_tn(uv[p], bk)

    y = jnp.concatenate(outs, axis=1)
    inv_n = 1.0 / RWKV_HEAD
    mean = _seg_sum(y, seg, seg_t) * inv_n
    yc = y - mean
    var = _seg_sum(yc * yc, seg, seg_t) * inv_n
    yn = yc * lax.rsqrt(var + RWKV_LNX_EPS) * lw_ref[...] + lb_ref[...]
    bonus = _seg_sum(r * k2 * rk_ref[...], seg, seg_t) * v
    o_ref[...] = ((yn + bonus) * g).astype(o_ref.dtype)


def _rwkv_call(z, v_first, mu, wl, w0, al, a0, gl, k_k, k_a, r_k, lnx_w, lnx_b, seg, seg_t, vb, vbias):
    length, zc = z.shape
    width = w0.shape[1]
    has_vres = v_first is not None
    full = lambda a: pl.BlockSpec(a.shape, lambda i: (0,) * a.ndim)
    rows = lambda c: pl.BlockSpec((RWKV_T, c), lambda i: (i, 0))
    common = [mu, wl, w0, al, a0, gl, k_k, k_a, r_k, lnx_w, lnx_b, seg, seg_t]
    if has_vres:
        args = [z, v_first] + common + [vb, vbias]
        in_specs = [rows(zc), rows(width)] + [full(a) for a in common + [vb, vbias]]
        out_shape = jax.ShapeDtypeStruct((length, width), BF16)
        out_specs = rows(width)
    else:
        args = [z] + common
        in_specs = [rows(zc)] + [full(a) for a in common]
        out_shape = (jax.ShapeDtypeStruct((length, width), BF16),
                     jax.ShapeDtypeStruct((length, width), F32))
        out_specs = (rows(width), rows(width))
    return pl.pallas_call(
        functools.partial(_rwkv_kernel, has_vres),
        out_shape=out_shape,
        grid=(length // RWKV_T,),
        in_specs=in_specs,
        out_specs=out_specs,
        scratch_shapes=[pltpu.VMEM((1, zc), F32),
                        pltpu.VMEM((width // LANE, LANE, LANE), F32)],
        compiler_params=_params(("arbitrary",)),
        name="rwkv7_mixer",
    )(*args)


def _gla_kernel(z_ref, al_ref, ab_ref, ng_ref, o_ref, st_ref):
    t = z_ref.shape[0]
    width = o_ref.shape[1]
    n_head = width // GLA_DV

    @pl.when(pl.program_id(0) == 0)
    def _():
        st_ref[...] = jnp.zeros_like(st_ref)

    z = z_ref[...]
    q = z[:, 0:width] * (GLA_DK ** -0.5)
    k = z[:, width:2 * width]
    v = z[:, 2 * width:3 * width]
    g = z[:, 3 * width:4 * width]
    a_in = z[:, 4 * width:]
    x = _dot_hp(a_in, al_ref[...]) + ab_ref[...]
    log_a = -_softplus(-x) * (1.0 / GLA_TAU)

    ri = lax.broadcasted_iota(jnp.int32, (t, t), 0)
    ci = lax.broadcasted_iota(jnp.int32, (t, t), 1)
    causal = ci <= ri
    tri = jnp.where(causal, 1.0, 0.0).astype(BF16)
    b = _dot_exact_lhs(tri, log_a)

    b_mid = b[t // 2:t // 2 + 1]
    b_last = b[t - 1:t]
    e_last = jnp.exp(b_last)
    q_in = q * jnp.exp(b)
    q_mid = q * jnp.exp(b - b_mid)
    k_mid = k * jnp.exp(b_mid - b)
    k_rem = k * jnp.exp(b_last - b)

    heads = range(n_head)
    cols = [slice(h * GLA_DV, (h + 1) * GLA_DV) for h in heads]
    sts = [st_ref[h] for h in heads]
    att = [jnp.where(causal, _dot_nt(q_mid[:, sl], k_mid[:, sl]), 0.0) for sl in cols]
    inter = [_dot_nt(q_in[:, sl], s_) for sl, s_ in zip(cols, sts)]
    o = [i_ + _dot(a_, v[:, sl]) for i_, a_, sl in zip(inter, att, cols)]
    for h in heads:
        st_ref[h] = sts[h] * e_last[:, cols[h]] + _dot_tn(v[:, cols[h]], k_rem[:, cols[h]])
    o = jnp.concatenate([o_ * lax.rsqrt(jnp.mean(o_ * o_, axis=-1, keepdims=True) + NORM_EPS) for o_ in o],
                        axis=1)
    o_ref[...] = (o * ng_ref[...] * (g * _sigmoid(g))).astype(o_ref.dtype)


def _gla_call(z, al, ab, ng):
    length, zc = z.shape
    width = ng.shape[1]
    full = lambda a: pl.BlockSpec(a.shape, lambda i: (0,) * a.ndim)
    return pl.pallas_call(
        _gla_kernel,
        out_shape=jax.ShapeDtypeStruct((length, width), BF16),
        grid=(length // GLA_T,),
        in_specs=[pl.BlockSpec((GLA_T, zc), lambda i: (i, 0)), full(al), full(ab), full(ng)],
        out_specs=pl.BlockSpec((GLA_T, width), lambda i: (i, 0)),
        scratch_shapes=[pltpu.VMEM((width // GLA_DV, GLA_DV, LANE), F32)],
        compiler_params=_params(("arbitrary",)),
        name="gla_mixer",
    )(z, al, ab, ng)


def _merge_kernel(ya_ref, yb_ref, yc_ref, gate_ref, x_ref, wua_ref, wub_ref, wuc_ref, wo_ref, g_ref,
                  xo_ref, h_ref):
    d = x_ref.shape[1]
    gates = gate_ref[...].astype(F32)
    merged = (gates[:, 0:d] * jnp.dot(ya_ref[...], wua_ref[...], preferred_element_type=F32)
              + gates[:, d:2 * d] * jnp.dot(yb_ref[...], wub_ref[...], preferred_element_type=F32)
              + gates[:, 2 * d:3 * d] * jnp.dot(yc_ref[...], wuc_ref[...], preferred_element_type=F32))
    x = x_ref[...] + _dot(merged, wo_ref[...])
    xo_ref[...] = x
    h_ref[...] = _rms_rows(x, g_ref[...]).astype(h_ref.dtype)


def _merge_call(ya, yb, yc, gates, x, wua, wub, wuc, wo, g, tm=256):
    m, d = x.shape
    rows = lambda a: pl.BlockSpec((tm, a.shape[1]), lambda i: (i, 0))
    const = lambda a: pl.BlockSpec(a.shape, lambda i: (0,) * a.ndim, pipeline_mode=pl.Buffered(1))
    return pl.pallas_call(
        _merge_kernel,
        out_shape=(jax.ShapeDtypeStruct((m, d), F32), jax.ShapeDtypeStruct((m, d), BF16)),
        grid=(m // tm,),
        in_specs=[rows(ya), rows(yb), rows(yc), rows(gates), rows(x),
                  const(wua), const(wub), const(wuc), const(wo), const(g)],
        out_specs=(pl.BlockSpec((tm, d), lambda i: (i, 0)), pl.BlockSpec((tm, d), lambda i: (i, 0))),
        compiler_params=_params(("parallel",)),
        name="merge_out_proj",
    )(ya, yb, yc, gates, x, wua, wub, wuc, wo, g)


def _mlp_kernel(final, h_ref, w1_ref, w2_ref, x_ref, g_ref, xo_ref, *rest):
    f = pl.program_id(1)

    @pl.when(f == 0)
    def _():
        xo_ref[...] = x_ref[...]

    hid = jnp.maximum(jnp.dot(h_ref[...], w1_ref[...], preferred_element_type=F32), 0.0)
    xo_ref[...] += _dot(hid * hid, w2_ref[...])

    @pl.when(f == pl.num_programs(1) - 1)
    def _():
        y = _rms_rows(xo_ref[...], g_ref[...])
        if final:
            xo_ref[...] = y
        else:
            rest[0][...] = y.astype(rest[0].dtype)


def _mlp_call(h, w1, w2, x, g_next, final, tm=512, tf=1024):
    m, d = x.shape
    ff = w1.shape[1]
    row_blk = pl.BlockSpec((tm, d), lambda i, f: (i, 0))
    out_shape = [jax.ShapeDtypeStruct((m, d), F32)]
    out_specs = [row_blk]
    if not final:
        out_shape.append(jax.ShapeDtypeStruct((m, d), BF16))
        out_specs.append(row_blk)
    return pl.pallas_call(
        functools.partial(_mlp_kernel, final),
        out_shape=tuple(out_shape),
        grid=(m // tm, ff // tf),
        in_specs=[row_blk,
                  pl.BlockSpec((d, tf), lambda i, f: (0, f)),
                  pl.BlockSpec((tf, d), lambda i, f: (f, 0)),
                  row_blk,
                  pl.BlockSpec((1, d), lambda i, f: (0, 0))],
        out_specs=tuple(out_specs),
        compiler_params=_params(("parallel", "arbitrary")),
        name="mlp",
    )(h, w1, w2, x, g_next)


def _pad_cols(a, n):
    return jnp.pad(a, ((0, 0), (0, n - a.shape[1])))


def _pad_rows(a, n):
    return jnp.pad(a, ((0, n - a.shape[0]), (0, 0)))


def _head_pad(a, heads, dh, to):
    lead = a.shape[:-1]
    a = a.reshape(lead + (heads, dh))
    a = jnp.pad(a, [(0, 0)] * len(lead) + [(0, 0), (0, to - dh)])
    return a.reshape(lead + (heads * to,))


def kernel(x, norm_mix, w_in, gate_bias, s5_lambda_re, s5_lambda_im, s5_log_step, s5_b_re, s5_b_im, s5_c_re, s5_c_im, s5_d, s5_glu_w, s5_glu_b, rwkv_mu, rwkv_w_lora, rwkv_w0, rwkv_a_lora, rwkv_a0, rwkv_g_lora, rwkv_k_k, rwkv_k_a, rwkv_r_k, rwkv_lnx_w, rwkv_lnx_b, rwkv_vres_a, rwkv_vres_mu, rwkv_vres_b, rwkv_vres_bias, gla_alpha_lora, gla_alpha_bias, gla_norm_g, w_up, w_out, norm_mlp, mlp_w1, mlp_w2, final_norm):
    bsz, length, d = x.shape
    depth = w_in.shape[0]
    s5_w = s5_d.shape[1]
    rw_w = rwkv_w0.shape[1]
    gla_v = gla_norm_g.shape[1]
    gla_heads = gla_v // GLA_DV
    gla_k = gla_heads * GLA_DK
    rw_cols = 3 * rw_w + RWKV_DECAY_LORA + RWKV_AAA_LORA + RWKV_GATE_LORA
    gla_cols = 2 * gla_k + 2 * gla_v + GLA_LORA
    o_rw = s5_w
    o_gla = o_rw + rw_cols
    o_gate = o_gla + gla_cols
    o1 = 3 * rw_w

    head_id = jnp.arange(rw_w) // RWKV_HEAD
    seg = (head_id[:, None] == jnp.arange(LANE)[None, :]).astype(BF16)
    seg_t = seg.T
    row1 = lambda a: a.reshape(1, -1).astype(F32)

    outs = []
    for b in range(bsz):
        xb = x[b].astype(F32)
        u = _norm_call(xb, row1(norm_mix[0]))
        v_first = None
        for l in range(depth):
            wl = w_in[l]
            w_s5 = wl[:, :o_rw].astype(BF16)
            wr = wl[:, o_rw:o_gla]
            pieces = [wr[:, :o1],
                      _pad_cols(wr[:, o1:o1 + RWKV_DECAY_LORA], LANE),
                      wr[:, o1 + RWKV_DECAY_LORA:o1 + RWKV_DECAY_LORA + RWKV_AAA_LORA],
                      wr[:, o1 + RWKV_DECAY_LORA + RWKV_AAA_LORA:]]
            mu = rwkv_mu[l]
            mu_pieces = [mu[:o1], jnp.pad(mu[o1:o1 + RWKV_DECAY_LORA], (0, LANE - RWKV_DECAY_LORA)),
                         mu[o1 + RWKV_DECAY_LORA:]]
            if l > 0:
                pieces.append(_pad_cols(rwkv_vres_a[l - 1], LANE))
                mu_pieces.append(jnp.pad(rwkv_vres_mu[l - 1], (0, LANE - RWKV_MV_LORA)))
            else:
                pieces.append(jnp.zeros((d, LANE), wl.dtype))
                mu_pieces.append(jnp.zeros((LANE,), mu.dtype))
            w_rw = jnp.concatenate(pieces, axis=1).astype(BF16)
            mu_rw = row1(jnp.concatenate(mu_pieces))
            wg = wl[:, o_gla:o_gate]
            w_gla = jnp.concatenate([
                _head_pad(wg[:, :gla_k], gla_heads, GLA_DK, GLA_DV),
                _head_pad(wg[:, gla_k:2 * gla_k], gla_heads, GLA_DK, GLA_DV),
                wg[:, 2 * gla_k:2 * gla_k + 2 * gla_v],
                _pad_cols(wg[:, 2 * gla_k + 2 * gla_v:], LANE)], axis=1).astype(BF16)
            w_gate = wl[:, o_gate:].astype(BF16)

            z_s5 = _mm_call(u, w_s5, tn=s5_w, name="in_proj_s5")
            z_rw = _mm_call(u, w_rw, tn=w_rw.shape[1] // 2, name="in_proj_rwkv")
            z_gla = _mm_call(u, w_gla, tn=w_gla.shape[1] // 3, name="in_proj_gla")
            gates = _mm_call(u, w_gate, tn=1024, bias=row1(gate_bias[l]), name="in_proj_gate")

            tabs = _s5_tables(s5_lambda_re[l], s5_lambda_im[l], s5_log_step[l],
                              s5_b_re[l].astype(F32), s5_b_im[l].astype(F32), s5_c_re[l], s5_c_im[l])
            y_a = _s5_call(z_s5, *tabs, row1(s5_d[l]), s5_glu_w[l].astype(BF16), row1(s5_glu_b[l]))

            wlo = _pad_rows(rwkv_w_lora[l].astype(F32), LANE)
            if l > 0:
                vb = _pad_rows(rwkv_vres_b[l - 1], LANE).astype(BF16)
                vbias = row1(rwkv_vres_bias[l - 1])
            else:
                vb = vbias = None
            res = _rwkv_call(z_rw, v_first, mu_rw, wlo, row1(rwkv_w0[l]),
                             rwkv_a_lora[l].astype(BF16), row1(rwkv_a0[l]), rwkv_g_lora[l].astype(BF16),
                             row1(rwkv_k_k[l]), row1(rwkv_k_a[l]), row1(rwkv_r_k[l]),
                             row1(rwkv_lnx_w[l]), row1(rwkv_lnx_b[l]), seg, seg_t, vb, vbias)
            if l == 0:
                y_b, v_first = res
            else:
                y_b = res

            al = _pad_rows(_head_pad(gla_alpha_lora[l].astype(F32), gla_heads, GLA_DK, GLA_DV), LANE)
            ab = row1(_head_pad(gla_alpha_bias[l], gla_heads, GLA_DK, GLA_DV))
            y_c = _gla_call(z_gla, al, ab, row1(gla_norm_g[l]))

            wu = w_up[l].astype(BF16)
            x_mid, h = _merge_call(y_a, y_b, y_c, gates, xb,
                                   wu[:s5_w], wu[s5_w:s5_w + rw_w], wu[s5_w + rw_w:],
                                   w_out[l].astype(BF16), row1(norm_mlp[l]))
            final = l == depth - 1
            g_next = row1(final_norm if final else norm_mix[l + 1])
            res = _mlp_call(h, mlp_w1[l].astype(BF16), mlp_w2[l].astype(BF16), x_mid, g_next, final)
            if final:
                xb = res[0]
            else:
                xb, u = res
        outs.append(xb.astype(x.dtype))
    return jnp.stack(outs, axis=0)
```

```python
import functools
import math

import jax
import jax.numpy as jnp
from jax import lax
from jax.experimental import pallas as pl
from jax.experimental.pallas import tpu as pltpu

F32 = jnp.float32
BF16 = jnp.bfloat16

LANE = 128
SUBLANE = 8
NORM_EPS = 1e-6

S5_GROUP = 16
S5_STATE = 64
S5_SLAB = 256
S5_SLAB_STATES = (S5_SLAB // S5_GROUP) * S5_STATE
S5_T = 128
S5_BLOCK_LEVELS = 3

RWKV_HEAD = 64
RWKV_T = 64
RWKV_BLOCK = 128
RWKV_DECAY_LORA = 96
RWKV_AAA_LORA = 128
RWKV_GATE_LORA = 256
RWKV_MV_LORA = 64
RWKV_LNX_EPS = 64e-5

GLA_DK = 64
GLA_DV = 128
GLA_LORA = 16
GLA_TAU = 16.0
GLA_T = 64
GLA_BLOCK = 256

VMEM_LIMIT = 56 * 1024 * 1024


def _dot(a, b):
    return jnp.dot(a.astype(BF16), b.astype(BF16), preferred_element_type=F32)


def _dot_nt(a, b):
    return lax.dot_general(a.astype(BF16), b.astype(BF16), (((1,), (1,)), ((), ())),
                           preferred_element_type=F32)


def _dot_tn(a, b):
    return lax.dot_general(a.astype(BF16), b.astype(BF16), (((0,), (0,)), ((), ())),
                           preferred_element_type=F32)


def _split3(x):
    hi = x.astype(BF16)
    r1 = x - hi.astype(F32)
    mid = r1.astype(BF16)
    lo = (r1 - mid.astype(F32)).astype(BF16)
    return hi, mid, lo


def _dot_exact_lhs(m_bf16, x):
    hi, mid, lo = _split3(x)
    return (jnp.dot(m_bf16, hi, preferred_element_type=F32)
            + jnp.dot(m_bf16, mid, preferred_element_type=F32)
            + jnp.dot(m_bf16, lo, preferred_element_type=F32))


def _hi_lo(x):
    hi = x.astype(BF16)
    return hi, (x - hi.astype(F32)).astype(BF16)


def _seg_sum(x, seg, seg_t):
    hi, lo = _hi_lo(x)
    s = jnp.dot(hi, seg, preferred_element_type=F32) + jnp.dot(lo, seg, preferred_element_type=F32)
    hi, lo = _hi_lo(s)
    return jnp.dot(hi, seg_t, preferred_element_type=F32) + jnp.dot(lo, seg_t, preferred_element_type=F32)


def _dot_hp(a, b):
    a_hi = a.astype(BF16)
    a_lo = (a - a_hi.astype(F32)).astype(BF16)
    b_hi = b.astype(BF16)
    b_lo = (b - b_hi.astype(F32)).astype(BF16)
    return (jnp.dot(a_hi, b_hi, preferred_element_type=F32)
            + jnp.dot(a_lo, b_hi, preferred_element_type=F32)
            + jnp.dot(a_hi, b_lo, preferred_element_type=F32))


def _sigmoid(x):
    return 1.0 / (1.0 + jnp.exp(-x))


def _softplus(x):
    return jnp.maximum(x, 0.0) + jnp.log(1.0 + jnp.exp(-jnp.abs(x)))


def _rms_rows(x, g):
    return x * lax.rsqrt(jnp.mean(x * x, axis=-1, keepdims=True) + NORM_EPS) * g


def _shift_rows(x, d):
    n = x.shape[0]
    if d % 8 == 0:
        return jnp.concatenate([jnp.zeros((d, x.shape[1]), x.dtype), x[:n - d]], axis=0)
    rolled = pltpu.roll(x, d, 0)
    row = lax.broadcasted_iota(jnp.int32, x.shape, 0)
    return jnp.where(row >= d, rolled, 0.0)


def _params(sem):
    return pltpu.CompilerParams(dimension_semantics=sem, vmem_limit_bytes=VMEM_LIMIT)


def _norm_kernel(x_ref, g_ref, o_ref):
    o_ref[...] = _rms_rows(x_ref[...], g_ref[...]).astype(o_ref.dtype)


def _norm_call(x, g, tm=512):
    m, d = x.shape
    return pl.pallas_call(
        _norm_kernel,
        out_shape=jax.ShapeDtypeStruct((m, d), BF16),
        grid=(m // tm,),
        in_specs=[pl.BlockSpec((tm, d), lambda i: (i, 0)),
                  pl.BlockSpec((1, d), lambda i: (0, 0))],
        out_specs=pl.BlockSpec((tm, d), lambda i: (i, 0)),
        compiler_params=_params(("parallel",)),
        name="rmsnorm",
    )(x, g)


def _mm_kernel(a_ref, w_ref, o_ref):
    o_ref[...] = jnp.dot(a_ref[...], w_ref[...], preferred_element_type=F32).astype(o_ref.dtype)


def _mm_gate_kernel(a_ref, w_ref, b_ref, o_ref):
    z = jnp.dot(a_ref[...], w_ref[...], preferred_element_type=F32) + b_ref[...]
    o_ref[...] = _sigmoid(z).astype(o_ref.dtype)


def _mm_call(a, w, tn, bias=None, tm=1024, name="in_proj"):
    m, k = a.shape
    n = w.shape[1]
    in_specs = [pl.BlockSpec((tm, k), lambda i, j: (i, 0)),
                pl.BlockSpec((k, tn), lambda i, j: (0, j))]
    args = [a, w]
    if bias is None:
        body, out_dtype = _mm_kernel, F32
    else:
        body, out_dtype = _mm_gate_kernel, BF16
        in_specs.append(pl.BlockSpec((1, tn), lambda i, j: (0, j)))
        args.append(bias)
    return pl.pallas_call(
        body,
        out_shape=jax.ShapeDtypeStruct((m, n), out_dtype),
        grid=(m // tm, n // tn),
        in_specs=in_specs,
        out_specs=pl.BlockSpec((tm, tn), lambda i, j: (i, j)),
        compiler_params=_params(("parallel", "parallel")),
        name=name,
    )(*args)


def _s5_kernel(z_ref, bre_ref, bim_ref, pre_ref, pim_ref, cre_ref, cim_ref, d_ref, gw_ref, gb_ref,
               o_ref, carry_re, carry_im):
    n_slab = bre_ref.shape[0]
    tiles_per_slab = S5_SLAB_STATES // LANE
    t = z_ref.shape[0]

    @pl.when(pl.program_id(0) == 0)
    def _():
        carry_re[...] = jnp.zeros_like(carry_re)
        carry_im[...] = jnp.zeros_like(carry_im)

    u = z_ref[...]
    ub = u.astype(BF16)
    n_blk = t // SUBLANE
    ys = []
    for j in range(n_slab):
        uj = ub[:, j * S5_SLAB:(j + 1) * S5_SLAB]
        bu_re = jnp.dot(uj, bre_ref[j], preferred_element_type=F32)
        bu_im = jnp.dot(uj, bim_ref[j], preferred_element_type=F32)
        tiles_re = []
        tiles_im = []
        for c in range(tiles_per_slab):
            idx = j * tiles_per_slab + c
            sr = bu_re[:, c * LANE:(c + 1) * LANE].reshape(n_blk, SUBLANE, LANE)
            si = bu_im[:, c * LANE:(c + 1) * LANE].reshape(n_blk, SUBLANE, LANE)
            pr = pre_ref[idx]
            pi = pim_ref[idx]
            for k in range(S5_BLOCK_LEVELS):
                qr = pltpu.roll(sr, 1 << k, 1)
                qi = pltpu.roll(si, 1 << k, 1)
                sr, si = sr + (pr[k] * qr - pi[k] * qi), si + (pr[k] * qi + pi[k] * qr)
            cr = jnp.broadcast_to(carry_re[idx], (SUBLANE, LANE))
            ci = jnp.broadcast_to(carry_im[idx], (SUBLANE, LANE))
            ar = pr[S5_BLOCK_LEVELS]
            ai = pi[S5_BLOCK_LEVELS]
            blocks_re = []
            blocks_im = []
            for b in range(n_blk):
                br = sr[b] + (ar * cr - ai * ci)
                bi = si[b] + (ar * ci + ai * cr)
                blocks_re.append(br)
                blocks_im.append(bi)
                cr = jnp.broadcast_to(br[SUBLANE - 1:SUBLANE], (SUBLANE, LANE))
                ci = jnp.broadcast_to(bi[SUBLANE - 1:SUBLANE], (SUBLANE, LANE))
            carry_re[idx] = cr[0:1]
            carry_im[idx] = ci[0:1]
            tiles_re.append(jnp.concatenate(blocks_re, axis=0).astype(BF16))
            tiles_im.append(jnp.concatenate(blocks_im, axis=0).astype(BF16))
        s_re = jnp.concatenate(tiles_re, axis=1)
        s_im = jnp.concatenate(tiles_im, axis=1)
        ys.append(jnp.dot(s_re, cre_ref[j], preferred_element_type=F32)
                  - jnp.dot(s_im, cim_ref[j], preferred_element_type=F32))
    y = jnp.concatenate(ys, axis=1) + d_ref[...] * u
    y = y * (0.5 * (1.0 + jnp.tanh(math.sqrt(2.0 / math.pi) * (y + 0.044715 * (y * y * y)))))
    y = y * _sigmoid(_dot(y, gw_ref[...]) + gb_ref[...])
    o_ref[...] = y.astype(o_ref.dtype)


def _s5_call(z, bre, bim, pre, pim, cre, cim, d, gw, gb):
    length, width = z.shape
    n_slab = bre.shape[0]
    n_tiles = n_slab * (S5_SLAB_STATES // LANE)
    full = lambda a: pl.BlockSpec(a.shape, lambda i: (0,) * a.ndim)
    return pl.pallas_call(
        _s5_kernel,
        out_shape=jax.ShapeDtypeStruct((length, width), BF16),
        grid=(length // S5_T,),
        in_specs=[pl.BlockSpec((S5_T, width), lambda i: (i, 0)),
                  full(bre), full(bim), full(pre), full(pim), full(cre), full(cim),
                  full(d), full(gw), full(gb)],
        out_specs=pl.BlockSpec((S5_T, width), lambda i: (i, 0)),
        scratch_shapes=[pltpu.VMEM((n_tiles, 1, LANE), F32),
                        pltpu.VMEM((n_tiles, 1, LANE), F32)],
        compiler_params=_params(("arbitrary",)),
        name="s5_mixer",
    )(z, bre, bim, pre, pim, cre, cim, d, gw, gb)


def _s5_tables(lam_re, lam_im, log_step, b_re, b_im, c_re, c_im):
    groups = lam_re.shape[0]
    n_slab = groups * S5_GROUP // S5_SLAB
    gps = S5_SLAB // S5_GROUP
    lr = jnp.minimum(lam_re.astype(F32), -1e-4)
    li = lam_im.astype(F32)
    dt = jnp.exp(log_step.astype(F32))[:, None]
    e = jnp.exp(lr * dt)
    lb_re = e * jnp.cos(li * dt)
    lb_im = e * jnp.sin(li * dt)
    den = lr * lr + li * li
    f_re = ((lb_re - 1.0) * lr + lb_im * li) / den
    f_im = (lb_im * lr - (lb_re - 1.0) * li) / den
    bb_re = f_re[..., None] * b_re - f_im[..., None] * b_im
    bb_im = f_re[..., None] * b_im + f_im[..., None] * b_re
    eye = jnp.eye(gps, dtype=F32)

    def bd_b(m):
        m = m.reshape(n_slab, gps, S5_STATE, S5_GROUP)
        return jnp.einsum('jgpi,gh->jgihp', m, eye).reshape(n_slab, S5_SLAB, S5_SLAB_STATES).astype(BF16)

    def bd_c(m):
        m = m.reshape(n_slab, gps, S5_GROUP, S5_STATE)
        return jnp.einsum('jgip,gh->jgphi', m, eye).reshape(n_slab, S5_SLAB_STATES, S5_SLAB).astype(BF16)

    r_idx = jnp.arange(SUBLANE)
    steps = 2 ** jnp.arange(S5_BLOCK_LEVELS)
    expo = jnp.concatenate([jnp.broadcast_to(steps[:, None], (S5_BLOCK_LEVELS, SUBLANE)),
                            (r_idx + 1)[None, :]], axis=0).astype(F32)
    keep = jnp.concatenate([r_idx[None, :] >= steps[:, None],
                            jnp.ones((1, SUBLANE), bool)], axis=0).astype(F32)
    ph = (li * dt).reshape(-1) * expo[..., None]
    mag = keep[..., None] * jnp.exp((lr * dt).reshape(-1) * expo[..., None])
    n_tiles = ph.shape[-1] // LANE
    tile_major = lambda a: a.reshape(a.shape[0], SUBLANE, n_tiles, LANE).transpose(2, 0, 1, 3)
    pw_re = tile_major(mag * jnp.cos(ph))
    pw_im = tile_major(mag * jnp.sin(ph))
    return (bd_b(bb_re), bd_b(bb_im), pw_re, pw_im,
            bd_c(c_re.astype(F32)), bd_c(c_im.astype(F32)))


def _rwkv_kernel(has_vres, *refs):
    if has_vres:
        (z_ref, vf_ref, mu_ref, wl_ref, w0_ref, al_ref, a0_ref, gl_ref, kk_ref, ka_ref, rk_ref,
         lw_ref, lb_ref, seg_ref, segt_ref, vb_ref, vbias_ref, o_ref, prev_ref, h_ref) = refs
    else:
        (z_ref, mu_ref, wl_ref, w0_ref, al_ref, a0_ref, gl_ref, kk_ref, ka_ref, rk_ref,
         lw_ref, lb_ref, seg_ref, segt_ref, o_ref, vo_ref, prev_ref, h_ref) = refs
    tb = z_ref.shape[0]
    t = RWKV_T
    n_chunk = tb // t
    width = o_ref.shape[1]
    n_pair = width // LANE

    @pl.when(pl.program_id(0) == 0)
    def _():
        prev_ref[...] = jnp.zeros_like(prev_ref)
        h_ref[...] = jnp.zeros_like(h_ref)

    z = z_ref[...]
    rowz = lax.broadcasted_iota(jnp.int32, z.shape, 0)
    prev = jnp.where(rowz == 0, prev_ref[...], pltpu.roll(z, 1, 0))
    prev_ref[...] = z[tb - 1:tb]
    zs = z + (prev - z) * mu_ref[...]

    o1 = 3 * width
    r = zs[:, 0:width]
    k = zs[:, width:2 * width]
    v = zs[:, 2 * width:o1]
    w_in = zs[:, o1:o1 + LANE]
    a_in = zs[:, o1 + LANE:o1 + 2 * LANE]
    g_in = zs[:, o1 + 2 * LANE:o1 + 2 * LANE + RWKV_GATE_LORA]

    wpre = w0_ref[...] + _dot_hp(jnp.tanh(w_in), wl_ref[...])
    logw = -jnp.exp(-_softplus(-wpre) - 0.5)
    a = _sigmoid(a0_ref[...] + _dot(a_in, al_ref[...]))
    g = _dot(_sigmoid(g_in), gl_ref[...])
    if has_vres:
        vr = zs[:, o1 + 2 * LANE + RWKV_GATE_LORA:]
        vg = _sigmoid(vbias_ref[...] + _dot(vr, vb_ref[...]))
        v = v + (vf_ref[...] - v) * vg
    else:
        vo_ref[...] = v

    seg = seg_ref[...]
    seg_t = segt_ref[...]
    kk = k * kk_ref[...]
    kk = kk * lax.rsqrt(jnp.maximum(_seg_sum(kk * kk, seg, seg_t), 1e-24))
    k2 = k * (1.0 + (a - 1.0) * ka_ref[...])
    kka = kk * a

    shift = t.bit_length() - 1
    ri = lax.broadcasted_iota(jnp.int32, (tb, tb), 0)
    ci = lax.broadcasted_iota(jnp.int32, (tb, tb), 1)
    tri = jnp.where((ci <= ri) & ((ri >> shift) == (ci >> shift)), 1.0, 0.0).astype(BF16)
    lc = _dot_exact_lhs(tri, logw)

    gi = lax.broadcasted_iota(jnp.int32, (4 * t, 4 * t), 0)
    gj = lax.broadcasted_iota(jnp.int32, (4 * t, 4 * t), 1)
    ti = gi & (t - 1)
    sj = gj & (t - 1)
    keep = sj + jnp.where(gi < 2 * t, 1, 0) <= ti
    lane = lax.broadcasted_iota(jnp.int32, (t, LANE), 1)
    head0 = lane < RWKV_HEAD

    def stack2(x):
        return jnp.concatenate([jnp.where(head0, x, 0.0), jnp.where(head0, 0.0, x)], axis=0)

    e_pos = jnp.exp(lc)
    e_neg = jnp.exp(-lc)
    e_prev = jnp.exp(lc - logw)
    rt_all = r * e_pos
    at_all = -kk * e_prev
    bt_all = kka * e_neg
    kt_all = k2 * e_neg

    items = [(c, p) for c in range(n_chunk) for p in range(n_pair)]
    tile = lambda x, c, p: x[c * t:(c + 1) * t, p * LANE:(p + 1) * LANE]
    at = [tile(at_all, c, p) for c, p in items]
    rt = [tile(rt_all, c, p) for c, p in items]
    vbd = [stack2(tile(v, c, p)) for c, p in items]
    gm = [jnp.where(keep, _dot_nt(jnp.concatenate([stack2(a_), stack2(r_)], axis=0),
                                  jnp.concatenate([stack2(tile(bt_all, c, p)), stack2(tile(kt_all, c, p))],
                                                  axis=0)), 0.0)
          for a_, r_, (c, p) in zip(at, rt, items)]
    n1 = [g_[0:2 * t, 0:2 * t] for g_ in gm]
    rhs0 = [_dot(g_[0:2 * t, 2 * t:4 * t], v_) for g_, v_ in zip(gm, vbd)]
    qi = lax.broadcasted_iota(jnp.int32, (2 * t, 2 * t), 0)
    qj = lax.broadcasted_iota(jnp.int32, (2 * t, 2 * t), 1)
    xinv = [jnp.where(qi == qj, 1.0, jnp.where((qi ^ qj) == 1, n_, 0.0)) for n_ in n1]
    for lvl in range(1, shift):
        couple = ((qi >> lvl) ^ (qj >> lvl)) == 1
        xm = [_dot(x_, jnp.where(couple, n_, 0.0)) for x_, n_ in zip(xinv, n1)]
        xinv = [x_ + _dot(m_, x_) for x_, m_ in zip(xinv, xm)]

    hts = [h_ref[p] for p in range(n_pair)]
    out_rows = []
    for c in range(n_chunk):
        idx = [c * n_pair + p for p in range(n_pair)]
        lc_c = lc[c * t:(c + 1) * t]
        e_rem = jnp.exp(lc_c[t - 1:t] - lc_c)
        bh_c = kka[c * t:(c + 1) * t] * e_rem
        kh_c = k2[c * t:(c + 1) * t] * e_rem
        e_last = e_pos[(c + 1) * t - 1:(c + 1) * t]
        ph = [_dot_nt(jnp.concatenate([at[i], rt[i]], axis=0), h_) for i, h_ in zip(idx, hts)]
        u = [_dot(xinv[i], stack2(p_[0:t]) + rhs0[i]) for i, p_ in zip(idx, ph)]
        uv = [jnp.concatenate([u_, vbd[i]], axis=0) for i, u_ in zip(idx, u)]
        opk = [stack2(p_[t:2 * t]) + _dot(gm[i][2 * t:4 * t, :], uv_) for i, p_, uv_ in zip(idx, ph, uv)]
        out_rows.append(jnp.concatenate([o_[0:t] + o_[t:2 * t] for o_ in opk], axis=1))
        hts = [h_ * e_last[:, p * LANE:(p + 1) * LANE]
               + _dot_tn(uv_, jnp.concatenate([stack2(bh_c[:, p * LANE:(p + 1) * LANE]),
                                               stack2(kh_c[:, p * LANE:(p + 1) * LANE])], axis=0))
               for p, (h_, uv_) in enumerate(zip(hts, uv))]
    for p in range(n_pair):
        h_ref[p] = hts[p]

    y = jnp.concatenate(out_rows, axis=0)
    inv_n = 1.0 / RWKV_HEAD
    mean = _seg_sum(y, seg, seg_t) * inv_n
    yc = y - mean
    var = _seg_sum(yc * yc, seg, seg_t) * inv_n
    yn = yc * lax.rsqrt(var + RWKV_LNX_EPS) * lw_ref[...] + lb_ref[...]
    bonus = _seg_sum(r * k2 * rk_ref[...], seg, seg_t) * v
    o_ref[...] = ((yn + bonus) * g).astype(o_ref.dtype)


def _rwkv_call(z, v_first, mu, wl, w0, al, a0, gl, k_k, k_a, r_k, lnx_w, lnx_b, seg, seg_t, vb, vbias):
    length, zc = z.shape
    width = w0.shape[1]
    has_vres = v_first is not None
    full = lambda a: pl.BlockSpec(a.shape, lambda i: (0,) * a.ndim)
    rows = lambda c: pl.BlockSpec((RWKV_BLOCK, c), lambda i: (i, 0))
    common = [mu, wl, w0, al, a0, gl, k_k, k_a, r_k, lnx_w, lnx_b, seg, seg_t]
    if has_vres:
        args = [z, v_first] + common + [vb, vbias]
        in_specs = [rows(zc), rows(width)] + [full(a) for a in common + [vb, vbias]]
        out_shape = jax.ShapeDtypeStruct((length, width), BF16)
        out_specs = rows(width)
    else:
        args = [z] + common
        in_specs = [rows(zc)] + [full(a) for a in common]
        out_shape = (jax.ShapeDtypeStruct((length, width), BF16),
                     jax.ShapeDtypeStruct((length, width), F32))
        out_specs = (rows(width), rows(width))
    return pl.pallas_call(
        functools.partial(_rwkv_kernel, has_vres),
        out_shape=out_shape,
        grid=(length // RWKV_BLOCK,),
        in_specs=in_specs,
        out_specs=out_specs,
        scratch_shapes=[pltpu.VMEM((1, zc), F32),
                        pltpu.VMEM((width // LANE, LANE, LANE), F32)],
        compiler_params=_params(("arbitrary",)),
        name="rwkv7_mixer",
    )(*args)


def _gla_kernel(z_ref, al_ref, ab_ref, ng_ref, o_ref, st_ref):
    tb = z_ref.shape[0]
    t = GLA_T
    n_chunk = tb // t
    width = o_ref.shape[1]
    n_head = width // GLA_DV

    @pl.when(pl.program_id(0) == 0)
    def _():
        st_ref[...] = jnp.zeros_like(st_ref)

    z = z_ref[...]
    q = z[:, 0:width] * (GLA_DK ** -0.5)
    k = z[:, width:2 * width]
    v = z[:, 2 * width:3 * width]
    g = z[:, 3 * width:4 * width]
    a_in = z[:, 4 * width:]
    x = _dot_hp(a_in, al_ref[...]) + ab_ref[...]
    log_a = -_softplus(-x) * (1.0 / GLA_TAU)

    shift = t.bit_length() - 1
    ri = lax.broadcasted_iota(jnp.int32, (tb, tb), 0)
    ci = lax.broadcasted_iota(jnp.int32, (tb, tb), 1)
    tri = jnp.where((ci <= ri) & ((ri >> shift) == (ci >> shift)), 1.0, 0.0).astype(BF16)
    b = _dot_exact_lhs(tri, log_a)
    causal = (lax.broadcasted_iota(jnp.int32, (t, t), 1) <= lax.broadcasted_iota(jnp.int32, (t, t), 0))

    q_in = q * jnp.exp(b)
    heads = range(n_head)
    head = lambda x, h: x[:, h * GLA_DV:(h + 1) * GLA_DV]
    chunk = lambda x, c: x[c * t:(c + 1) * t]
    v_c = [chunk(v, c) for c in range(n_chunk)]
    k_rem = []
    e_last = []
    intra = []
    for c in range(n_chunk):
        b_c = chunk(b, c)
        b_mid = b_c[t // 2:t // 2 + 1]
        b_last = b_c[t - 1:t]
        q_mid = chunk(q, c) * jnp.exp(b_c - b_mid)
        k_mid = chunk(k, c) * jnp.exp(b_mid - b_c)
        k_rem.append(chunk(k, c) * jnp.exp(b_last - b_c))
        e_last.append(jnp.exp(b_last))
        intra.append([_dot(jnp.where(causal, _dot_nt(head(q_mid, h), head(k_mid, h)), 0.0), head(v_c[c], h))
                      for h in heads])
    sts = [st_ref[h] for h in heads]
    out_rows = []
    for c in range(n_chunk):
        q_c = chunk(q_in, c)
        o = [i_ + _dot_nt(head(q_c, h), s_) for h, (i_, s_) in enumerate(zip(intra[c], sts))]
        sts = [s_ * head(e_last[c], h) + _dot_tn(head(v_c[c], h), head(k_rem[c], h))
               for h, s_ in enumerate(sts)]
        out_rows.append(jnp.concatenate(
            [o_ * lax.rsqrt(jnp.mean(o_ * o_, axis=-1, keepdims=True) + NORM_EPS) for o_ in o], axis=1))
    for h in heads:
        st_ref[h] = sts[h]
    o = jnp.concatenate(out_rows, axis=0)
    o_ref[...] = (o * ng_ref[...] * (g * _sigmoid(g))).astype(o_ref.dtype)


def _gla_call(z, al, ab, ng):
    length, zc = z.shape
    width = ng.shape[1]
    full = lambda a: pl.BlockSpec(a.shape, lambda i: (0,) * a.ndim)
    return pl.pallas_call(
        _gla_kernel,
        out_shape=jax.ShapeDtypeStruct((length, width), BF16),
        grid=(length // GLA_BLOCK,),
        in_specs=[pl.BlockSpec((GLA_BLOCK, zc), lambda i: (i, 0)), full(al), full(ab), full(ng)],
        out_specs=pl.BlockSpec((GLA_BLOCK, width), lambda i: (i, 0)),
        scratch_shapes=[pltpu.VMEM((width // GLA_DV, GLA_DV, LANE), F32)],
        compiler_params=_params(("arbitrary",)),
        name="gla_mixer",
    )(z, al, ab, ng)


def _merge_kernel(ya_ref, yb_ref, yc_ref, gate_ref, x_ref, wua_ref, wub_ref, wuc_ref, wo_ref, g_ref,
                  xo_ref, h_ref):
    d = x_ref.shape[1]
    gates = gate_ref[...].astype(F32)
    merged = (gates[:, 0:d] * jnp.dot(ya_ref[...], wua_ref[...], preferred_element_type=F32)
              + gates[:, d:2 * d] * jnp.dot(yb_ref[...], wub_ref[...], preferred_element_type=F32)
              + gates[:, 2 * d:3 * d] * jnp.dot(yc_ref[...], wuc_ref[...], preferred_element_type=F32))
    x = x_ref[...] + _dot(merged, wo_ref[...])
    xo_ref[...] = x
    h_ref[...] = _rms_rows(x, g_ref[...]).astype(h_ref.dtype)


def _merge_call(ya, yb, yc, gates, x, wua, wub, wuc, wo, g, tm=256):
    m, d = x.shape
    rows = lambda a: pl.BlockSpec((tm, a.shape[1]), lambda i: (i, 0))
    const = lambda a: pl.BlockSpec(a.shape, lambda i: (0,) * a.ndim, pipeline_mode=pl.Buffered(1))
    return pl.pallas_call(
        _merge_kernel,
        out_shape=(jax.ShapeDtypeStruct((m, d), F32), jax.ShapeDtypeStruct((m, d), BF16)),
        grid=(m // tm,),
        in_specs=[rows(ya), rows(yb), rows(yc), rows(gates), rows(x),
                  const(wua), const(wub), const(wuc), const(wo), const(g)],
        out_specs=(pl.BlockSpec((tm, d), lambda i: (i, 0)), pl.BlockSpec((tm, d), lambda i: (i, 0))),
        compiler_params=_params(("parallel",)),
        name="merge_out_proj",
    )(ya, yb, yc, gates, x, wua, wub, wuc, wo, g)


def _mlp_kernel(final, h_ref, w1_ref, w2_ref, x_ref, g_ref, xo_ref, *rest):
    f = pl.program_id(1)

    @pl.when(f == 0)
    def _():
        xo_ref[...] = x_ref[...]

    hid = jnp.maximum(jnp.dot(h_ref[...], w1_ref[...], preferred_element_type=F32), 0.0)
    xo_ref[...] += _dot(hid * hid, w2_ref[...])

    @pl.when(f == pl.num_programs(1) - 1)
    def _():
        y = _rms_rows(xo_ref[...], g_ref[...])
        if final:
            xo_ref[...] = y
        else:
            rest[0][...] = y.astype(rest[0].dtype)


def _mlp_call(h, w1, w2, x, g_next, final, tm=512, tf=1024):
    m, d = x.shape
    ff = w1.shape[1]
    row_blk = pl.BlockSpec((tm, d), lambda i, f: (i, 0))
    out_shape = [jax.ShapeDtypeStruct((m, d), F32)]
    out_specs = [row_blk]
    if not final:
        out_shape.append(jax.ShapeDtypeStruct((m, d), BF16))
        out_specs.append(row_blk)
    return pl.pallas_call(
        functools.partial(_mlp_kernel, final),
        out_shape=tuple(out_shape),
        grid=(m // tm, ff // tf),
        in_specs=[row_blk,
                  pl.BlockSpec((d, tf), lambda i, f: (0, f)),
                  pl.BlockSpec((tf, d), lambda i, f: (f, 0)),
                  row_blk,
                  pl.BlockSpec((1, d), lambda i, f: (0, 0))],
        out_specs=tuple(out_specs),
        compiler_params=_params(("parallel", "arbitrary")),
        name="mlp",
    )(h, w1, w2, x, g_next)


def _pad_cols(a, n):
    return jnp.pad(a, ((0, 0), (0, n - a.shape[1])))


def _pad_rows(a, n):
    return jnp.pad(a, ((0, n - a.shape[0]), (0, 0)))


def _head_pad(a, heads, dh, to):
    lead = a.shape[:-1]
    a = a.reshape(lead + (heads, dh))
    a = jnp.pad(a, [(0, 0)] * len(lead) + [(0, 0), (0, to - dh)])
    return a.reshape(lead + (heads * to,))


def kernel(x, norm_mix, w_in, gate_bias, s5_lambda_re, s5_lambda_im, s5_log_step, s5_b_re, s5_b_im, s5_c_re, s5_c_im, s5_d, s5_glu_w, s5_glu_b, rwkv_mu, rwkv_w_lora, rwkv_w0, rwkv_a_lora, rwkv_a0, rwkv_g_lora, rwkv_k_k, rwkv_k_a, rwkv_r_k, rwkv_lnx_w, rwkv_lnx_b, rwkv_vres_a, rwkv_vres_mu, rwkv_vres_b, rwkv_vres_bias, gla_alpha_lora, gla_alpha_bias, gla_norm_g, w_up, w_out, norm_mlp, mlp_w1, mlp_w2, final_norm):
    bsz, length, d = x.shape
    depth = w_in.shape[0]
    s5_w = s5_d.shape[1]
    rw_w = rwkv_w0.shape[1]
    gla_v = gla_norm_g.shape[1]
    gla_heads = gla_v // GLA_DV
    gla_k = gla_heads * GLA_DK
    rw_cols = 3 * rw_w + RWKV_DECAY_LORA + RWKV_AAA_LORA + RWKV_GATE_LORA
    gla_cols = 2 * gla_k + 2 * gla_v + GLA_LORA
    o_rw = s5_w
    o_gla = o_rw + rw_cols
    o_gate = o_gla + gla_cols
    o1 = 3 * rw_w

    head_id = jnp.arange(rw_w) // RWKV_HEAD
    seg = (head_id[:, None] == jnp.arange(LANE)[None, :]).astype(BF16)
    seg_t = seg.T
    row1 = lambda a: a.reshape(1, -1).astype(F32)

    outs = []
    for b in range(bsz):
        xb = x[b].astype(F32)
        u = _norm_call(xb, row1(norm_mix[0]))
        v_first = None
        for l in range(depth):
            wl = w_in[l]
            w_s5 = wl[:, :o_rw].astype(BF16)
            wr = wl[:, o_rw:o_gla]
            pieces = [wr[:, :o1],
                      _pad_cols(wr[:, o1:o1 + RWKV_DECAY_LORA], LANE),
                      wr[:, o1 + RWKV_DECAY_LORA:o1 + RWKV_DECAY_LORA + RWKV_AAA_LORA],
                      wr[:, o1 + RWKV_DECAY_LORA + RWKV_AAA_LORA:]]
            mu = rwkv_mu[l]
            mu_pieces = [mu[:o1], jnp.pad(mu[o1:o1 + RWKV_DECAY_LORA], (0, LANE - RWKV_DECAY_LORA)),
                         mu[o1 + RWKV_DECAY_LORA:]]
            if l > 0:
                pieces.append(_pad_cols(rwkv_vres_a[l - 1], LANE))
                mu_pieces.append(jnp.pad(rwkv_vres_mu[l - 1], (0, LANE - RWKV_MV_LORA)))
            else:
                pieces.append(jnp.zeros((d, LANE), wl.dtype))
                mu_pieces.append(jnp.zeros((LANE,), mu.dtype))
            w_rw = jnp.concatenate(pieces, axis=1).astype(BF16)
            mu_rw = row1(jnp.concatenate(mu_pieces))
            wg = wl[:, o_gla:o_gate]
            w_gla = jnp.concatenate([
                _head_pad(wg[:, :gla_k], gla_heads, GLA_DK, GLA_DV),
                _head_pad(wg[:, gla_k:2 * gla_k], gla_heads, GLA_DK, GLA_DV),
                wg[:, 2 * gla_k:2 * gla_k + 2 * gla_v],
                _pad_cols(wg[:, 2 * gla_k + 2 * gla_v:], LANE)], axis=1).astype(BF16)
            w_gate = wl[:, o_gate:].astype(BF16)

            z_s5 = _mm_call(u, w_s5, tn=s5_w, name="in_proj_s5")
            z_rw = _mm_call(u, w_rw, tn=w_rw.shape[1] // 2, name="in_proj_rwkv")
            z_gla = _mm_call(u, w_gla, tn=w_gla.shape[1] // 3, name="in_proj_gla")
            gates = _mm_call(u, w_gate, tn=1024, bias=row1(gate_bias[l]), name="in_proj_gate")

            tabs = _s5_tables(s5_lambda_re[l], s5_lambda_im[l], s5_log_step[l],
                              s5_b_re[l].astype(F32), s5_b_im[l].astype(F32), s5_c_re[l], s5_c_im[l])
            y_a = _s5_call(z_s5, *tabs, row1(s5_d[l]), s5_glu_w[l].astype(BF16), row1(s5_glu_b[l]))

            wlo = _pad_rows(rwkv_w_lora[l].astype(F32), LANE)
            if l > 0:
                vb = _pad_rows(rwkv_vres_b[l - 1], LANE).astype(BF16)
                vbias = row1(rwkv_vres_bias[l - 1])
            else:
                vb = vbias = None
            res = _rwkv_call(z_rw, v_first, mu_rw, wlo, row1(rwkv_w0[l]),
                             rwkv_a_lora[l].astype(BF16), row1(rwkv_a0[l]), rwkv_g_lora[l].astype(BF16),
                             row1(rwkv_k_k[l]), row1(rwkv_k_a[l]), row1(rwkv_r_k[l]),
                             row1(rwkv_lnx_w[l]), row1(rwkv_lnx_b[l]), seg, seg_t, vb, vbias)
            if l == 0:
                y_b, v_first = res
            else:
                y_b = res

            al = _pad_rows(_head_pad(gla_alpha_lora[l].astype(F32), gla_heads, GLA_DK, GLA_DV), LANE)
            ab = row1(_head_pad(gla_alpha_bias[l], gla_heads, GLA_DK, GLA_DV))
            y_c = _gla_call(z_gla, al, ab, row1(gla_norm_g[l]))

            wu = w_up[l].astype(BF16)
            x_mid, h = _merge_call(y_a, y_b, y_c, gates, xb,
                                   wu[:s5_w], wu[s5_w:s5_w + rw_w], wu[s5_w + rw_w:],
                                   w_out[l].astype(BF16), row1(norm_mlp[l]))
            final = l == depth - 1
            g_next = row1(final_norm if final else norm_mix[l + 1])
            res = _mlp_call(h, mlp_w1[l].astype(BF16), mlp_w2[l].astype(BF16), x_mid, g_next, final)
            if final:
                xb = res[0]
            else:
                xb, u = res
        outs.append(xb.astype(x.dtype))
    return jnp.stack(outs, axis=0)
```

```python
import functools
import math

import jax
import jax.numpy as jnp
from jax import lax
from jax.experimental import pallas as pl
from jax.experimental.pallas import tpu as pltpu

F32 = jnp.float32
BF16 = jnp.bfloat16

LANE = 128
SUBLANE = 8
NORM_EPS = 1e-6

S5_GROUP = 16
S5_STATE = 64
S5_SLAB = 256
S5_SLAB_STATES = (S5_SLAB // S5_GROUP) * S5_STATE
S5_T = 128
S5_BLOCK_LEVELS = 3

RWKV_HEAD = 64
RWKV_T = 64
RWKV_BLOCK = 256
RWKV_DECAY_LORA = 96
RWKV_AAA_LORA = 128
RWKV_GATE_LORA = 256
RWKV_MV_LORA = 64
RWKV_LNX_EPS = 64e-5

GLA_DK = 64
GLA_DV = 128
GLA_LORA = 16
GLA_TAU = 16.0
GLA_T = 64
GLA_BLOCK = 256

VMEM_LIMIT = 56 * 1024 * 1024


def _dot(a, b):
    return jnp.dot(a.astype(BF16), b.astype(BF16), preferred_element_type=F32)


def _dot_nt(a, b):
    return lax.dot_general(a.astype(BF16), b.astype(BF16), (((1,), (1,)), ((), ())),
                           preferred_element_type=F32)


def _dot_tn(a, b):
    return lax.dot_general(a.astype(BF16), b.astype(BF16), (((0,), (0,)), ((), ())),
                           preferred_element_type=F32)


def _split3(x):
    hi = x.astype(BF16)
    r1 = x - hi.astype(F32)
    mid = r1.astype(BF16)
    lo = (r1 - mid.astype(F32)).astype(BF16)
    return hi, mid, lo


def _dot_exact_lhs(m_bf16, x):
    hi, mid, lo = _split3(x)
    return (jnp.dot(m_bf16, hi, preferred_element_type=F32)
            + jnp.dot(m_bf16, mid, preferred_element_type=F32)
            + jnp.dot(m_bf16, lo, preferred_element_type=F32))


def _hi_lo(x):
    hi = x.astype(BF16)
    return hi, (x - hi.astype(F32)).astype(BF16)


def _seg_sum(x, seg, seg_t):
    s = jnp.dot(x.astype(BF16), seg, preferred_element_type=F32)
    hi, lo = _hi_lo(s)
    return jnp.dot(hi, seg_t, preferred_element_type=F32) + jnp.dot(lo, seg_t, preferred_element_type=F32)


def _split_weight(w):
    hi, lo = _hi_lo(w.astype(F32))
    return jnp.stack([hi, lo])


def _dot_hp(a, w_ref):
    a_hi, a_lo = _hi_lo(a)
    return (jnp.dot(a_hi, w_ref[0], preferred_element_type=F32)
            + jnp.dot(a_lo, w_ref[0], preferred_element_type=F32)
            + jnp.dot(a_hi, w_ref[1], preferred_element_type=F32))


def _sigmoid(x):
    return 1.0 / (1.0 + jnp.exp(-x))


def _softplus(x):
    return jnp.maximum(x, 0.0) + jnp.log(1.0 + jnp.exp(-jnp.abs(x)))


def _rms_rows(x, g):
    return x * lax.rsqrt(jnp.mean(x * x, axis=-1, keepdims=True) + NORM_EPS) * g


def _shift_rows(x, d):
    n = x.shape[0]
    if d % 8 == 0:
        return jnp.concatenate([jnp.zeros((d, x.shape[1]), x.dtype), x[:n - d]], axis=0)
    rolled = pltpu.roll(x, d, 0)
    row = lax.broadcasted_iota(jnp.int32, x.shape, 0)
    return jnp.where(row >= d, rolled, 0.0)


def _params(sem):
    return pltpu.CompilerParams(dimension_semantics=sem, vmem_limit_bytes=VMEM_LIMIT)


def _norm_kernel(x_ref, g_ref, o_ref):
    o_ref[...] = _rms_rows(x_ref[...], g_ref[...]).astype(o_ref.dtype)


def _norm_call(x, g, tm=512):
    m, d = x.shape
    return pl.pallas_call(
        _norm_kernel,
        out_shape=jax.ShapeDtypeStruct((m, d), BF16),
        grid=(m // tm,),
        in_specs=[pl.BlockSpec((tm, d), lambda i: (i, 0)),
                  pl.BlockSpec((1, d), lambda i: (0, 0))],
        out_specs=pl.BlockSpec((tm, d), lambda i: (i, 0)),
        compiler_params=_params(("parallel",)),
        name="rmsnorm",
    )(x, g)


def _mm_kernel(a_ref, w_ref, o_ref):
    o_ref[...] = jnp.dot(a_ref[...], w_ref[...], preferred_element_type=F32).astype(o_ref.dtype)


def _mm_gate_kernel(a_ref, w_ref, b_ref, o_ref):
    z = jnp.dot(a_ref[...], w_ref[...], preferred_element_type=F32) + b_ref[...]
    o_ref[...] = _sigmoid(z).astype(o_ref.dtype)


def _mm_call(a, w, tn, bias=None, tm=1024, name="in_proj"):
    m, k = a.shape
    n = w.shape[1]
    in_specs = [pl.BlockSpec((tm, k), lambda i, j: (i, 0)),
                pl.BlockSpec((k, tn), lambda i, j: (0, j))]
    args = [a, w]
    if bias is None:
        body, out_dtype = _mm_kernel, F32
    else:
        body, out_dtype = _mm_gate_kernel, BF16
        in_specs.append(pl.BlockSpec((1, tn), lambda i, j: (0, j)))
        args.append(bias)
    return pl.pallas_call(
        body,
        out_shape=jax.ShapeDtypeStruct((m, n), out_dtype),
        grid=(m // tm, n // tn),
        in_specs=in_specs,
        out_specs=pl.BlockSpec((tm, tn), lambda i, j: (i, j)),
        compiler_params=_params(("parallel", "parallel")),
        name=name,
    )(*args)


def _s5_kernel(z_ref, bre_ref, bim_ref, pre_ref, pim_ref, cre_ref, cim_ref, d_ref, gw_ref, gb_ref,
               o_ref, carry_re, carry_im):
    n_slab = bre_ref.shape[0]
    tiles_per_slab = S5_SLAB_STATES // LANE
    t = z_ref.shape[0]

    @pl.when(pl.program_id(0) == 0)
    def _():
        carry_re[...] = jnp.zeros_like(carry_re)
        carry_im[...] = jnp.zeros_like(carry_im)

    u = z_ref[...]
    ub = u.astype(BF16)
    n_blk = t // SUBLANE
    ys = []
    for j in range(n_slab):
        uj = ub[:, j * S5_SLAB:(j + 1) * S5_SLAB]
        bu_re = jnp.dot(uj, bre_ref[j], preferred_element_type=F32)
        bu_im = jnp.dot(uj, bim_ref[j], preferred_element_type=F32)
        tiles_re = []
        tiles_im = []
        for c in range(tiles_per_slab):
            idx = j * tiles_per_slab + c
            sr = bu_re[:, c * LANE:(c + 1) * LANE].reshape(n_blk, SUBLANE, LANE)
            si = bu_im[:, c * LANE:(c + 1) * LANE].reshape(n_blk, SUBLANE, LANE)
            pr = pre_ref[idx]
            pi = pim_ref[idx]
            for k in range(S5_BLOCK_LEVELS):
                qr = pltpu.roll(sr, 1 << k, 1)
                qi = pltpu.roll(si, 1 << k, 1)
                sr, si = sr + (pr[k] * qr - pi[k] * qi), si + (pr[k] * qi + pi[k] * qr)
            cr = jnp.broadcast_to(carry_re[idx], (SUBLANE, LANE))
            ci = jnp.broadcast_to(carry_im[idx], (SUBLANE, LANE))
            ar = pr[S5_BLOCK_LEVELS]
            ai = pi[S5_BLOCK_LEVELS]
            blocks_re = []
            blocks_im = []
            for b in range(n_blk):
                br = sr[b] + (ar * cr - ai * ci)
                bi = si[b] + (ar * ci + ai * cr)
                blocks_re.append(br)
                blocks_im.append(bi)
                cr = jnp.broadcast_to(br[SUBLANE - 1:SUBLANE], (SUBLANE, LANE))
                ci = jnp.broadcast_to(bi[SUBLANE - 1:SUBLANE], (SUBLANE, LANE))
            carry_re[idx] = cr[0:1]
            carry_im[idx] = ci[0:1]
            tiles_re.append(jnp.concatenate(blocks_re, axis=0).astype(BF16))
            tiles_im.append(jnp.concatenate(blocks_im, axis=0).astype(BF16))
        s_re = jnp.concatenate(tiles_re, axis=1)
        s_im = jnp.concatenate(tiles_im, axis=1)
        ys.append(jnp.dot(s_re, cre_ref[j], preferred_element_type=F32)
                  - jnp.dot(s_im, cim_ref[j], preferred_element_type=F32))
    y = jnp.concatenate(ys, axis=1) + d_ref[...] * u
    y = y * (0.5 * (1.0 + jnp.tanh(math.sqrt(2.0 / math.pi) * (y + 0.044715 * (y * y * y)))))
    y = y * _sigmoid(_dot(y, gw_ref[...]) + gb_ref[...])
    o_ref[...] = y.astype(o_ref.dtype)


def _s5_call(z, bre, bim, pre, pim, cre, cim, d, gw, gb):
    length, width = z.shape
    n_slab = bre.shape[0]
    n_tiles = n_slab * (S5_SLAB_STATES // LANE)
    full = lambda a: pl.BlockSpec(a.shape, lambda i: (0,) * a.ndim)
    return pl.pallas_call(
        _s5_kernel,
        out_shape=jax.ShapeDtypeStruct((length, width), BF16),
        grid=(length // S5_T,),
        in_specs=[pl.BlockSpec((S5_T, width), lambda i: (i, 0)),
                  full(bre), full(bim), full(pre), full(pim), full(cre), full(cim),
                  full(d), full(gw), full(gb)],
        out_specs=pl.BlockSpec((S5_T, width), lambda i: (i, 0)),
        scratch_shapes=[pltpu.VMEM((n_tiles, 1, LANE), F32),
                        pltpu.VMEM((n_tiles, 1, LANE), F32)],
        compiler_params=_params(("arbitrary",)),
        name="s5_mixer",
    )(z, bre, bim, pre, pim, cre, cim, d, gw, gb)


def _s5_tables(lam_re, lam_im, log_step, b_re, b_im, c_re, c_im):
    groups = lam_re.shape[0]
    n_slab = groups * S5_GROUP // S5_SLAB
    gps = S5_SLAB // S5_GROUP
    lr = jnp.minimum(lam_re.astype(F32), -1e-4)
    li = lam_im.astype(F32)
    dt = jnp.exp(log_step.astype(F32))[:, None]
    e = jnp.exp(lr * dt)
    lb_re = e * jnp.cos(li * dt)
    lb_im = e * jnp.sin(li * dt)
    den = lr * lr + li * li
    f_re = ((lb_re - 1.0) * lr + lb_im * li) / den
    f_im = (lb_im * lr - (lb_re - 1.0) * li) / den
    bb_re = f_re[..., None] * b_re - f_im[..., None] * b_im
    bb_im = f_re[..., None] * b_im + f_im[..., None] * b_re
    eye = jnp.eye(gps, dtype=F32)

    def bd_b(m):
        m = m.reshape(n_slab, gps, S5_STATE, S5_GROUP)
        return jnp.einsum('jgpi,gh->jgihp', m, eye).reshape(n_slab, S5_SLAB, S5_SLAB_STATES).astype(BF16)

    def bd_c(m):
        m = m.reshape(n_slab, gps, S5_GROUP, S5_STATE)
        return jnp.einsum('jgip,gh->jgphi', m, eye).reshape(n_slab, S5_SLAB_STATES, S5_SLAB).astype(BF16)

    r_idx = jnp.arange(SUBLANE)
    steps = 2 ** jnp.arange(S5_BLOCK_LEVELS)
    expo = jnp.concatenate([jnp.broadcast_to(steps[:, None], (S5_BLOCK_LEVELS, SUBLANE)),
                            (r_idx + 1)[None, :]], axis=0).astype(F32)
    keep = jnp.concatenate([r_idx[None, :] >= steps[:, None],
                            jnp.ones((1, SUBLANE), bool)], axis=0).astype(F32)
    ph = (li * dt).reshape(-1) * expo[..., None]
    mag = keep[..., None] * jnp.exp((lr * dt).reshape(-1) * expo[..., None])
    n_tiles = ph.shape[-1] // LANE
    tile_major = lambda a: a.reshape(a.shape[0], SUBLANE, n_tiles, LANE).transpose(2, 0, 1, 3)
    pw_re = tile_major(mag * jnp.cos(ph))
    pw_im = tile_major(mag * jnp.sin(ph))
    return (bd_b(bb_re), bd_b(bb_im), pw_re, pw_im,
            bd_c(c_re.astype(F32)), bd_c(c_im.astype(F32)))


def _rwkv_kernel(has_vres, *refs):
    if has_vres:
        (z_ref, vf_ref, mu_ref, wl_ref, w0_ref, al_ref, a0_ref, gl_ref, kk_ref, ka_ref, rk_ref,
         lw_ref, lb_ref, seg_ref, segt_ref, vb_ref, vbias_ref, o_ref, prev_ref, h_ref) = refs
    else:
        (z_ref, mu_ref, wl_ref, w0_ref, al_ref, a0_ref, gl_ref, kk_ref, ka_ref, rk_ref,
         lw_ref, lb_ref, seg_ref, segt_ref, o_ref, vo_ref, prev_ref, h_ref) = refs
    tb = z_ref.shape[0]
    t = RWKV_T
    n_chunk = tb // t
    width = o_ref.shape[1]
    n_pair = width // LANE

    @pl.when(pl.program_id(0) == 0)
    def _():
        prev_ref[...] = jnp.zeros_like(prev_ref)
        h_ref[...] = jnp.zeros_like(h_ref)

    z = z_ref[...]
    rowz = lax.broadcasted_iota(jnp.int32, z.shape, 0)
    prev = jnp.where(rowz == 0, prev_ref[...], pltpu.roll(z, 1, 0))
    prev_ref[...] = z[tb - 1:tb]
    zs = z + (prev - z) * mu_ref[...]

    o1 = 3 * width
    r = zs[:, 0:width]
    k = zs[:, width:2 * width]
    v = zs[:, 2 * width:o1]
    w_in = zs[:, o1:o1 + LANE]
    a_in = zs[:, o1 + LANE:o1 + 2 * LANE]
    g_in = zs[:, o1 + 2 * LANE:o1 + 2 * LANE + RWKV_GATE_LORA]

    wpre = w0_ref[...] + _dot_hp(jnp.tanh(w_in), wl_ref)
    logw = -jnp.exp(-_softplus(-wpre) - 0.5)
    a = _sigmoid(a0_ref[...] + _dot(a_in, al_ref[...]))
    g = _dot(_sigmoid(g_in), gl_ref[...])
    if has_vres:
        vr = zs[:, o1 + 2 * LANE + RWKV_GATE_LORA:]
        vg = _sigmoid(vbias_ref[...] + _dot(vr, vb_ref[...]))
        v = v + (vf_ref[...] - v) * vg
    else:
        vo_ref[...] = v

    seg = seg_ref[...]
    seg_t = segt_ref[...]
    kk = k * kk_ref[...]
    kk = kk * lax.rsqrt(jnp.maximum(_seg_sum(kk * kk, seg, seg_t), 1e-24))
    k2 = k * (1.0 + (a - 1.0) * ka_ref[...])
    kka = kk * a

    shift = t.bit_length() - 1
    ri = lax.broadcasted_iota(jnp.int32, (tb, tb), 0)
    ci = lax.broadcasted_iota(jnp.int32, (tb, tb), 1)
    tri = jnp.where((ci <= ri) & ((ri >> shift) == (ci >> shift)), 1.0, 0.0).astype(BF16)
    lc = _dot_exact_lhs(tri, logw)

    gi = lax.broadcasted_iota(jnp.int32, (4 * t, 4 * t), 0)
    gj = lax.broadcasted_iota(jnp.int32, (4 * t, 4 * t), 1)
    ti = gi & (t - 1)
    sj = gj & (t - 1)
    keep = sj + jnp.where(gi < 2 * t, 1, 0) <= ti
    lane = lax.broadcasted_iota(jnp.int32, (t, LANE), 1)
    head0 = lane < RWKV_HEAD

    def stack2(x):
        return jnp.concatenate([jnp.where(head0, x, 0.0), jnp.where(head0, 0.0, x)], axis=0)

    e_pos = jnp.exp(lc)
    e_neg = jnp.exp(-lc)
    e_prev = jnp.exp(lc - logw)
    rt_all = r * e_pos
    at_all = -kk * e_prev
    bt_all = kka * e_neg
    kt_all = k2 * e_neg

    items = [(c, p) for c in range(n_chunk) for p in range(n_pair)]
    tile = lambda x, c, p: x[c * t:(c + 1) * t, p * LANE:(p + 1) * LANE]
    at = [tile(at_all, c, p) for c, p in items]
    rt = [tile(rt_all, c, p) for c, p in items]
    vbd = [stack2(tile(v, c, p)) for c, p in items]
    gm = [jnp.where(keep, _dot_nt(jnp.concatenate([stack2(a_), stack2(r_)], axis=0),
                                  jnp.concatenate([stack2(tile(bt_all, c, p)), stack2(tile(kt_all, c, p))],
                                                  axis=0)), 0.0)
          for a_, r_, (c, p) in zip(at, rt, items)]
    n1 = [g_[0:2 * t, 0:2 * t] for g_ in gm]
    rhs0 = [_dot(g_[0:2 * t, 2 * t:4 * t], v_) for g_, v_ in zip(gm, vbd)]
    qi = lax.broadcasted_iota(jnp.int32, (2 * t, 2 * t), 0)
    qj = lax.broadcasted_iota(jnp.int32, (2 * t, 2 * t), 1)
    xinv = [jnp.where(qi == qj, 1.0, jnp.where((qi ^ qj) == 1, n_, 0.0)) for n_ in n1]
    for lvl in range(1, shift):
        couple = ((qi >> lvl) ^ (qj >> lvl)) == 1
        xm = [_dot(x_, jnp.where(couple, n_, 0.0)) for x_, n_ in zip(xinv, n1)]
        xinv = [x_ + _dot(m_, x_) for x_, m_ in zip(xinv, xm)]

    hts = [h_ref[p] for p in range(n_pair)]
    out_rows = []
    for c in range(n_chunk):
        idx = [c * n_pair + p for p in range(n_pair)]
        lc_c = lc[c * t:(c + 1) * t]
        e_rem = jnp.exp(lc_c[t - 1:t] - lc_c)
        bh_c = kka[c * t:(c + 1) * t] * e_rem
        kh_c = k2[c * t:(c + 1) * t] * e_rem
        e_last = e_pos[(c + 1) * t - 1:(c + 1) * t]
        ph = [_dot_nt(jnp.concatenate([at[i], rt[i]], axis=0), h_) for i, h_ in zip(idx, hts)]
        u = [_dot(xinv[i], stack2(p_[0:t]) + rhs0[i]) for i, p_ in zip(idx, ph)]
        uv = [jnp.concatenate([u_, vbd[i]], axis=0) for i, u_ in zip(idx, u)]
        opk = [stack2(p_[t:2 * t]) + _dot(gm[i][2 * t:4 * t, :], uv_) for i, p_, uv_ in zip(idx, ph, uv)]
        out_rows.append(jnp.concatenate([o_[0:t] + o_[t:2 * t] for o_ in opk], axis=1))
        hts = [h_ * e_last[:, p * LANE:(p + 1) * LANE]
               + _dot_tn(uv_, jnp.concatenate([stack2(bh_c[:, p * LANE:(p + 1) * LANE]),
                                               stack2(kh_c[:, p * LANE:(p + 1) * LANE])], axis=0))
               for p, (h_, uv_) in enumerate(zip(hts, uv))]
    for p in range(n_pair):
        h_ref[p] = hts[p]

    y = jnp.concatenate(out_rows, axis=0)
    inv_n = 1.0 / RWKV_HEAD
    mean = _seg_sum(y, seg, seg_t) * inv_n
    yc = y - mean
    var = _seg_sum(yc * yc, seg, seg_t) * inv_n
    yn = yc * lax.rsqrt(var + RWKV_LNX_EPS) * lw_ref[...] + lb_ref[...]
    bonus = _seg_sum(r * k2 * rk_ref[...], seg, seg_t) * v
    o_ref[...] = ((yn + bonus) * g).astype(o_ref.dtype)


def _rwkv_call(z, v_first, mu, wl, w0, al, a0, gl, k_k, k_a, r_k, lnx_w, lnx_b, seg, seg_t, vb, vbias):
    length, zc = z.shape
    width = w0.shape[1]
    has_vres = v_first is not None
    full = lambda a: pl.BlockSpec(a.shape, lambda i: (0,) * a.ndim)
    rows = lambda c: pl.BlockSpec((RWKV_BLOCK, c), lambda i: (i, 0))
    common = [mu, wl, w0, al, a0, gl, k_k, k_a, r_k, lnx_w, lnx_b, seg, seg_t]
    if has_vres:
        args = [z, v_first] + common + [vb, vbias]
        in_specs = [rows(zc), rows(width)] + [full(a) for a in common + [vb, vbias]]
        out_shape = jax.ShapeDtypeStruct((length, width), BF16)
        out_specs = rows(width)
    else:
        args = [z] + common
        in_specs = [rows(zc)] + [full(a) for a in common]
        out_shape = (jax.ShapeDtypeStruct((length, width), BF16),
                     jax.ShapeDtypeStruct((length, width), F32))
        out_specs = (rows(width), rows(width))
    return pl.pallas_call(
        functools.partial(_rwkv_kernel, has_vres),
        out_shape=out_shape,
        grid=(length // RWKV_BLOCK,),
        in_specs=in_specs,
        out_specs=out_specs,
        scratch_shapes=[pltpu.VMEM((1, zc), F32),
                        pltpu.VMEM((width // LANE, LANE, LANE), F32)],
        compiler_params=_params(("arbitrary",)),
        name="rwkv7_mixer",
    )(*args)


def _gla_kernel(z_ref, al_ref, ab_ref, ng_ref, o_ref, st_ref):
    tb = z_ref.shape[0]
    t = GLA_T
    n_chunk = tb // t
    width = o_ref.shape[1]
    n_head = width // GLA_DV

    @pl.when(pl.program_id(0) == 0)
    def _():
        st_ref[...] = jnp.zeros_like(st_ref)

    z = z_ref[...]
    q = z[:, 0:width] * (GLA_DK ** -0.5)
    k = z[:, width:2 * width]
    v = z[:, 2 * width:3 * width]
    g = z[:, 3 * width:4 * width]
    a_in = z[:, 4 * width:]
    x = _dot_hp(a_in, al_ref) + ab_ref[...]
    log_a = -_softplus(-x) * (1.0 / GLA_TAU)

    shift = t.bit_length() - 1
    ri = lax.broadcasted_iota(jnp.int32, (tb, tb), 0)
    ci = lax.broadcasted_iota(jnp.int32, (tb, tb), 1)
    tri = jnp.where((ci <= ri) & ((ri >> shift) == (ci >> shift)), 1.0, 0.0).astype(BF16)
    b = _dot_exact_lhs(tri, log_a)
    causal = (lax.broadcasted_iota(jnp.int32, (t, t), 1) <= lax.broadcasted_iota(jnp.int32, (t, t), 0))

    q_in = q * jnp.exp(b)
    heads = range(n_head)
    head = lambda x, h: x[:, h * GLA_DV:(h + 1) * GLA_DV]
    chunk = lambda x, c: x[c * t:(c + 1) * t]
    v_c = [chunk(v, c) for c in range(n_chunk)]
    k_rem = []
    e_last = []
    intra = []
    for c in range(n_chunk):
        b_c = chunk(b, c)
        b_mid = b_c[t // 2:t // 2 + 1]
        b_last = b_c[t - 1:t]
        q_mid = chunk(q, c) * jnp.exp(b_c - b_mid)
        k_mid = chunk(k, c) * jnp.exp(b_mid - b_c)
        k_rem.append(chunk(k, c) * jnp.exp(b_last - b_c))
        e_last.append(jnp.exp(b_last))
        intra.append([_dot(jnp.where(causal, _dot_nt(head(q_mid, h), head(k_mid, h)), 0.0), head(v_c[c], h))
                      for h in heads])
    sts = [st_ref[h] for h in heads]
    out_rows = []
    for c in range(n_chunk):
        q_c = chunk(q_in, c)
        o = [i_ + _dot_nt(head(q_c, h), s_) for h, (i_, s_) in enumerate(zip(intra[c], sts))]
        sts = [s_ * head(e_last[c], h) + _dot_tn(head(v_c[c], h), head(k_rem[c], h))
               for h, s_ in enumerate(sts)]
        out_rows.append(jnp.concatenate(
            [o_ * lax.rsqrt(jnp.mean(o_ * o_, axis=-1, keepdims=True) + NORM_EPS) for o_ in o], axis=1))
    for h in heads:
        st_ref[h] = sts[h]
    o = jnp.concatenate(out_rows, axis=0)
    o_ref[...] = (o * ng_ref[...] * (g * _sigmoid(g))).astype(o_ref.dtype)


def _gla_call(z, al, ab, ng):
    length, zc = z.shape
    width = ng.shape[1]
    full = lambda a: pl.BlockSpec(a.shape, lambda i: (0,) * a.ndim)
    return pl.pallas_call(
        _gla_kernel,
        out_shape=jax.ShapeDtypeStruct((length, width), BF16),
        grid=(length // GLA_BLOCK,),
        in_specs=[pl.BlockSpec((GLA_BLOCK, zc), lambda i: (i, 0)), full(al), full(ab), full(ng)],
        out_specs=pl.BlockSpec((GLA_BLOCK, width), lambda i: (i, 0)),
        scratch_shapes=[pltpu.VMEM((width // GLA_DV, GLA_DV, LANE), F32)],
        compiler_params=_params(("arbitrary",)),
        name="gla_mixer",
    )(z, al, ab, ng)


def _merge_kernel(ya_ref, yb_ref, yc_ref, gate_ref, x_ref, wua_ref, wub_ref, wuc_ref, wo_ref, g_ref,
                  xo_ref, h_ref):
    d = x_ref.shape[1]
    gates = gate_ref[...].astype(F32)
    merged = (gates[:, 0:d] * jnp.dot(ya_ref[...], wua_ref[...], preferred_element_type=F32)
              + gates[:, d:2 * d] * jnp.dot(yb_ref[...], wub_ref[...], preferred_element_type=F32)
              + gates[:, 2 * d:3 * d] * jnp.dot(yc_ref[...], wuc_ref[...], preferred_element_type=F32))
    x = x_ref[...] + _dot(merged, wo_ref[...])
    xo_ref[...] = x
    h_ref[...] = _rms_rows(x, g_ref[...]).astype(h_ref.dtype)


def _merge_call(ya, yb, yc, gates, x, wua, wub, wuc, wo, g, tm=256):
    m, d = x.shape
    rows = lambda a: pl.BlockSpec((tm, a.shape[1]), lambda i: (i, 0))
    const = lambda a: pl.BlockSpec(a.shape, lambda i: (0,) * a.ndim, pipeline_mode=pl.Buffered(1))
    return pl.pallas_call(
        _merge_kernel,
        out_shape=(jax.ShapeDtypeStruct((m, d), F32), jax.ShapeDtypeStruct((m, d), BF16)),
        grid=(m // tm,),
        in_specs=[rows(ya), rows(yb), rows(yc), rows(gates), rows(x),
                  const(wua), const(wub), const(wuc), const(wo), const(g)],
        out_specs=(pl.BlockSpec((tm, d), lambda i: (i, 0)), pl.BlockSpec((tm, d), lambda i: (i, 0))),
        compiler_params=_params(("parallel",)),
        name="merge_out_proj",
    )(ya, yb, yc, gates, x, wua, wub, wuc, wo, g)


def _mlp_kernel(final, h_ref, w1_ref, w2_ref, x_ref, g_ref, xo_ref, *rest):
    f = pl.program_id(1)

    @pl.when(f == 0)
    def _():
        xo_ref[...] = x_ref[...]

    hid = jnp.maximum(jnp.dot(h_ref[...], w1_ref[...], preferred_element_type=F32), 0.0)
    xo_ref[...] += _dot(hid * hid, w2_ref[...])

    @pl.when(f == pl.num_programs(1) - 1)
    def _():
        y = _rms_rows(xo_ref[...], g_ref[...])
        if final:
            xo_ref[...] = y
        else:
            rest[0][...] = y.astype(rest[0].dtype)


def _mlp_call(h, w1, w2, x, g_next, final, tm=512, tf=1024):
    m, d = x.shape
    ff = w1.shape[1]
    row_blk = pl.BlockSpec((tm, d), lambda i, f: (i, 0))
    out_shape = [jax.ShapeDtypeStruct((m, d), F32)]
    out_specs = [row_blk]
    if not final:
        out_shape.append(jax.ShapeDtypeStruct((m, d), BF16))
        out_specs.append(row_blk)
    return pl.pallas_call(
        functools.partial(_mlp_kernel, final),
        out_shape=tuple(out_shape),
        grid=(m // tm, ff // tf),
        in_specs=[row_blk,
                  pl.BlockSpec((d, tf), lambda i, f: (0, f)),
                  pl.BlockSpec((tf, d), lambda i, f: (f, 0)),
                  row_blk,
                  pl.BlockSpec((1, d), lambda i, f: (0, 0))],
        out_specs=tuple(out_specs),
        compiler_params=_params(("parallel", "arbitrary")),
        name="mlp",
    )(h, w1, w2, x, g_next)


def _pad_cols(a, n):
    return jnp.pad(a, ((0, 0), (0, n - a.shape[1])))


def _pad_rows(a, n):
    return jnp.pad(a, ((0, n - a.shape[0]), (0, 0)))


def _head_pad(a, heads, dh, to):
    lead = a.shape[:-1]
    a = a.reshape(lead + (heads, dh))
    a = jnp.pad(a, [(0, 0)] * len(lead) + [(0, 0), (0, to - dh)])
    return a.reshape(lead + (heads * to,))


def kernel(x, norm_mix, w_in, gate_bias, s5_lambda_re, s5_lambda_im, s5_log_step, s5_b_re, s5_b_im, s5_c_re, s5_c_im, s5_d, s5_glu_w, s5_glu_b, rwkv_mu, rwkv_w_lora, rwkv_w0, rwkv_a_lora, rwkv_a0, rwkv_g_lora, rwkv_k_k, rwkv_k_a, rwkv_r_k, rwkv_lnx_w, rwkv_lnx_b, rwkv_vres_a, rwkv_vres_mu, rwkv_vres_b, rwkv_vres_bias, gla_alpha_lora, gla_alpha_bias, gla_norm_g, w_up, w_out, norm_mlp, mlp_w1, mlp_w2, final_norm):
    bsz, length, d = x.shape
    depth = w_in.shape[0]
    s5_w = s5_d.shape[1]
    rw_w = rwkv_w0.shape[1]
    gla_v = gla_norm_g.shape[1]
    gla_heads = gla_v // GLA_DV
    gla_k = gla_heads * GLA_DK
    rw_cols = 3 * rw_w + RWKV_DECAY_LORA + RWKV_AAA_LORA + RWKV_GATE_LORA
    gla_cols = 2 * gla_k + 2 * gla_v + GLA_LORA
    o_rw = s5_w
    o_gla = o_rw + rw_cols
    o_gate = o_gla + gla_cols
    o1 = 3 * rw_w

    head_id = jnp.arange(rw_w) // RWKV_HEAD
    seg = (head_id[:, None] == jnp.arange(LANE)[None, :]).astype(BF16)
    seg_t = seg.T
    row1 = lambda a: a.reshape(1, -1).astype(F32)

    outs = []
    for b in range(bsz):
        xb = x[b].astype(F32)
        u = _norm_call(xb, row1(norm_mix[0]))
        v_first = None
        for l in range(depth):
            wl = w_in[l]
            w_s5 = wl[:, :o_rw].astype(BF16)
            wr = wl[:, o_rw:o_gla]
            pieces = [wr[:, :o1],
                      _pad_cols(wr[:, o1:o1 + RWKV_DECAY_LORA], LANE),
                      wr[:, o1 + RWKV_DECAY_LORA:o1 + RWKV_DECAY_LORA + RWKV_AAA_LORA],
                      wr[:, o1 + RWKV_DECAY_LORA + RWKV_AAA_LORA:]]
            mu = rwkv_mu[l]
            mu_pieces = [mu[:o1], jnp.pad(mu[o1:o1 + RWKV_DECAY_LORA], (0, LANE - RWKV_DECAY_LORA)),
                         mu[o1 + RWKV_DECAY_LORA:]]
            if l > 0:
                pieces.append(_pad_cols(rwkv_vres_a[l - 1], LANE))
                mu_pieces.append(jnp.pad(rwkv_vres_mu[l - 1], (0, LANE - RWKV_MV_LORA)))
            else:
                pieces.append(jnp.zeros((d, LANE), wl.dtype))
                mu_pieces.append(jnp.zeros((LANE,), mu.dtype))
            w_rw = jnp.concatenate(pieces, axis=1).astype(BF16)
            mu_rw = row1(jnp.concatenate(mu_pieces))
            wg = wl[:, o_gla:o_gate]
            w_gla = jnp.concatenate([
                _head_pad(wg[:, :gla_k], gla_heads, GLA_DK, GLA_DV),
                _head_pad(wg[:, gla_k:2 * gla_k], gla_heads, GLA_DK, GLA_DV),
                wg[:, 2 * gla_k:2 * gla_k + 2 * gla_v],
                _pad_cols(wg[:, 2 * gla_k + 2 * gla_v:], LANE)], axis=1).astype(BF16)
            w_gate = wl[:, o_gate:].astype(BF16)

            z_s5 = _mm_call(u, w_s5, tn=s5_w, name="in_proj_s5")
            z_rw = _mm_call(u, w_rw, tn=w_rw.shape[1] // 2, name="in_proj_rwkv")
            z_gla = _mm_call(u, w_gla, tn=w_gla.shape[1] // 3, name="in_proj_gla")
            gates = _mm_call(u, w_gate, tn=1024, bias=row1(gate_bias[l]), name="in_proj_gate")

            tabs = _s5_tables(s5_lambda_re[l], s5_lambda_im[l], s5_log_step[l],
                              s5_b_re[l].astype(F32), s5_b_im[l].astype(F32), s5_c_re[l], s5_c_im[l])
            y_a = _s5_call(z_s5, *tabs, row1(s5_d[l]), s5_glu_w[l].astype(BF16), row1(s5_glu_b[l]))

            wlo = _split_weight(_pad_rows(rwkv_w_lora[l], LANE))
            if l > 0:
                vb = _pad_rows(rwkv_vres_b[l - 1], LANE).astype(BF16)
                vbias = row1(rwkv_vres_bias[l - 1])
            else:
                vb = vbias = None
            res = _rwkv_call(z_rw, v_first, mu_rw, wlo, row1(rwkv_w0[l]),
                             rwkv_a_lora[l].astype(BF16), row1(rwkv_a0[l]), rwkv_g_lora[l].astype(BF16),
                             row1(rwkv_k_k[l]), row1(rwkv_k_a[l]), row1(rwkv_r_k[l]),
                             row1(rwkv_lnx_w[l]), row1(rwkv_lnx_b[l]), seg, seg_t, vb, vbias)
            if l == 0:
                y_b, v_first = res
            else:
                y_b = res

            al = _split_weight(_pad_rows(_head_pad(gla_alpha_lora[l], gla_heads, GLA_DK, GLA_DV), LANE))
            ab = row1(_head_pad(gla_alpha_bias[l], gla_heads, GLA_DK, GLA_DV))
            y_c = _gla_call(z_gla, al, ab, row1(gla_norm_g[l]))

            wu = w_up[l].astype(BF16)
            x_mid, h = _merge_call(y_a, y_b, y_c, gates, xb,
                                   wu[:s5_w], wu[s5_w:s5_w + rw_w], wu[s5_w + rw_w:],
                                   w_out[l].astype(BF16), row1(norm_mlp[l]))
            final = l == depth - 1
            g_next = row1(final_norm if final else norm_mix[l + 1])
            res = _mlp_call(h, mlp_w1[l].astype(BF16), mlp_w2[l].astype(BF16), x_mid, g_next, final)
            if final:
                xb = res[0]
            else:
                xb, u = res
        outs.append(xb.astype(x.dtype))
    return jnp.stack(outs, axis=0)
```

```python
import functools
import math

import jax
import jax.numpy as jnp
from jax import lax
from jax.experimental import pallas as pl
from jax.experimental.pallas import tpu as pltpu

F32 = jnp.float32
BF16 = jnp.bfloat16

LANE = 128
SUBLANE = 8
NORM_EPS = 1e-6

S5_GROUP = 16
S5_STATE = 64
S5_SLAB = 256
S5_SLAB_STATES = (S5_SLAB // S5_GROUP) * S5_STATE
S5_T = 256
S5_BLOCK_LEVELS = 3
S5_FOLD = 2

RWKV_HEAD = 64
RWKV_T = 64
RWKV_BLOCK = 256
RWKV_DECAY_LORA = 96
RWKV_AAA_LORA = 128
RWKV_GATE_LORA = 256
RWKV_MV_LORA = 64
RWKV_LNX_EPS = 64e-5

GLA_DK = 64
GLA_DV = 128
GLA_LORA = 16
GLA_TAU = 16.0
GLA_T = 64
GLA_BLOCK = 256

VMEM_LIMIT = 56 * 1024 * 1024


def _dot(a, b):
    return jnp.dot(a.astype(BF16), b.astype(BF16), preferred_element_type=F32)


def _dot_nt(a, b):
    return lax.dot_general(a.astype(BF16), b.astype(BF16), (((1,), (1,)), ((), ())),
                           preferred_element_type=F32)


def _dot_tn(a, b):
    return lax.dot_general(a.astype(BF16), b.astype(BF16), (((0,), (0,)), ((), ())),
                           preferred_element_type=F32)


def _split3(x):
    hi = x.astype(BF16)
    r1 = x - hi.astype(F32)
    mid = r1.astype(BF16)
    lo = (r1 - mid.astype(F32)).astype(BF16)
    return hi, mid, lo


def _dot_exact_lhs(m_bf16, x):
    hi, mid, lo = _split3(x)
    return (jnp.dot(m_bf16, hi, preferred_element_type=F32)
            + jnp.dot(m_bf16, mid, preferred_element_type=F32)
            + jnp.dot(m_bf16, lo, preferred_element_type=F32))


def _hi_lo(x):
    hi = x.astype(BF16)
    return hi, (x - hi.astype(F32)).astype(BF16)


def _seg_sum(x, seg, seg_t):
    s = jnp.dot(x.astype(BF16), seg, preferred_element_type=F32)
    hi, lo = _hi_lo(s)
    return jnp.dot(hi, seg_t, preferred_element_type=F32) + jnp.dot(lo, seg_t, preferred_element_type=F32)


def _split_weight(w):
    hi, lo = _hi_lo(w.astype(F32))
    return jnp.stack([hi, lo])


def _dot_hp(a, w_ref):
    a_hi, a_lo = _hi_lo(a)
    return (jnp.dot(a_hi, w_ref[0], preferred_element_type=F32)
            + jnp.dot(a_lo, w_ref[0], preferred_element_type=F32)
            + jnp.dot(a_hi, w_ref[1], preferred_element_type=F32))


def _sigmoid(x):
    return 1.0 / (1.0 + jnp.exp(-x))


def _softplus(x):
    return jnp.maximum(x, 0.0) + jnp.log(1.0 + jnp.exp(-jnp.abs(x)))


def _rms_rows(x, g):
    return x * lax.rsqrt(jnp.mean(x * x, axis=-1, keepdims=True) + NORM_EPS) * g


def _shift_rows(x, d):
    n = x.shape[0]
    if d % 8 == 0:
        return jnp.concatenate([jnp.zeros((d, x.shape[1]), x.dtype), x[:n - d]], axis=0)
    rolled = pltpu.roll(x, d, 0)
    row = lax.broadcasted_iota(jnp.int32, x.shape, 0)
    return jnp.where(row >= d, rolled, 0.0)


def _params(sem):
    return pltpu.CompilerParams(dimension_semantics=sem, vmem_limit_bytes=VMEM_LIMIT)


def _norm_kernel(x_ref, g_ref, o_ref):
    o_ref[...] = _rms_rows(x_ref[...], g_ref[...]).astype(o_ref.dtype)


def _norm_call(x, g, tm=512):
    m, d = x.shape
    return pl.pallas_call(
        _norm_kernel,
        out_shape=jax.ShapeDtypeStruct((m, d), BF16),
        grid=(m // tm,),
        in_specs=[pl.BlockSpec((tm, d), lambda i: (i, 0)),
                  pl.BlockSpec((1, d), lambda i: (0, 0))],
        out_specs=pl.BlockSpec((tm, d), lambda i: (i, 0)),
        compiler_params=_params(("parallel",)),
        name="rmsnorm",
    )(x, g)


def _mm_kernel(a_ref, w_ref, o_ref):
    o_ref[...] = jnp.dot(a_ref[...], w_ref[...], preferred_element_type=F32).astype(o_ref.dtype)


def _mm_gate_kernel(a_ref, w_ref, b_ref, o_ref):
    z = jnp.dot(a_ref[...], w_ref[...], preferred_element_type=F32) + b_ref[...]
    o_ref[...] = _sigmoid(z).astype(o_ref.dtype)


def _mm_call(a, w, tn, bias=None, tm=1024, name="in_proj"):
    m, k = a.shape
    n = w.shape[1]
    in_specs = [pl.BlockSpec((tm, k), lambda i, j: (i, 0)),
                pl.BlockSpec((k, tn), lambda i, j: (0, j))]
    args = [a, w]
    if bias is None:
        body, out_dtype = _mm_kernel, F32
    else:
        body, out_dtype = _mm_gate_kernel, BF16
        in_specs.append(pl.BlockSpec((1, tn), lambda i, j: (0, j)))
        args.append(bias)
    return pl.pallas_call(
        body,
        out_shape=jax.ShapeDtypeStruct((m, n), out_dtype),
        grid=(m // tm, n // tn),
        in_specs=in_specs,
        out_specs=pl.BlockSpec((tm, tn), lambda i, j: (i, j)),
        compiler_params=_params(("parallel", "parallel")),
        name=name,
    )(*args)


def _s5_kernel(z_ref, bre_ref, bim_ref, pre_ref, pim_ref, cre_ref, cim_ref, d_ref, gw_ref, gb_ref,
               o_ref, carry_re, carry_im):
    n_slab = bre_ref.shape[0]
    tiles_per_slab = S5_SLAB_STATES // LANE
    t = z_ref.shape[0]

    @pl.when(pl.program_id(0) == 0)
    def _():
        carry_re[...] = jnp.zeros_like(carry_re)
        carry_im[...] = jnp.zeros_like(carry_im)

    u = z_ref[...]
    ub = u.astype(BF16)
    n_blk = t // SUBLANE
    row_in_block = lax.broadcasted_iota(jnp.int32, u.shape, 0) & (SUBLANE - 1)
    shifted = [ub] + [jnp.where(row_in_block >= d, pltpu.roll(u, d, 0), 0.0).astype(BF16)
                      for d in range(1, S5_FOLD)]
    ys = []
    for j in range(n_slab):
        uj = jnp.concatenate([s_[:, j * S5_SLAB:(j + 1) * S5_SLAB] for s_ in shifted], axis=1)
        bu_re = jnp.dot(uj, bre_ref[j], preferred_element_type=F32)
        bu_im = jnp.dot(uj, bim_ref[j], preferred_element_type=F32)
        tiles_re = []
        tiles_im = []
        for c in range(tiles_per_slab):
            idx = j * tiles_per_slab + c
            sr = bu_re[:, c * LANE:(c + 1) * LANE].reshape(n_blk, SUBLANE, LANE)
            si = bu_im[:, c * LANE:(c + 1) * LANE].reshape(n_blk, SUBLANE, LANE)
            pr = pre_ref[idx]
            pi = pim_ref[idx]
            for k in range(S5_FOLD.bit_length() - 1, S5_BLOCK_LEVELS):
                qr = pltpu.roll(sr, 1 << k, 1)
                qi = pltpu.roll(si, 1 << k, 1)
                sr, si = sr + (pr[k] * qr - pi[k] * qi), si + (pr[k] * qi + pi[k] * qr)
            cr = jnp.broadcast_to(carry_re[idx], (SUBLANE, LANE))
            ci = jnp.broadcast_to(carry_im[idx], (SUBLANE, LANE))
            ar = pr[S5_BLOCK_LEVELS]
            ai = pi[S5_BLOCK_LEVELS]
            blocks_re = []
            blocks_im = []
            for b in range(n_blk):
                br = sr[b] + (ar * cr - ai * ci)
                bi = si[b] + (ar * ci + ai * cr)
                blocks_re.append(br)
                blocks_im.append(bi)
                cr = jnp.broadcast_to(br[SUBLANE - 1:SUBLANE], (SUBLANE, LANE))
                ci = jnp.broadcast_to(bi[SUBLANE - 1:SUBLANE], (SUBLANE, LANE))
            carry_re[idx] = cr[0:1]
            carry_im[idx] = ci[0:1]
            tiles_re.append(jnp.concatenate(blocks_re, axis=0).astype(BF16))
            tiles_im.append(jnp.concatenate(blocks_im, axis=0).astype(BF16))
        s_re = jnp.concatenate(tiles_re, axis=1)
        s_im = jnp.concatenate(tiles_im, axis=1)
        ys.append(jnp.dot(s_re, cre_ref[j], preferred_element_type=F32)
                  - jnp.dot(s_im, cim_ref[j], preferred_element_type=F32))
    y = jnp.concatenate(ys, axis=1) + d_ref[...] * u
    y = y * (0.5 * (1.0 + jnp.tanh(math.sqrt(2.0 / math.pi) * (y + 0.044715 * (y * y * y)))))
    y = y * _sigmoid(_dot(y, gw_ref[...]) + gb_ref[...])
    o_ref[...] = y.astype(o_ref.dtype)


def _s5_call(z, bre, bim, pre, pim, cre, cim, d, gw, gb):
    length, width = z.shape
    n_slab = bre.shape[0]
    n_tiles = n_slab * (S5_SLAB_STATES // LANE)
    full = lambda a: pl.BlockSpec(a.shape, lambda i: (0,) * a.ndim, pipeline_mode=pl.Buffered(1))
    return pl.pallas_call(
        _s5_kernel,
        out_shape=jax.ShapeDtypeStruct((length, width), BF16),
        grid=(length // S5_T,),
        in_specs=[pl.BlockSpec((S5_T, width), lambda i: (i, 0)),
                  full(bre), full(bim), full(pre), full(pim), full(cre), full(cim),
                  full(d), full(gw), full(gb)],
        out_specs=pl.BlockSpec((S5_T, width), lambda i: (i, 0)),
        scratch_shapes=[pltpu.VMEM((n_tiles, 1, LANE), F32),
                        pltpu.VMEM((n_tiles, 1, LANE), F32)],
        compiler_params=_params(("arbitrary",)),
        name="s5_mixer",
    )(z, bre, bim, pre, pim, cre, cim, d, gw, gb)


def _s5_tables(lam_re, lam_im, log_step, b_re, b_im, c_re, c_im):
    groups = lam_re.shape[0]
    n_slab = groups * S5_GROUP // S5_SLAB
    gps = S5_SLAB // S5_GROUP
    lr = jnp.minimum(lam_re.astype(F32), -1e-4)
    li = lam_im.astype(F32)
    dt = jnp.exp(log_step.astype(F32))[:, None]
    e = jnp.exp(lr * dt)
    lb_re = e * jnp.cos(li * dt)
    lb_im = e * jnp.sin(li * dt)
    den = lr * lr + li * li
    f_re = ((lb_re - 1.0) * lr + lb_im * li) / den
    f_im = (lb_im * lr - (lb_re - 1.0) * li) / den
    bb_re = f_re[..., None] * b_re - f_im[..., None] * b_im
    bb_im = f_re[..., None] * b_im + f_im[..., None] * b_re
    eye = jnp.eye(gps, dtype=F32)

    def bd_b(m):
        m = m.reshape(n_slab, gps, S5_STATE, S5_GROUP)
        return jnp.einsum('jgpi,gh->jgihp', m, eye).reshape(n_slab, S5_SLAB, S5_SLAB_STATES).astype(BF16)

    def bd_c(m):
        m = m.reshape(n_slab, gps, S5_GROUP, S5_STATE)
        return jnp.einsum('jgip,gh->jgphi', m, eye).reshape(n_slab, S5_SLAB_STATES, S5_SLAB).astype(BF16)

    r_idx = jnp.arange(SUBLANE)
    steps = 2 ** jnp.arange(S5_BLOCK_LEVELS)
    expo = jnp.concatenate([jnp.broadcast_to(steps[:, None], (S5_BLOCK_LEVELS, SUBLANE)),
                            (r_idx + 1)[None, :]], axis=0).astype(F32)
    keep = jnp.concatenate([r_idx[None, :] >= steps[:, None],
                            jnp.ones((1, SUBLANE), bool)], axis=0).astype(F32)
    ph = (li * dt).reshape(-1) * expo[..., None]
    mag = keep[..., None] * jnp.exp((lr * dt).reshape(-1) * expo[..., None])
    n_tiles = ph.shape[-1] // LANE
    tile_major = lambda a: a.reshape(a.shape[0], SUBLANE, n_tiles, LANE).transpose(2, 0, 1, 3)
    pw_re = tile_major(mag * jnp.cos(ph))
    pw_im = tile_major(mag * jnp.sin(ph))
    fold = jnp.arange(S5_FOLD, dtype=F32)[:, None, None]
    fd_mag = jnp.exp(lr * dt * fold)
    fd_re = (fd_mag * jnp.cos(li * dt * fold))[..., None]
    fd_im = (fd_mag * jnp.sin(li * dt * fold))[..., None]
    b_stack_re = jnp.concatenate([bd_b(fd_re[d] * bb_re - fd_im[d] * bb_im) for d in range(S5_FOLD)], axis=1)
    b_stack_im = jnp.concatenate([bd_b(fd_re[d] * bb_im + fd_im[d] * bb_re) for d in range(S5_FOLD)], axis=1)
    return (b_stack_re, b_stack_im, pw_re, pw_im,
            bd_c(c_re.astype(F32)), bd_c(c_im.astype(F32)))


def _rwkv_kernel(has_vres, *refs):
    if has_vres:
        (z_ref, vf_ref, mu_ref, wl_ref, w0_ref, al_ref, a0_ref, gl_ref, kk_ref, ka_ref, rk_ref,
         lw_ref, lb_ref, seg_ref, segt_ref, vb_ref, vbias_ref, o_ref, prev_ref, h_ref) = refs
    else:
        (z_ref, mu_ref, wl_ref, w0_ref, al_ref, a0_ref, gl_ref, kk_ref, ka_ref, rk_ref,
         lw_ref, lb_ref, seg_ref, segt_ref, o_ref, vo_ref, prev_ref, h_ref) = refs
    tb = z_ref.shape[0]
    t = RWKV_T
    n_chunk = tb // t
    width = o_ref.shape[1]
    n_pair = width // LANE

    @pl.when(pl.program_id(0) == 0)
    def _():
        prev_ref[...] = jnp.zeros_like(prev_ref)
        h_ref[...] = jnp.zeros_like(h_ref)

    z = z_ref[...]
    rowz = lax.broadcasted_iota(jnp.int32, z.shape, 0)
    prev = jnp.where(rowz == 0, prev_ref[...], pltpu.roll(z, 1, 0))
    prev_ref[...] = z[tb - 1:tb]
    zs = z + (prev - z) * mu_ref[...]

    o1 = 3 * width
    r = zs[:, 0:width]
    k = zs[:, width:2 * width]
    v = zs[:, 2 * width:o1]
    w_in = zs[:, o1:o1 + LANE]
    a_in = zs[:, o1 + LANE:o1 + 2 * LANE]
    g_in = zs[:, o1 + 2 * LANE:o1 + 2 * LANE + RWKV_GATE_LORA]

    wpre = w0_ref[...] + _dot_hp(jnp.tanh(w_in), wl_ref)
    logw = -jnp.exp(-_softplus(-wpre) - 0.5)
    a = _sigmoid(a0_ref[...] + _dot(a_in, al_ref[...]))
    g = _dot(_sigmoid(g_in), gl_ref[...])
    if has_vres:
        vr = zs[:, o1 + 2 * LANE + RWKV_GATE_LORA:]
        vg = _sigmoid(vbias_ref[...] + _dot(vr, vb_ref[...]))
        v = v + (vf_ref[...] - v) * vg
    else:
        vo_ref[...] = v

    seg = seg_ref[...]
    seg_t = segt_ref[...]
    kk = k * kk_ref[...]
    kk = kk * lax.rsqrt(jnp.maximum(_seg_sum(kk * kk, seg, seg_t), 1e-24))
    k2 = k * (1.0 + (a - 1.0) * ka_ref[...])
    kka = kk * a

    shift = t.bit_length() - 1
    ri = lax.broadcasted_iota(jnp.int32, (tb, tb), 0)
    ci = lax.broadcasted_iota(jnp.int32, (tb, tb), 1)
    tri = jnp.where((ci <= ri) & ((ri >> shift) == (ci >> shift)), 1.0, 0.0).astype(BF16)
    lc = _dot_exact_lhs(tri, logw)

    gi = lax.broadcasted_iota(jnp.int32, (4 * t, 4 * t), 0)
    gj = lax.broadcasted_iota(jnp.int32, (4 * t, 4 * t), 1)
    ti = gi & (t - 1)
    sj = gj & (t - 1)
    keep = sj + jnp.where(gi < 2 * t, 1, 0) <= ti
    lane = lax.broadcasted_iota(jnp.int32, (t, LANE), 1)
    head0 = lane < RWKV_HEAD

    def stack2(x):
        return jnp.concatenate([jnp.where(head0, x, 0.0), jnp.where(head0, 0.0, x)], axis=0)

    e_pos = jnp.exp(lc)
    e_neg = jnp.exp(-lc)
    e_prev = jnp.exp(lc - logw)
    rt_all = r * e_pos
    at_all = -kk * e_prev
    bt_all = kka * e_neg
    kt_all = k2 * e_neg

    items = [(c, p) for c in range(n_chunk) for p in range(n_pair)]
    tile = lambda x, c, p: x[c * t:(c + 1) * t, p * LANE:(p + 1) * LANE]
    at = [tile(at_all, c, p) for c, p in items]
    rt = [tile(rt_all, c, p) for c, p in items]
    vbd = [stack2(tile(v, c, p)) for c, p in items]
    gm = [jnp.where(keep, _dot_nt(jnp.concatenate([stack2(a_), stack2(r_)], axis=0),
                                  jnp.concatenate([stack2(tile(bt_all, c, p)), stack2(tile(kt_all, c, p))],
                                                  axis=0)), 0.0)
          for a_, r_, (c, p) in zip(at, rt, items)]
    n1 = [g_[0:2 * t, 0:2 * t] for g_ in gm]
    rhs0 = [_dot(g_[0:2 * t, 2 * t:4 * t], v_) for g_, v_ in zip(gm, vbd)]
    qi = lax.broadcasted_iota(jnp.int32, (2 * t, 2 * t), 0)
    qj = lax.broadcasted_iota(jnp.int32, (2 * t, 2 * t), 1)
    xinv = [jnp.where(qi == qj, 1.0, jnp.where((qi ^ qj) == 1, n_, 0.0)) for n_ in n1]
    for lvl in range(1, shift):
        couple = ((qi >> lvl) ^ (qj >> lvl)) == 1
        xm = [_dot(x_, jnp.where(couple, n_, 0.0)) for x_, n_ in zip(xinv, n1)]
        xinv = [x_ + _dot(m_, x_) for x_, m_ in zip(xinv, xm)]

    hts = [h_ref[p] for p in range(n_pair)]
    out_rows = []
    for c in range(n_chunk):
        idx = [c * n_pair + p for p in range(n_pair)]
        lc_c = lc[c * t:(c + 1) * t]
        e_rem = jnp.exp(lc_c[t - 1:t] - lc_c)
        bh_c = kka[c * t:(c + 1) * t] * e_rem
        kh_c = k2[c * t:(c + 1) * t] * e_rem
        e_last = e_pos[(c + 1) * t - 1:(c + 1) * t]
        ph = [_dot_nt(jnp.concatenate([at[i], rt[i]], axis=0), h_) for i, h_ in zip(idx, hts)]
        u = [_dot(xinv[i], stack2(p_[0:t]) + rhs0[i]) for i, p_ in zip(idx, ph)]
        uv = [jnp.concatenate([u_, vbd[i]], axis=0) for i, u_ in zip(idx, u)]
        opk = [stack2(p_[t:2 * t]) + _dot(gm[i][2 * t:4 * t, :], uv_) for i, p_, uv_ in zip(idx, ph, uv)]
        out_rows.append(jnp.concatenate([o_[0:t] + o_[t:2 * t] for o_ in opk], axis=1))
        hts = [h_ * e_last[:, p * LANE:(p + 1) * LANE]
               + _dot_tn(uv_, jnp.concatenate([stack2(bh_c[:, p * LANE:(p + 1) * LANE]),
                                               stack2(kh_c[:, p * LANE:(p + 1) * LANE])], axis=0))
               for p, (h_, uv_) in enumerate(zip(hts, uv))]
    for p in range(n_pair):
        h_ref[p] = hts[p]

    y = jnp.concatenate(out_rows, axis=0)
    inv_n = 1.0 / RWKV_HEAD
    mean = _seg_sum(y, seg, seg_t) * inv_n
    yc = y - mean
    var = _seg_sum(yc * yc, seg, seg_t) * inv_n
    yn = yc * lax.rsqrt(var + RWKV_LNX_EPS) * lw_ref[...] + lb_ref[...]
    bonus = _seg_sum(r * k2 * rk_ref[...], seg, seg_t) * v
    o_ref[...] = ((yn + bonus) * g).astype(o_ref.dtype)


def _rwkv_call(z, v_first, mu, wl, w0, al, a0, gl, k_k, k_a, r_k, lnx_w, lnx_b, seg, seg_t, vb, vbias):
    length, zc = z.shape
    width = w0.shape[1]
    has_vres = v_first is not None
    full = lambda a: pl.BlockSpec(a.shape, lambda i: (0,) * a.ndim)
    rows = lambda c: pl.BlockSpec((RWKV_BLOCK, c), lambda i: (i, 0))
    common = [mu, wl, w0, al, a0, gl, k_k, k_a, r_k, lnx_w, lnx_b, seg, seg_t]
    if has_vres:
        args = [z, v_first] + common + [vb, vbias]
        in_specs = [rows(zc), rows(width)] + [full(a) for a in common + [vb, vbias]]
        out_shape = jax.ShapeDtypeStruct((length, width), BF16)
        out_specs = rows(width)
    else:
        args = [z] + common
        in_specs = [rows(zc)] + [full(a) for a in common]
        out_shape = (jax.ShapeDtypeStruct((length, width), BF16),
                     jax.ShapeDtypeStruct((length, width), F32))
        out_specs = (rows(width), rows(width))
    return pl.pallas_call(
        functools.partial(_rwkv_kernel, has_vres),
        out_shape=out_shape,
        grid=(length // RWKV_BLOCK,),
        in_specs=in_specs,
        out_specs=out_specs,
        scratch_shapes=[pltpu.VMEM((1, zc), F32),
                        pltpu.VMEM((width // LANE, LANE, LANE), F32)],
        compiler_params=_params(("arbitrary",)),
        name="rwkv7_mixer",
    )(*args)


def _gla_kernel(z_ref, al_ref, ab_ref, ng_ref, o_ref, st_ref):
    tb = z_ref.shape[0]
    t = GLA_T
    n_chunk = tb // t
    width = o_ref.shape[1]
    n_head = width // GLA_DV

    @pl.when(pl.program_id(0) == 0)
    def _():
        st_ref[...] = jnp.zeros_like(st_ref)

    z = z_ref[...]
    q = z[:, 0:width] * (GLA_DK ** -0.5)
    k = z[:, width:2 * width]
    v = z[:, 2 * width:3 * width]
    g = z[:, 3 * width:4 * width]
    a_in = z[:, 4 * width:]
    x = _dot_hp(a_in, al_ref) + ab_ref[...]
    log_a = -_softplus(-x) * (1.0 / GLA_TAU)

    shift = t.bit_length() - 1
    ri = lax.broadcasted_iota(jnp.int32, (tb, tb), 0)
    ci = lax.broadcasted_iota(jnp.int32, (tb, tb), 1)
    tri = jnp.where((ci <= ri) & ((ri >> shift) == (ci >> shift)), 1.0, 0.0).astype(BF16)
    b = _dot_exact_lhs(tri, log_a)
    causal = (lax.broadcasted_iota(jnp.int32, (t, t), 1) <= lax.broadcasted_iota(jnp.int32, (t, t), 0))

    q_in = q * jnp.exp(b)
    heads = range(n_head)
    head = lambda x, h: x[:, h * GLA_DV:(h + 1) * GLA_DV]
    chunk = lambda x, c: x[c * t:(c + 1) * t]
    v_c = [chunk(v, c) for c in range(n_chunk)]
    k_rem = []
    e_last = []
    intra = []
    for c in range(n_chunk):
        b_c = chunk(b, c)
        b_mid = b_c[t // 2:t // 2 + 1]
        b_last = b_c[t - 1:t]
        q_mid = chunk(q, c) * jnp.exp(b_c - b_mid)
        k_mid = chunk(k, c) * jnp.exp(b_mid - b_c)
        k_rem.append(chunk(k, c) * jnp.exp(b_last - b_c))
        e_last.append(jnp.exp(b_last))
        intra.append([_dot(jnp.where(causal, _dot_nt(head(q_mid, h), head(k_mid, h)), 0.0), head(v_c[c], h))
                      for h in heads])
    sts = [st_ref[h] for h in heads]
    out_rows = []
    for c in range(n_chunk):
        q_c = chunk(q_in, c)
        o = [i_ + _dot_nt(head(q_c, h), s_) for h, (i_, s_) in enumerate(zip(intra[c], sts))]
        sts = [s_ * head(e_last[c], h) + _dot_tn(head(v_c[c], h), head(k_rem[c], h))
               for h, s_ in enumerate(sts)]
        out_rows.append(jnp.concatenate(
            [o_ * lax.rsqrt(jnp.mean(o_ * o_, axis=-1, keepdims=True) + NORM_EPS) for o_ in o], axis=1))
    for h in heads:
        st_ref[h] = sts[h]
    o = jnp.concatenate(out_rows, axis=0)
    o_ref[...] = (o * ng_ref[...] * (g * _sigmoid(g))).astype(o_ref.dtype)


def _gla_call(z, al, ab, ng):
    length, zc = z.shape
    width = ng.shape[1]
    full = lambda a: pl.BlockSpec(a.shape, lambda i: (0,) * a.ndim)
    return pl.pallas_call(
        _gla_kernel,
        out_shape=jax.ShapeDtypeStruct((length, width), BF16),
        grid=(length // GLA_BLOCK,),
        in_specs=[pl.BlockSpec((GLA_BLOCK, zc), lambda i: (i, 0)), full(al), full(ab), full(ng)],
        out_specs=pl.BlockSpec((GLA_BLOCK, width), lambda i: (i, 0)),
        scratch_shapes=[pltpu.VMEM((width // GLA_DV, GLA_DV, LANE), F32)],
        compiler_params=_params(("arbitrary",)),
        name="gla_mixer",
    )(z, al, ab, ng)


def _merge_kernel(ya_ref, yb_ref, yc_ref, gate_ref, x_ref, wua_ref, wub_ref, wuc_ref, wo_ref, g_ref,
                  xo_ref, h_ref):
    d = x_ref.shape[1]
    gates = gate_ref[...].astype(F32)
    merged = (gates[:, 0:d] * jnp.dot(ya_ref[...], wua_ref[...], preferred_element_type=F32)
              + gates[:, d:2 * d] * jnp.dot(yb_ref[...], wub_ref[...], preferred_element_type=F32)
              + gates[:, 2 * d:3 * d] * jnp.dot(yc_ref[...], wuc_ref[...], preferred_element_type=F32))
    x = x_ref[...] + _dot(merged, wo_ref[...])
    xo_ref[...] = x
    h_ref[...] = _rms_rows(x, g_ref[...]).astype(h_ref.dtype)


def _merge_call(ya, yb, yc, gates, x, wua, wub, wuc, wo, g, tm=256):
    m, d = x.shape
    rows = lambda a: pl.BlockSpec((tm, a.shape[1]), lambda i: (i, 0))
    const = lambda a: pl.BlockSpec(a.shape, lambda i: (0,) * a.ndim, pipeline_mode=pl.Buffered(1))
    return pl.pallas_call(
        _merge_kernel,
        out_shape=(jax.ShapeDtypeStruct((m, d), F32), jax.ShapeDtypeStruct((m, d), BF16)),
        grid=(m // tm,),
        in_specs=[rows(ya), rows(yb), rows(yc), rows(gates), rows(x),
                  const(wua), const(wub), const(wuc), const(wo), const(g)],
        out_specs=(pl.BlockSpec((tm, d), lambda i: (i, 0)), pl.BlockSpec((tm, d), lambda i: (i, 0))),
        compiler_params=_params(("parallel",)),
        name="merge_out_proj",
    )(ya, yb, yc, gates, x, wua, wub, wuc, wo, g)


def _mlp_kernel(final, h_ref, w1_ref, w2_ref, x_ref, g_ref, xo_ref, *rest):
    f = pl.program_id(1)

    @pl.when(f == 0)
    def _():
        xo_ref[...] = x_ref[...]

    hid = jnp.maximum(jnp.dot(h_ref[...], w1_ref[...], preferred_element_type=F32), 0.0)
    xo_ref[...] += _dot(hid * hid, w2_ref[...])

    @pl.when(f == pl.num_programs(1) - 1)
    def _():
        y = _rms_rows(xo_ref[...], g_ref[...])
        if final:
            xo_ref[...] = y
        else:
            rest[0][...] = y.astype(rest[0].dtype)


def _mlp_call(h, w1, w2, x, g_next, final, tm=512, tf=1024):
    m, d = x.shape
    ff = w1.shape[1]
    row_blk = pl.BlockSpec((tm, d), lambda i, f: (i, 0))
    out_shape = [jax.ShapeDtypeStruct((m, d), F32)]
    out_specs = [row_blk]
    if not final:
        out_shape.append(jax.ShapeDtypeStruct((m, d), BF16))
        out_specs.append(row_blk)
    return pl.pallas_call(
        functools.partial(_mlp_kernel, final),
        out_shape=tuple(out_shape),
        grid=(m // tm, ff // tf),
        in_specs=[row_blk,
                  pl.BlockSpec((d, tf), lambda i, f: (0, f)),
                  pl.BlockSpec((tf, d), lambda i, f: (f, 0)),
                  row_blk,
                  pl.BlockSpec((1, d), lambda i, f: (0, 0))],
        out_specs=tuple(out_specs),
        compiler_params=_params(("parallel", "arbitrary")),
        name="mlp",
    )(h, w1, w2, x, g_next)


def _pad_cols(a, n):
    return jnp.pad(a, ((0, 0), (0, n - a.shape[1])))


def _pad_rows(a, n):
    return jnp.pad(a, ((0, n - a.shape[0]), (0, 0)))


def _head_pad(a, heads, dh, to):
    lead = a.shape[:-1]
    a = a.reshape(lead + (heads, dh))
    a = jnp.pad(a, [(0, 0)] * len(lead) + [(0, 0), (0, to - dh)])
    return a.reshape(lead + (heads * to,))


def kernel(x, norm_mix, w_in, gate_bias, s5_lambda_re, s5_lambda_im, s5_log_step, s5_b_re, s5_b_im, s5_c_re, s5_c_im, s5_d, s5_glu_w, s5_glu_b, rwkv_mu, rwkv_w_lora, rwkv_w0, rwkv_a_lora, rwkv_a0, rwkv_g_lora, rwkv_k_k, rwkv_k_a, rwkv_r_k, rwkv_lnx_w, rwkv_lnx_b, rwkv_vres_a, rwkv_vres_mu, rwkv_vres_b, rwkv_vres_bias, gla_alpha_lora, gla_alpha_bias, gla_norm_g, w_up, w_out, norm_mlp, mlp_w1, mlp_w2, final_norm):
    bsz, length, d = x.shape
    depth = w_in.shape[0]
    s5_w = s5_d.shape[1]
    rw_w = rwkv_w0.shape[1]
    gla_v = gla_norm_g.shape[1]
    gla_heads = gla_v // GLA_DV
    gla_k = gla_heads * GLA_DK
    rw_cols = 3 * rw_w + RWKV_DECAY_LORA + RWKV_AAA_LORA + RWKV_GATE_LORA
    gla_cols = 2 * gla_k + 2 * gla_v + GLA_LORA
    o_rw = s5_w
    o_gla = o_rw + rw_cols
    o_gate = o_gla + gla_cols
    o1 = 3 * rw_w

    head_id = jnp.arange(rw_w) // RWKV_HEAD
    seg = (head_id[:, None] == jnp.arange(LANE)[None, :]).astype(BF16)
    seg_t = seg.T
    row1 = lambda a: a.reshape(1, -1).astype(F32)

    outs = []
    for b in range(bsz):
        xb = x[b].astype(F32)
        u = _norm_call(xb, row1(norm_mix[0]))
        v_first = None
        for l in range(depth):
            wl = w_in[l]
            w_s5 = wl[:, :o_rw].astype(BF16)
            wr = wl[:, o_rw:o_gla]
            pieces = [wr[:, :o1],
                      _pad_cols(wr[:, o1:o1 + RWKV_DECAY_LORA], LANE),
                      wr[:, o1 + RWKV_DECAY_LORA:o1 + RWKV_DECAY_LORA + RWKV_AAA_LORA],
                      wr[:, o1 + RWKV_DECAY_LORA + RWKV_AAA_LORA:]]
            mu = rwkv_mu[l]
            mu_pieces = [mu[:o1], jnp.pad(mu[o1:o1 + RWKV_DECAY_LORA], (0, LANE - RWKV_DECAY_LORA)),
                         mu[o1 + RWKV_DECAY_LORA:]]
            if l > 0:
                pieces.append(_pad_cols(rwkv_vres_a[l - 1], LANE))
                mu_pieces.append(jnp.pad(rwkv_vres_mu[l - 1], (0, LANE - RWKV_MV_LORA)))
            else:
                pieces.append(jnp.zeros((d, LANE), wl.dtype))
                mu_pieces.append(jnp.zeros((LANE,), mu.dtype))
            w_rw = jnp.concatenate(pieces, axis=1).astype(BF16)
            mu_rw = row1(jnp.concatenate(mu_pieces))
            wg = wl[:, o_gla:o_gate]
            w_gla = jnp.concatenate([
                _head_pad(wg[:, :gla_k], gla_heads, GLA_DK, GLA_DV),
                _head_pad(wg[:, gla_k:2 * gla_k], gla_heads, GLA_DK, GLA_DV),
                wg[:, 2 * gla_k:2 * gla_k + 2 * gla_v],
                _pad_cols(wg[:, 2 * gla_k + 2 * gla_v:], LANE)], axis=1).astype(BF16)
            w_gate = wl[:, o_gate:].astype(BF16)

            z_s5 = _mm_call(u, w_s5, tn=s5_w, name="in_proj_s5")
            z_rw = _mm_call(u, w_rw, tn=w_rw.shape[1] // 2, name="in_proj_rwkv")
            z_gla = _mm_call(u, w_gla, tn=w_gla.shape[1] // 3, name="in_proj_gla")
            gates = _mm_call(u, w_gate, tn=1024, bias=row1(gate_bias[l]), name="in_proj_gate")

            tabs = _s5_tables(s5_lambda_re[l], s5_lambda_im[l], s5_log_step[l],
                              s5_b_re[l].astype(F32), s5_b_im[l].astype(F32), s5_c_re[l], s5_c_im[l])
            y_a = _s5_call(z_s5, *tabs, row1(s5_d[l]), s5_glu_w[l].astype(BF16), row1(s5_glu_b[l]))

            wlo = _split_weight(_pad_rows(rwkv_w_lora[l], LANE))
            if l > 0:
                vb = _pad_rows(rwkv_vres_b[l - 1], LANE).astype(BF16)
                vbias = row1(rwkv_vres_bias[l - 1])
            else:
                vb = vbias = None
            res = _rwkv_call(z_rw, v_first, mu_rw, wlo, row1(rwkv_w0[l]),
                             rwkv_a_lora[l].astype(BF16), row1(rwkv_a0[l]), rwkv_g_lora[l].astype(BF16),
                             row1(rwkv_k_k[l]), row1(rwkv_k_a[l]), row1(rwkv_r_k[l]),
                             row1(rwkv_lnx_w[l]), row1(rwkv_lnx_b[l]), seg, seg_t, vb, vbias)
            if l == 0:
                y_b, v_first = res
            else:
                y_b = res

            al = _split_weight(_pad_rows(_head_pad(gla_alpha_lora[l], gla_heads, GLA_DK, GLA_DV), LANE))
            ab = row1(_head_pad(gla_alpha_bias[l], gla_heads, GLA_DK, GLA_DV))
            y_c = _gla_call(z_gla, al, ab, row1(gla_norm_g[l]))

            wu = w_up[l].astype(BF16)
            x_mid, h = _merge_call(y_a, y_b, y_c, gates, xb,
                                   wu[:s5_w], wu[s5_w:s5_w + rw_w], wu[s5_w + rw_w:],
                                   w_out[l].astype(BF16), row1(norm_mlp[l]))
            final = l == depth - 1
            g_next = row1(final_norm if final else norm_mix[l + 1])
            res = _mlp_call(h, mlp_w1[l].astype(BF16), mlp_w2[l].astype(BF16), x_mid, g_next, final)
            if final:
                xb = res[0]
            else:
                xb, u = res
        outs.append(xb.astype(x.dtype))
    return jnp.stack(outs, axis=0)
```

```python
import functools
import math

import jax
import jax.numpy as jnp
from jax import lax
from jax.experimental import pallas as pl
from jax.experimental.pallas import tpu as pltpu

F32 = jnp.float32
BF16 = jnp.bfloat16

LANE = 128
SUBLANE = 8
NORM_EPS = 1e-6

S5_GROUP = 16
S5_STATE = 64
S5_SLAB = 256
S5_SLAB_STATES = (S5_SLAB // S5_GROUP) * S5_STATE
S5_T = 256
S5_BLOCK_LEVELS = 3
S5_FOLD = 2

RWKV_HEAD = 64
RWKV_T = 64
RWKV_BLOCK = 256
RWKV_DECAY_LORA = 96
RWKV_AAA_LORA = 128
RWKV_GATE_LORA = 256
RWKV_MV_LORA = 64
RWKV_LNX_EPS = 64e-5

GLA_DK = 64
GLA_DV = 128
GLA_LORA = 16
GLA_TAU = 16.0
GLA_T = 64
GLA_BLOCK = 256

VMEM_LIMIT = 56 * 1024 * 1024


def _dot(a, b):
    return jnp.dot(a.astype(BF16), b.astype(BF16), preferred_element_type=F32)


def _dot_nt(a, b):
    return lax.dot_general(a.astype(BF16), b.astype(BF16), (((1,), (1,)), ((), ())),
                           preferred_element_type=F32)


def _dot_tn(a, b):
    return lax.dot_general(a.astype(BF16), b.astype(BF16), (((0,), (0,)), ((), ())),
                           preferred_element_type=F32)


def _split3(x):
    hi = x.astype(BF16)
    r1 = x - hi.astype(F32)
    mid = r1.astype(BF16)
    lo = (r1 - mid.astype(F32)).astype(BF16)
    return hi, mid, lo


def _dot_exact_lhs(m_bf16, x):
    hi, mid, lo = _split3(x)
    return (jnp.dot(m_bf16, hi, preferred_element_type=F32)
            + jnp.dot(m_bf16, mid, preferred_element_type=F32)
            + jnp.dot(m_bf16, lo, preferred_element_type=F32))


def _hi_lo(x):
    hi = x.astype(BF16)
    return hi, (x - hi.astype(F32)).astype(BF16)


def _seg_sum(x, seg, seg_t):
    s = jnp.dot(x.astype(BF16), seg, preferred_element_type=F32)
    hi, lo = _hi_lo(s)
    return jnp.dot(hi, seg_t, preferred_element_type=F32) + jnp.dot(lo, seg_t, preferred_element_type=F32)


def _split_weight(w):
    hi, lo = _hi_lo(w.astype(F32))
    return jnp.stack([hi, lo])


def _dot_hp(a, w_ref):
    a_hi, a_lo = _hi_lo(a)
    return (jnp.dot(a_hi, w_ref[0], preferred_element_type=F32)
            + jnp.dot(a_lo, w_ref[0], preferred_element_type=F32)
            + jnp.dot(a_hi, w_ref[1], preferred_element_type=F32))


def _sigmoid(x):
    return 1.0 / (1.0 + jnp.exp(-x))


def _softplus(x):
    return jnp.maximum(x, 0.0) + jnp.log(1.0 + jnp.exp(-jnp.abs(x)))


def _rms_rows(x, g):
    return x * lax.rsqrt(jnp.mean(x * x, axis=-1, keepdims=True) + NORM_EPS) * g


def _shift_rows(x, d):
    n = x.shape[0]
    if d % 8 == 0:
        return jnp.concatenate([jnp.zeros((d, x.shape[1]), x.dtype), x[:n - d]], axis=0)
    rolled = pltpu.roll(x, d, 0)
    row = lax.broadcasted_iota(jnp.int32, x.shape, 0)
    return jnp.where(row >= d, rolled, 0.0)


def _params(sem):
    return pltpu.CompilerParams(dimension_semantics=sem, vmem_limit_bytes=VMEM_LIMIT)


def _norm_kernel(x_ref, g_ref, o_ref):
    o_ref[...] = _rms_rows(x_ref[...], g_ref[...]).astype(o_ref.dtype)


def _norm_call(x, g, tm=512):
    m, d = x.shape
    return pl.pallas_call(
        _norm_kernel,
        out_shape=jax.ShapeDtypeStruct((m, d), BF16),
        grid=(m // tm,),
        in_specs=[pl.BlockSpec((tm, d), lambda i: (i, 0)),
                  pl.BlockSpec((1, d), lambda i: (0, 0))],
        out_specs=pl.BlockSpec((tm, d), lambda i: (i, 0)),
        compiler_params=_params(("parallel",)),
        name="rmsnorm",
    )(x, g)


def _mm_kernel(a_ref, w_ref, o_ref):
    o_ref[...] = jnp.dot(a_ref[...], w_ref[...], preferred_element_type=F32).astype(o_ref.dtype)


def _mm_gate_kernel(a_ref, w_ref, b_ref, o_ref):
    z = jnp.dot(a_ref[...], w_ref[...], preferred_element_type=F32) + b_ref[...]
    o_ref[...] = _sigmoid(z).astype(o_ref.dtype)


def _mm_call(a, w, tn, bias=None, tm=1024, name="in_proj"):
    m, k = a.shape
    n = w.shape[1]
    in_specs = [pl.BlockSpec((tm, k), lambda i, j: (i, 0)),
                pl.BlockSpec((k, tn), lambda i, j: (0, j))]
    args = [a, w]
    if bias is None:
        body, out_dtype = _mm_kernel, F32
    else:
        body, out_dtype = _mm_gate_kernel, BF16
        in_specs.append(pl.BlockSpec((1, tn), lambda i, j: (0, j)))
        args.append(bias)
    return pl.pallas_call(
        body,
        out_shape=jax.ShapeDtypeStruct((m, n), out_dtype),
        grid=(m // tm, n // tn),
        in_specs=in_specs,
        out_specs=pl.BlockSpec((tm, tn), lambda i, j: (i, j)),
        compiler_params=_params(("parallel", "parallel")),
        name=name,
    )(*args)


def _s5_kernel(z_ref, bre_ref, bim_ref, pre_ref, pim_ref, cre_ref, cim_ref, d_ref, gw_ref, gb_ref,
               o_ref, carry_re, carry_im):
    n_slab = bre_ref.shape[0]
    tiles_per_slab = S5_SLAB_STATES // LANE
    t = z_ref.shape[0]

    @pl.when(pl.program_id(0) == 0)
    def _():
        carry_re[...] = jnp.zeros_like(carry_re)
        carry_im[...] = jnp.zeros_like(carry_im)

    u = z_ref[...]
    ub = u.astype(BF16)
    n_blk = t // SUBLANE
    row_in_block = lax.broadcasted_iota(jnp.int32, u.shape, 0) & (SUBLANE - 1)
    shifted = [ub] + [jnp.where(row_in_block >= d, pltpu.roll(u, d, 0), 0.0).astype(BF16)
                      for d in range(1, S5_FOLD)]
    ys = []
    for j in range(n_slab):
        uj = jnp.concatenate([s_[:, j * S5_SLAB:(j + 1) * S5_SLAB] for s_ in shifted], axis=1)
        bu_re = jnp.dot(uj, bre_ref[j], preferred_element_type=F32)
        bu_im = jnp.dot(uj, bim_ref[j], preferred_element_type=F32)
        tiles_re = []
        tiles_im = []
        for c in range(tiles_per_slab):
            idx = j * tiles_per_slab + c
            sr = bu_re[:, c * LANE:(c + 1) * LANE].reshape(n_blk, SUBLANE, LANE)
            si = bu_im[:, c * LANE:(c + 1) * LANE].reshape(n_blk, SUBLANE, LANE)
            pr = pre_ref[idx]
            pi = pim_ref[idx]
            for k in range(S5_FOLD.bit_length() - 1, S5_BLOCK_LEVELS):
                qr = pltpu.roll(sr, 1 << k, 1)
                qi = pltpu.roll(si, 1 << k, 1)
                sr, si = sr + (pr[k] * qr - pi[k] * qi), si + (pr[k] * qi + pi[k] * qr)
            cr = jnp.broadcast_to(carry_re[idx], (SUBLANE, LANE))
            ci = jnp.broadcast_to(carry_im[idx], (SUBLANE, LANE))
            ar = pr[S5_BLOCK_LEVELS]
            ai = pi[S5_BLOCK_LEVELS]
            blocks_re = []
            blocks_im = []
            for b in range(n_blk):
                br = sr[b] + (ar * cr - ai * ci)
                bi = si[b] + (ar * ci + ai * cr)
                blocks_re.append(br)
                blocks_im.append(bi)
                cr = jnp.broadcast_to(br[SUBLANE - 1:SUBLANE], (SUBLANE, LANE))
                ci = jnp.broadcast_to(bi[SUBLANE - 1:SUBLANE], (SUBLANE, LANE))
            carry_re[idx] = cr[0:1]
            carry_im[idx] = ci[0:1]
            tiles_re.append(jnp.concatenate(blocks_re, axis=0).astype(BF16))
            tiles_im.append(jnp.concatenate(blocks_im, axis=0).astype(BF16))
        s_re = jnp.concatenate(tiles_re, axis=1)
        s_im = jnp.concatenate(tiles_im, axis=1)
        ys.append(jnp.dot(s_re, cre_ref[j], preferred_element_type=F32)
                  - jnp.dot(s_im, cim_ref[j], preferred_element_type=F32))
    y = jnp.concatenate(ys, axis=1) + d_ref[...] * u
    y = y * (0.5 * (1.0 + jnp.tanh(math.sqrt(2.0 / math.pi) * (y + 0.044715 * (y * y * y)))))
    y = y * _sigmoid(_dot(y, gw_ref[...]) + gb_ref[...])
    o_ref[...] = y.astype(o_ref.dtype)


def _s5_call(z, bre, bim, pre, pim, cre, cim, d, gw, gb):
    length, width = z.shape
    n_slab = bre.shape[0]
    n_tiles = n_slab * (S5_SLAB_STATES // LANE)
    full = lambda a: pl.BlockSpec(a.shape, lambda i: (0,) * a.ndim, pipeline_mode=pl.Buffered(1))
    return pl.pallas_call(
        _s5_kernel,
        out_shape=jax.ShapeDtypeStruct((length, width), BF16),
        grid=(length // S5_T,),
        in_specs=[pl.BlockSpec((S5_T, width), lambda i: (i, 0)),
                  full(bre), full(bim), full(pre), full(pim), full(cre), full(cim),
                  full(d), full(gw), full(gb)],
        out_specs=pl.BlockSpec((S5_T, width), lambda i: (i, 0)),
        scratch_shapes=[pltpu.VMEM((n_tiles, 1, LANE), F32),
                        pltpu.VMEM((n_tiles, 1, LANE), F32)],
        compiler_params=_params(("arbitrary",)),
        name="s5_mixer",
    )(z, bre, bim, pre, pim, cre, cim, d, gw, gb)


def _s5_tables(lam_re, lam_im, log_step, b_re, b_im, c_re, c_im):
    groups = lam_re.shape[0]
    n_slab = groups * S5_GROUP // S5_SLAB
    gps = S5_SLAB // S5_GROUP
    lr = jnp.minimum(lam_re.astype(F32), -1e-4)
    li = lam_im.astype(F32)
    dt = jnp.exp(log_step.astype(F32))[:, None]
    e = jnp.exp(lr * dt)
    lb_re = e * jnp.cos(li * dt)
    lb_im = e * jnp.sin(li * dt)
    den = lr * lr + li * li
    f_re = ((lb_re - 1.0) * lr + lb_im * li) / den
    f_im = (lb_im * lr - (lb_re - 1.0) * li) / den
    bb_re = f_re[..., None] * b_re - f_im[..., None] * b_im
    bb_im = f_re[..., None] * b_im + f_im[..., None] * b_re
    eye = jnp.eye(gps, dtype=F32)

    def bd_b(m):
        m = m.reshape(n_slab, gps, S5_STATE, S5_GROUP)
        return jnp.einsum('jgpi,gh->jgihp', m, eye).reshape(n_slab, S5_SLAB, S5_SLAB_STATES).astype(BF16)

    def bd_c(m):
        m = m.reshape(n_slab, gps, S5_GROUP, S5_STATE)
        return jnp.einsum('jgip,gh->jgphi', m, eye).reshape(n_slab, S5_SLAB_STATES, S5_SLAB).astype(BF16)

    r_idx = jnp.arange(SUBLANE)
    steps = 2 ** jnp.arange(S5_BLOCK_LEVELS)
    expo = jnp.concatenate([jnp.broadcast_to(steps[:, None], (S5_BLOCK_LEVELS, SUBLANE)),
                            (r_idx + 1)[None, :]], axis=0).astype(F32)
    keep = jnp.concatenate([r_idx[None, :] >= steps[:, None],
                            jnp.ones((1, SUBLANE), bool)], axis=0).astype(F32)
    ph = (li * dt).reshape(-1) * expo[..., None]
    mag = keep[..., None] * jnp.exp((lr * dt).reshape(-1) * expo[..., None])
    n_tiles = ph.shape[-1] // LANE
    tile_major = lambda a: a.reshape(a.shape[0], SUBLANE, n_tiles, LANE).transpose(2, 0, 1, 3)
    pw_re = tile_major(mag * jnp.cos(ph))
    pw_im = tile_major(mag * jnp.sin(ph))
    fold = jnp.arange(S5_FOLD, dtype=F32)[:, None, None]
    fd_mag = jnp.exp(lr * dt * fold)
    fd_re = (fd_mag * jnp.cos(li * dt * fold))[..., None]
    fd_im = (fd_mag * jnp.sin(li * dt * fold))[..., None]
    b_stack_re = jnp.concatenate([bd_b(fd_re[d] * bb_re - fd_im[d] * bb_im) for d in range(S5_FOLD)], axis=1)
    b_stack_im = jnp.concatenate([bd_b(fd_re[d] * bb_im + fd_im[d] * bb_re) for d in range(S5_FOLD)], axis=1)
    return (b_stack_re, b_stack_im, pw_re, pw_im,
            bd_c(c_re.astype(F32)), bd_c(c_im.astype(F32)))


def _rwkv_kernel(has_vres, *refs):
    if has_vres:
        (z_ref, vf_ref, mu_ref, wl_ref, w0_ref, al_ref, a0_ref, gl_ref, kk_ref, ka_ref, rk_ref,
         lw_ref, lb_ref, seg_ref, segt_ref, vb_ref, vbias_ref, o_ref, prev_ref, h_ref) = refs
    else:
        (z_ref, mu_ref, wl_ref, w0_ref, al_ref, a0_ref, gl_ref, kk_ref, ka_ref, rk_ref,
         lw_ref, lb_ref, seg_ref, segt_ref, o_ref, vo_ref, prev_ref, h_ref) = refs
    tb = z_ref.shape[0]
    t = RWKV_T
    n_chunk = tb // t
    width = o_ref.shape[1]
    n_pair = width // LANE

    @pl.when(pl.program_id(0) == 0)
    def _():
        prev_ref[...] = jnp.zeros_like(prev_ref)
        h_ref[...] = jnp.zeros_like(h_ref)

    z = z_ref[...]
    rowz = lax.broadcasted_iota(jnp.int32, z.shape, 0)
    prev = jnp.where(rowz == 0, prev_ref[...], pltpu.roll(z, 1, 0))
    prev_ref[...] = z[tb - 1:tb]
    zs = z + (prev - z) * mu_ref[...]

    o1 = 3 * width
    r = zs[:, 0:width]
    k = zs[:, width:2 * width]
    v = zs[:, 2 * width:o1]
    w_in = zs[:, o1:o1 + LANE]
    a_in = zs[:, o1 + LANE:o1 + 2 * LANE]
    g_in = zs[:, o1 + 2 * LANE:o1 + 2 * LANE + RWKV_GATE_LORA]

    wpre = w0_ref[...] + _dot_hp(jnp.tanh(w_in), wl_ref)
    logw = -jnp.exp(-_softplus(-wpre) - 0.5)
    a = _sigmoid(a0_ref[...] + _dot(a_in, al_ref[...]))
    g = _dot(_sigmoid(g_in), gl_ref[...])
    if has_vres:
        vr = zs[:, o1 + 2 * LANE + RWKV_GATE_LORA:]
        vg = _sigmoid(vbias_ref[...] + _dot(vr, vb_ref[...]))
        v = v + (vf_ref[...] - v) * vg
    else:
        vo_ref[...] = v

    seg = seg_ref[...]
    seg_t = segt_ref[...]
    kk = k * kk_ref[...]
    kk = kk * lax.rsqrt(jnp.maximum(_seg_sum(kk * kk, seg, seg_t), 1e-24))
    k2 = k * (1.0 + (a - 1.0) * ka_ref[...])
    kka = kk * a

    shift = t.bit_length() - 1
    ri = lax.broadcasted_iota(jnp.int32, (tb, tb), 0)
    ci = lax.broadcasted_iota(jnp.int32, (tb, tb), 1)
    tri = jnp.where((ci <= ri) & ((ri >> shift) == (ci >> shift)), 1.0, 0.0).astype(BF16)
    lc = _dot_exact_lhs(tri, logw)

    gi = lax.broadcasted_iota(jnp.int32, (4 * t, 4 * t), 0)
    gj = lax.broadcasted_iota(jnp.int32, (4 * t, 4 * t), 1)
    ti = gi & (t - 1)
    sj = gj & (t - 1)
    keep = sj + jnp.where(gi < 2 * t, 1, 0) <= ti
    lane = lax.broadcasted_iota(jnp.int32, (t, LANE), 1)
    head0 = lane < RWKV_HEAD

    def stack2(x):
        return jnp.concatenate([jnp.where(head0, x, 0.0), jnp.where(head0, 0.0, x)], axis=0)

    e_pos = jnp.exp(lc)
    e_neg = jnp.exp(-lc)
    e_prev = jnp.exp(lc - logw)
    rt_all = r * e_pos
    at_all = -kk * e_prev
    bt_all = kka * e_neg
    kt_all = k2 * e_neg

    items = [(c, p) for c in range(n_chunk) for p in range(n_pair)]
    tile = lambda x, c, p: x[c * t:(c + 1) * t, p * LANE:(p + 1) * LANE]
    at = [tile(at_all, c, p) for c, p in items]
    rt = [tile(rt_all, c, p) for c, p in items]
    vbd = [stack2(tile(v, c, p)) for c, p in items]
    gm = [jnp.where(keep, _dot_nt(jnp.concatenate([stack2(a_), stack2(r_)], axis=0),
                                  jnp.concatenate([stack2(tile(bt_all, c, p)), stack2(tile(kt_all, c, p))],
                                                  axis=0)), 0.0)
          for a_, r_, (c, p) in zip(at, rt, items)]
    n1 = [g_[0:2 * t, 0:2 * t] for g_ in gm]
    rhs0 = [_dot(g_[0:2 * t, 2 * t:4 * t], v_) for g_, v_ in zip(gm, vbd)]
    qi = lax.broadcasted_iota(jnp.int32, (2 * t, 2 * t), 0)
    qj = lax.broadcasted_iota(jnp.int32, (2 * t, 2 * t), 1)
    xinv = [jnp.where(qi == qj, 1.0, jnp.where((qi ^ qj) == 1, n_, 0.0)) for n_ in n1]
    for lvl in range(1, shift):
        couple = ((qi >> lvl) ^ (qj >> lvl)) == 1
        xm = [_dot(x_, jnp.where(couple, n_, 0.0)) for x_, n_ in zip(xinv, n1)]
        xinv = [x_ + _dot(m_, x_) for x_, m_ in zip(xinv, xm)]

    hts = [h_ref[p] for p in range(n_pair)]
    out_rows = []
    for c in range(n_chunk):
        idx = [c * n_pair + p for p in range(n_pair)]
        lc_c = lc[c * t:(c + 1) * t]
        e_rem = jnp.exp(lc_c[t - 1:t] - lc_c)
        bh_c = kka[c * t:(c + 1) * t] * e_rem
        kh_c = k2[c * t:(c + 1) * t] * e_rem
        e_last = e_pos[(c + 1) * t - 1:(c + 1) * t]
        ph = [_dot_nt(jnp.concatenate([at[i], rt[i]], axis=0), h_) for i, h_ in zip(idx, hts)]
        u = [_dot(xinv[i], stack2(p_[0:t]) + rhs0[i]) for i, p_ in zip(idx, ph)]
        uv = [jnp.concatenate([u_, vbd[i]], axis=0) for i, u_ in zip(idx, u)]
        opk = [stack2(p_[t:2 * t]) + _dot(gm[i][2 * t:4 * t, :], uv_) for i, p_, uv_ in zip(idx, ph, uv)]
        out_rows.append(jnp.concatenate([o_[0:t] + o_[t:2 * t] for o_ in opk], axis=1))
        hts = [h_ * e_last[:, p * LANE:(p + 1) * LANE]
               + _dot_tn(uv_, jnp.concatenate([stack2(bh_c[:, p * LANE:(p + 1) * LANE]),
                                               stack2(kh_c[:, p * LANE:(p + 1) * LANE])], axis=0))
               for p, (h_, uv_) in enumerate(zip(hts, uv))]
    for p in range(n_pair):
        h_ref[p] = hts[p]

    y = jnp.concatenate(out_rows, axis=0)
    inv_n = 1.0 / RWKV_HEAD
    mean = _seg_sum(y, seg, seg_t) * inv_n
    yc = y - mean
    var = _seg_sum(yc * yc, seg, seg_t) * inv_n
    yn = yc * lax.rsqrt(var + RWKV_LNX_EPS) * lw_ref[...] + lb_ref[...]
    bonus = _seg_sum(r * k2 * rk_ref[...], seg, seg_t) * v
    o_ref[...] = ((yn + bonus) * g).astype(o_ref.dtype)


def _rwkv_call(z, v_first, mu, wl, w0, al, a0, gl, k_k, k_a, r_k, lnx_w, lnx_b, seg, seg_t, vb, vbias):
    length, zc = z.shape
    width = w0.shape[1]
    has_vres = v_first is not None
    full = lambda a: pl.BlockSpec(a.shape, lambda i: (0,) * a.ndim)
    rows = lambda c: pl.BlockSpec((RWKV_BLOCK, c), lambda i: (i, 0))
    common = [mu, wl, w0, al, a0, gl, k_k, k_a, r_k, lnx_w, lnx_b, seg, seg_t]
    if has_vres:
        args = [z, v_first] + common + [vb, vbias]
        in_specs = [rows(zc), rows(width)] + [full(a) for a in common + [vb, vbias]]
        out_shape = jax.ShapeDtypeStruct((length, width), BF16)
        out_specs = rows(width)
    else:
        args = [z] + common
        in_specs = [rows(zc)] + [full(a) for a in common]
        out_shape = (jax.ShapeDtypeStruct((length, width), BF16),
                     jax.ShapeDtypeStruct((length, width), F32))
        out_specs = (rows(width), rows(width))
    return pl.pallas_call(
        functools.partial(_rwkv_kernel, has_vres),
        out_shape=out_shape,
        grid=(length // RWKV_BLOCK,),
        in_specs=in_specs,
        out_specs=out_specs,
        scratch_shapes=[pltpu.VMEM((1, zc), F32),
                        pltpu.VMEM((width // LANE, LANE, LANE), F32)],
        compiler_params=_params(("arbitrary",)),
        name="rwkv7_mixer",
    )(*args)


def _gla_kernel(z_ref, al_ref, ab_ref, ng_ref, o_ref, st_ref):
    tb = z_ref.shape[0]
    t = GLA_T
    n_chunk = tb // t
    width = o_ref.shape[1]
    n_head = width // GLA_DV

    @pl.when(pl.program_id(0) == 0)
    def _():
        st_ref[...] = jnp.zeros_like(st_ref)

    z = z_ref[...]
    q = z[:, 0:width] * (GLA_DK ** -0.5)
    k = z[:, width:2 * width]
    v = z[:, 2 * width:3 * width]
    g = z[:, 3 * width:4 * width]
    a_in = z[:, 4 * width:]
    x = _dot_hp(a_in, al_ref) + ab_ref[...]
    log_a = -_softplus(-x) * (1.0 / GLA_TAU)

    shift = t.bit_length() - 1
    ri = lax.broadcasted_iota(jnp.int32, (tb, tb), 0)
    ci = lax.broadcasted_iota(jnp.int32, (tb, tb), 1)
    tri = jnp.where((ci <= ri) & ((ri >> shift) == (ci >> shift)), 1.0, 0.0).astype(BF16)
    b = _dot_exact_lhs(tri, log_a)
    causal = (lax.broadcasted_iota(jnp.int32, (t, t), 1) <= lax.broadcasted_iota(jnp.int32, (t, t), 0))

    q_in = q * jnp.exp(b)
    heads = range(n_head)
    head = lambda x, h: x[:, h * GLA_DV:(h + 1) * GLA_DV]
    chunk = lambda x, c: x[c * t:(c + 1) * t]
    v_c = [chunk(v, c) for c in range(n_chunk)]
    k_rem = []
    e_last = []
    intra = []
    for c in range(n_chunk):
        b_c = chunk(b, c)
        b_mid = b_c[t // 2:t // 2 + 1]
        b_last = b_c[t - 1:t]
        q_mid = chunk(q, c) * jnp.exp(b_c - b_mid)
        k_mid = chunk(k, c) * jnp.exp(b_mid - b_c)
        k_rem.append(chunk(k, c) * jnp.exp(b_last - b_c))
        e_last.append(jnp.exp(b_last))
        intra.append([_dot(jnp.where(causal, _dot_nt(head(q_mid, h), head(k_mid, h)), 0.0), head(v_c[c], h))
                      for h in heads])
    sts = [st_ref[h] for h in heads]
    out_rows = []
    for c in range(n_chunk):
        q_c = chunk(q_in, c)
        o = [i_ + _dot_nt(head(q_c, h), s_) for h, (i_, s_) in enumerate(zip(intra[c], sts))]
        sts = [s_ * head(e_last[c], h) + _dot_tn(head(v_c[c], h), head(k_rem[c], h))
               for h, s_ in enumerate(sts)]
        out_rows.append(jnp.concatenate(
            [o_ * lax.rsqrt(jnp.mean(o_ * o_, axis=-1, keepdims=True) + NORM_EPS) for o_ in o], axis=1))
    for h in heads:
        st_ref[h] = sts[h]
    o = jnp.concatenate(out_rows, axis=0)
    o_ref[...] = (o * ng_ref[...] * (g * _sigmoid(g))).astype(o_ref.dtype)


def _gla_call(z, al, ab, ng):
    length, zc = z.shape
    width = ng.shape[1]
    full = lambda a: pl.BlockSpec(a.shape, lambda i: (0,) * a.ndim)
    return pl.pallas_call(
        _gla_kernel,
        out_shape=jax.ShapeDtypeStruct((length, width), BF16),
        grid=(length // GLA_BLOCK,),
        in_specs=[pl.BlockSpec((GLA_BLOCK, zc), lambda i: (i, 0)), full(al), full(ab), full(ng)],
        out_specs=pl.BlockSpec((GLA_BLOCK, width), lambda i: (i, 0)),
        scratch_shapes=[pltpu.VMEM((width // GLA_DV, GLA_DV, LANE), F32)],
        compiler_params=_params(("arbitrary",)),
        name="gla_mixer",
    )(z, al, ab, ng)


def _merge_kernel(ya_ref, yb_ref, yc_ref, gate_ref, x_ref, wua_ref, wub_ref, wuc_ref, wo_ref, g_ref,
                  xo_ref, h_ref):
    d = x_ref.shape[1]
    gates = gate_ref[...].astype(F32)
    merged = (gates[:, 0:d] * jnp.dot(ya_ref[...], wua_ref[...], preferred_element_type=F32)
              + gates[:, d:2 * d] * jnp.dot(yb_ref[...], wub_ref[...], preferred_element_type=F32)
              + gates[:, 2 * d:3 * d] * jnp.dot(yc_ref[...], wuc_ref[...], preferred_element_type=F32))
    x = x_ref[...] + _dot(merged, wo_ref[...])
    xo_ref[...] = x
    h_ref[...] = _rms_rows(x, g_ref[...]).astype(h_ref.dtype)


def _merge_call(ya, yb, yc, gates, x, wua, wub, wuc, wo, g, tm=256):
    m, d = x.shape
    rows = lambda a: pl.BlockSpec((tm, a.shape[1]), lambda i: (i, 0))
    const = lambda a: pl.BlockSpec(a.shape, lambda i: (0,) * a.ndim, pipeline_mode=pl.Buffered(1))
    return pl.pallas_call(
        _merge_kernel,
        out_shape=(jax.ShapeDtypeStruct((m, d), F32), jax.ShapeDtypeStruct((m, d), BF16)),
        grid=(m // tm,),
        in_specs=[rows(ya), rows(yb), rows(yc), rows(gates), rows(x),
                  const(wua), const(wub), const(wuc), const(wo), const(g)],
        out_specs=(pl.BlockSpec((tm, d), lambda i: (i, 0)), pl.BlockSpec((tm, d), lambda i: (i, 0))),
        compiler_params=_params(("parallel",)),
        name="merge_out_proj",
    )(ya, yb, yc, gates, x, wua, wub, wuc, wo, g)


def _mlp_kernel(final, h_ref, w1_ref, w2_ref, x_ref, g_ref, xo_ref, *rest):
    f = pl.program_id(1)

    @pl.when(f == 0)
    def _():
        xo_ref[...] = x_ref[...]

    hid = jnp.maximum(jnp.dot(h_ref[...], w1_ref[...], preferred_element_type=F32), 0.0)
    xo_ref[...] += _dot(hid * hid, w2_ref[...])

    @pl.when(f == pl.num_programs(1) - 1)
    def _():
        y = _rms_rows(xo_ref[...], g_ref[...])
        if final:
            xo_ref[...] = y
        else:
            rest[0][...] = y.astype(rest[0].dtype)


def _mlp_call(h, w1, w2, x, g_next, final, tm=512, tf=1024):
    m, d = x.shape
    ff = w1.shape[1]
    row_blk = pl.BlockSpec((tm, d), lambda i, f: (i, 0))
    out_shape = [jax.ShapeDtypeStruct((m, d), F32)]
    out_specs = [row_blk]
    if not final:
        out_shape.append(jax.ShapeDtypeStruct((m, d), BF16))
        out_specs.append(row_blk)
    return pl.pallas_call(
        functools.partial(_mlp_kernel, final),
        out_shape=tuple(out_shape),
        grid=(m // tm, ff // tf),
        in_specs=[row_blk,
                  pl.BlockSpec((d, tf), lambda i, f: (0, f)),
                  pl.BlockSpec((tf, d), lambda i, f: (f, 0)),
                  row_blk,
                  pl.BlockSpec((1, d), lambda i, f: (0, 0))],
        out_specs=tuple(out_specs),
        compiler_params=_params(("parallel", "arbitrary")),
        name="mlp",
    )(h, w1, w2, x, g_next)


def _pad_cols(a, n):
    return jnp.pad(a, [(0, 0)] * (a.ndim - 1) + [(0, n - a.shape[-1])])


def _pad_rows(a, n):
    return jnp.pad(a, [(0, 0)] * (a.ndim - 2) + [(0, n - a.shape[-2]), (0, 0)])


def _head_pad(a, heads, dh, to):
    lead = a.shape[:-1]
    a = a.reshape(lead + (heads, dh))
    a = jnp.pad(a, [(0, 0)] * len(lead) + [(0, 0), (0, to - dh)])
    return a.reshape(lead + (heads * to,))


def kernel(x, norm_mix, w_in, gate_bias, s5_lambda_re, s5_lambda_im, s5_log_step, s5_b_re, s5_b_im, s5_c_re, s5_c_im, s5_d, s5_glu_w, s5_glu_b, rwkv_mu, rwkv_w_lora, rwkv_w0, rwkv_a_lora, rwkv_a0, rwkv_g_lora, rwkv_k_k, rwkv_k_a, rwkv_r_k, rwkv_lnx_w, rwkv_lnx_b, rwkv_vres_a, rwkv_vres_mu, rwkv_vres_b, rwkv_vres_bias, gla_alpha_lora, gla_alpha_bias, gla_norm_g, w_up, w_out, norm_mlp, mlp_w1, mlp_w2, final_norm):
    bsz, length, d = x.shape
    depth = w_in.shape[0]
    s5_w = s5_d.shape[1]
    rw_w = rwkv_w0.shape[1]
    gla_v = gla_norm_g.shape[1]
    gla_heads = gla_v // GLA_DV
    gla_k = gla_heads * GLA_DK
    rw_cols = 3 * rw_w + RWKV_DECAY_LORA + RWKV_AAA_LORA + RWKV_GATE_LORA
    gla_cols = 2 * gla_k + 2 * gla_v + GLA_LORA
    o_rw = s5_w
    o_gla = o_rw + rw_cols
    o_gate = o_gla + gla_cols
    o1 = 3 * rw_w

    head_id = jnp.arange(rw_w) // RWKV_HEAD
    seg = (head_id[:, None] == jnp.arange(LANE)[None, :]).astype(BF16)
    seg_t = seg.T
    rows1 = lambda a: a.reshape(a.shape[0], 1, -1).astype(F32)

    w_in_bf = w_in.astype(BF16)
    w_s5 = w_in_bf[:, :, :o_rw]
    wr = w_in_bf[:, :, o_rw:o_gla]
    o2 = o1 + RWKV_DECAY_LORA
    vres_a = jnp.concatenate([jnp.zeros((1,) + rwkv_vres_a.shape[1:], BF16), rwkv_vres_a.astype(BF16)], axis=0)
    vres_mu = jnp.concatenate([jnp.zeros((1,) + rwkv_vres_mu.shape[1:], F32), rwkv_vres_mu.astype(F32)], axis=0)
    w_rw = jnp.concatenate([wr[:, :, :o1], _pad_cols(wr[:, :, o1:o2], LANE), wr[:, :, o2:],
                            _pad_cols(vres_a, LANE)], axis=2)
    mu_rw = rows1(jnp.concatenate([rwkv_mu[:, :o1], _pad_cols(rwkv_mu[:, o1:o2], LANE), rwkv_mu[:, o2:],
                                   _pad_cols(vres_mu, LANE)], axis=1))
    wg = w_in_bf[:, :, o_gla:o_gate]
    w_gla = jnp.concatenate([_head_pad(wg[:, :, :gla_k], gla_heads, GLA_DK, GLA_DV),
                             _head_pad(wg[:, :, gla_k:2 * gla_k], gla_heads, GLA_DK, GLA_DV),
                             wg[:, :, 2 * gla_k:2 * gla_k + 2 * gla_v],
                             _pad_cols(wg[:, :, 2 * gla_k + 2 * gla_v:], LANE)], axis=2)
    w_gate = w_in_bf[:, :, o_gate:]
    gate_b = rows1(gate_bias)
    s5_tabs = jax.vmap(_s5_tables)(s5_lambda_re, s5_lambda_im, s5_log_step, s5_b_re.astype(F32),
                                   s5_b_im.astype(F32), s5_c_re, s5_c_im)
    s5_dd, s5_gw, s5_gb = rows1(s5_d), s5_glu_w.astype(BF16), rows1(s5_glu_b)
    rw_wl = jax.vmap(_split_weight)(_pad_rows(rwkv_w_lora, LANE))
    rw_vb = _pad_rows(rwkv_vres_b, LANE).astype(BF16)
    rw_rows = [rows1(a) for a in (rwkv_w0, rwkv_a0, rwkv_k_k, rwkv_k_a, rwkv_r_k, rwkv_lnx_w, rwkv_lnx_b)]
    rw_al, rw_gl, rw_vbias = rwkv_a_lora.astype(BF16), rwkv_g_lora.astype(BF16), rows1(rwkv_vres_bias)
    gla_al = jax.vmap(_split_weight)(_pad_rows(_head_pad(gla_alpha_lora, gla_heads, GLA_DK, GLA_DV), LANE))
    gla_ab = rows1(_head_pad(gla_alpha_bias, gla_heads, GLA_DK, GLA_DV))
    gla_ng = rows1(gla_norm_g)
    wu_bf, wo_bf = w_up.astype(BF16), w_out.astype(BF16)
    w1_bf, w2_bf = mlp_w1.astype(BF16), mlp_w2.astype(BF16)
    n_mix, n_mlp, n_fin = rows1(norm_mix), rows1(norm_mlp), final_norm.reshape(1, -1).astype(F32)

    outs = []
    for b in range(bsz):
        xb = x[b].astype(F32)
        u = _norm_call(xb, n_mix[0])
        v_first = None
        for l in range(depth):
            z_s5 = _mm_call(u, w_s5[l], tn=s5_w, name="in_proj_s5")
            z_rw = _mm_call(u, w_rw[l], tn=w_rw.shape[2] // 2, name="in_proj_rwkv")
            z_gla = _mm_call(u, w_gla[l], tn=w_gla.shape[2] // 3, name="in_proj_gla")
            gates = _mm_call(u, w_gate[l], tn=1024, bias=gate_b[l], name="in_proj_gate")

            y_a = _s5_call(z_s5, *[t_[l] for t_ in s5_tabs], s5_dd[l], s5_gw[l], s5_gb[l])

            w0, a0, k_k, k_a, r_k, lnx_w, lnx_b = [a[l] for a in rw_rows]
            vb, vbias = (rw_vb[l - 1], rw_vbias[l - 1]) if l > 0 else (None, None)
            res = _rwkv_call(z_rw, v_first, mu_rw[l], rw_wl[l], w0, rw_al[l], a0, rw_gl[l],
                             k_k, k_a, r_k, lnx_w, lnx_b, seg, seg_t, vb, vbias)
            if l == 0:
                y_b, v_first = res
            else:
                y_b = res

            y_c = _gla_call(z_gla, gla_al[l], gla_ab[l], gla_ng[l])

            wu = wu_bf[l]
            x_mid, h = _merge_call(y_a, y_b, y_c, gates, xb,
                                   wu[:s5_w], wu[s5_w:s5_w + rw_w], wu[s5_w + rw_w:],
                                   wo_bf[l], n_mlp[l])
            final = l == depth - 1
            res = _mlp_call(h, w1_bf[l], w2_bf[l], x_mid, n_fin if final else n_mix[l + 1], final)
            if final:
                xb = res[0]
            else:
                xb, u = res
        outs.append(xb.astype(x.dtype))
    return jnp.stack(outs, axis=0)
```

```python
import functools
import math

import jax
import jax.numpy as jnp
from jax import lax
from jax.experimental import pallas as pl
from jax.experimental.pallas import tpu as pltpu

F32 = jnp.float32
BF16 = jnp.bfloat16

LANE = 128
SUBLANE = 8
NORM_EPS = 1e-6

S5_GROUP = 16
S5_STATE = 64
S5_SLAB = 128
S5_SLAB_STATES = (S5_SLAB // S5_GROUP) * S5_STATE
S5_T = 256
S5_BLOCK_LEVELS = 3
S5_FOLD = 4

RWKV_HEAD = 64
RWKV_T = 64
RWKV_BLOCK = 256
RWKV_DECAY_LORA = 96
RWKV_AAA_LORA = 128
RWKV_GATE_LORA = 256
RWKV_MV_LORA = 64
RWKV_LNX_EPS = 64e-5

GLA_DK = 64
GLA_DV = 128
GLA_LORA = 16
GLA_TAU = 16.0
GLA_T = 64
GLA_BLOCK = 256

VMEM_LIMIT = 56 * 1024 * 1024


def _dot(a, b):
    return jnp.dot(a.astype(BF16), b.astype(BF16), preferred_element_type=F32)


def _dot_nt(a, b):
    return lax.dot_general(a.astype(BF16), b.astype(BF16), (((1,), (1,)), ((), ())),
                           preferred_element_type=F32)


def _dot_tn(a, b):
    return lax.dot_general(a.astype(BF16), b.astype(BF16), (((0,), (0,)), ((), ())),
                           preferred_element_type=F32)


def _split3(x):
    hi = x.astype(BF16)
    r1 = x - hi.astype(F32)
    mid = r1.astype(BF16)
    lo = (r1 - mid.astype(F32)).astype(BF16)
    return hi, mid, lo


def _dot_exact_lhs(m_bf16, x):
    hi, mid, lo = _split3(x)
    return (jnp.dot(m_bf16, hi, preferred_element_type=F32)
            + jnp.dot(m_bf16, mid, preferred_element_type=F32)
            + jnp.dot(m_bf16, lo, preferred_element_type=F32))


def _hi_lo(x):
    hi = x.astype(BF16)
    return hi, (x - hi.astype(F32)).astype(BF16)


def _seg_sum(x, seg, seg_t):
    s = jnp.dot(x.astype(BF16), seg, preferred_element_type=F32)
    hi, lo = _hi_lo(s)
    return jnp.dot(hi, seg_t, preferred_element_type=F32) + jnp.dot(lo, seg_t, preferred_element_type=F32)


def _split_weight(w):
    hi, lo = _hi_lo(w.astype(F32))
    return jnp.stack([hi, lo])


def _dot_hp(a, w_ref):
    a_hi, a_lo = _hi_lo(a)
    return (jnp.dot(a_hi, w_ref[0], preferred_element_type=F32)
            + jnp.dot(a_lo, w_ref[0], preferred_element_type=F32)
            + jnp.dot(a_hi, w_ref[1], preferred_element_type=F32))


def _sigmoid(x):
    return 1.0 / (1.0 + jnp.exp(-x))


def _softplus(x):
    return jnp.maximum(x, 0.0) + jnp.log(1.0 + jnp.exp(-jnp.abs(x)))


def _rms_rows(x, g):
    return x * lax.rsqrt(jnp.mean(x * x, axis=-1, keepdims=True) + NORM_EPS) * g


def _shift_rows(x, d):
    n = x.shape[0]
    if d % 8 == 0:
        return jnp.concatenate([jnp.zeros((d, x.shape[1]), x.dtype), x[:n - d]], axis=0)
    rolled = pltpu.roll(x, d, 0)
    row = lax.broadcasted_iota(jnp.int32, x.shape, 0)
    return jnp.where(row >= d, rolled, 0.0)


def _params(sem):
    return pltpu.CompilerParams(dimension_semantics=sem, vmem_limit_bytes=VMEM_LIMIT)


def _norm_kernel(x_ref, g_ref, o_ref):
    o_ref[...] = _rms_rows(x_ref[...], g_ref[...]).astype(o_ref.dtype)


def _norm_call(x, g, tm=512):
    m, d = x.shape
    return pl.pallas_call(
        _norm_kernel,
        out_shape=jax.ShapeDtypeStruct((m, d), BF16),
        grid=(m // tm,),
        in_specs=[pl.BlockSpec((tm, d), lambda i: (i, 0)),
                  pl.BlockSpec((1, d), lambda i: (0, 0))],
        out_specs=pl.BlockSpec((tm, d), lambda i: (i, 0)),
        compiler_params=_params(("parallel",)),
        name="rmsnorm",
    )(x, g)


def _mm_kernel(a_ref, w_ref, o_ref):
    o_ref[...] = jnp.dot(a_ref[...], w_ref[...], preferred_element_type=F32).astype(o_ref.dtype)


def _mm_gate_kernel(a_ref, w_ref, b_ref, o_ref):
    z = jnp.dot(a_ref[...], w_ref[...], preferred_element_type=F32) + b_ref[...]
    o_ref[...] = _sigmoid(z).astype(o_ref.dtype)


def _mm_call(a, w, tn, bias=None, tm=1024, name="in_proj"):
    m, k = a.shape
    n = w.shape[1]
    in_specs = [pl.BlockSpec((tm, k), lambda i, j: (i, 0)),
                pl.BlockSpec((k, tn), lambda i, j: (0, j))]
    args = [a, w]
    if bias is None:
        body, out_dtype = _mm_kernel, F32
    else:
        body, out_dtype = _mm_gate_kernel, BF16
        in_specs.append(pl.BlockSpec((1, tn), lambda i, j: (0, j)))
        args.append(bias)
    return pl.pallas_call(
        body,
        out_shape=jax.ShapeDtypeStruct((m, n), out_dtype),
        grid=(m // tm, n // tn),
        in_specs=in_specs,
        out_specs=pl.BlockSpec((tm, tn), lambda i, j: (i, j)),
        compiler_params=_params(("parallel", "parallel")),
        name=name,
    )(*args)


def _s5_kernel(z_ref, bre_ref, bim_ref, pre_ref, pim_ref, cre_ref, cim_ref, d_ref, gw_ref, gb_ref,
               o_ref, carry_re, carry_im):
    n_slab = bre_ref.shape[0]
    tiles_per_slab = S5_SLAB_STATES // LANE
    t = z_ref.shape[0]

    @pl.when(pl.program_id(0) == 0)
    def _():
        carry_re[...] = jnp.zeros_like(carry_re)
        carry_im[...] = jnp.zeros_like(carry_im)

    u = z_ref[...]
    ub = u.astype(BF16)
    n_blk = t // SUBLANE
    row_in_block = lax.broadcasted_iota(jnp.int32, u.shape, 0) & (SUBLANE - 1)
    shifted = [ub] + [jnp.where(row_in_block >= d, pltpu.roll(u, d, 0), 0.0).astype(BF16)
                      for d in range(1, S5_FOLD)]
    ys = []
    for j in range(n_slab):
        uj = jnp.concatenate([s_[:, j * S5_SLAB:(j + 1) * S5_SLAB] for s_ in shifted], axis=1)
        bu_re = jnp.dot(uj, bre_ref[j], preferred_element_type=F32)
        bu_im = jnp.dot(uj, bim_ref[j], preferred_element_type=F32)
        tiles_re = []
        tiles_im = []
        for c in range(tiles_per_slab):
            idx = j * tiles_per_slab + c
            sr = bu_re[:, c * LANE:(c + 1) * LANE].reshape(n_blk, SUBLANE, LANE)
            si = bu_im[:, c * LANE:(c + 1) * LANE].reshape(n_blk, SUBLANE, LANE)
            pr = pre_ref[idx]
            pi = pim_ref[idx]
            for k in range(S5_FOLD.bit_length() - 1, S5_BLOCK_LEVELS):
                qr = pltpu.roll(sr, 1 << k, 1)
                qi = pltpu.roll(si, 1 << k, 1)
                sr, si = sr + (pr[k] * qr - pi[k] * qi), si + (pr[k] * qi + pi[k] * qr)
            cr = jnp.broadcast_to(carry_re[idx], (SUBLANE, LANE))
            ci = jnp.broadcast_to(carry_im[idx], (SUBLANE, LANE))
            ar = pr[S5_BLOCK_LEVELS]
            ai = pi[S5_BLOCK_LEVELS]
            blocks_re = []
            blocks_im = []
            for b in range(n_blk):
                br = sr[b] + (ar * cr - ai * ci)
                bi = si[b] + (ar * ci + ai * cr)
                blocks_re.append(br)
                blocks_im.append(bi)
                cr = jnp.broadcast_to(br[SUBLANE - 1:SUBLANE], (SUBLANE, LANE))
                ci = jnp.broadcast_to(bi[SUBLANE - 1:SUBLANE], (SUBLANE, LANE))
            carry_re[idx] = cr[0:1]
            carry_im[idx] = ci[0:1]
            tiles_re.append(jnp.concatenate(blocks_re, axis=0).astype(BF16))
            tiles_im.append(jnp.concatenate(blocks_im, axis=0).astype(BF16))
        s_re = jnp.concatenate(tiles_re, axis=1)
        s_im = jnp.concatenate(tiles_im, axis=1)
        ys.append(jnp.dot(s_re, cre_ref[j], preferred_element_type=F32)
                  - jnp.dot(s_im, cim_ref[j], preferred_element_type=F32))
    y = jnp.concatenate(ys, axis=1) + d_ref[...] * u
    y = y * (0.5 * (1.0 + jnp.tanh(math.sqrt(2.0 / math.pi) * (y + 0.044715 * (y * y * y)))))
    y = y * _sigmoid(_dot(y, gw_ref[...]) + gb_ref[...])
    o_ref[...] = y.astype(o_ref.dtype)


def _s5_call(z, bre, bim, pre, pim, cre, cim, d, gw, gb):
    length, width = z.shape
    n_slab = bre.shape[0]
    n_tiles = n_slab * (S5_SLAB_STATES // LANE)
    full = lambda a: pl.BlockSpec(a.shape, lambda i: (0,) * a.ndim, pipeline_mode=pl.Buffered(1))
    return pl.pallas_call(
        _s5_kernel,
        out_shape=jax.ShapeDtypeStruct((length, width), BF16),
        grid=(length // S5_T,),
        in_specs=[pl.BlockSpec((S5_T, width), lambda i: (i, 0)),
                  full(bre), full(bim), full(pre), full(pim), full(cre), full(cim),
                  full(d), full(gw), full(gb)],
        out_specs=pl.BlockSpec((S5_T, width), lambda i: (i, 0)),
        scratch_shapes=[pltpu.VMEM((n_tiles, 1, LANE), F32),
                        pltpu.VMEM((n_tiles, 1, LANE), F32)],
        compiler_params=_params(("arbitrary",)),
        name="s5_mixer",
    )(z, bre, bim, pre, pim, cre, cim, d, gw, gb)


def _s5_tables(lam_re, lam_im, log_step, b_re, b_im, c_re, c_im):
    groups = lam_re.shape[0]
    n_slab = groups * S5_GROUP // S5_SLAB
    gps = S5_SLAB // S5_GROUP
    lr = jnp.minimum(lam_re.astype(F32), -1e-4)
    li = lam_im.astype(F32)
    dt = jnp.exp(log_step.astype(F32))[:, None]
    e = jnp.exp(lr * dt)
    lb_re = e * jnp.cos(li * dt)
    lb_im = e * jnp.sin(li * dt)
    den = lr * lr + li * li
    f_re = ((lb_re - 1.0) * lr + lb_im * li) / den
    f_im = (lb_im * lr - (lb_re - 1.0) * li) / den
    bb_re = f_re[..., None] * b_re - f_im[..., None] * b_im
    bb_im = f_re[..., None] * b_im + f_im[..., None] * b_re
    eye = jnp.eye(gps, dtype=F32)

    def bd_b(m):
        m = m.reshape(n_slab, gps, S5_STATE, S5_GROUP)
        return jnp.einsum('jgpi,gh->jgihp', m, eye).reshape(n_slab, S5_SLAB, S5_SLAB_STATES).astype(BF16)

    def bd_c(m):
        m = m.reshape(n_slab, gps, S5_GROUP, S5_STATE)
        return jnp.einsum('jgip,gh->jgphi', m, eye).reshape(n_slab, S5_SLAB_STATES, S5_SLAB).astype(BF16)

    r_idx = jnp.arange(SUBLANE)
    steps = 2 ** jnp.arange(S5_BLOCK_LEVELS)
    expo = jnp.concatenate([jnp.broadcast_to(steps[:, None], (S5_BLOCK_LEVELS, SUBLANE)),
                            (r_idx + 1)[None, :]], axis=0).astype(F32)
    keep = jnp.concatenate([r_idx[None, :] >= steps[:, None],
                            jnp.ones((1, SUBLANE), bool)], axis=0).astype(F32)
    ph = (li * dt).reshape(-1) * expo[..., None]
    mag = keep[..., None] * jnp.exp((lr * dt).reshape(-1) * expo[..., None])
    n_tiles = ph.shape[-1] // LANE
    tile_major = lambda a: a.reshape(a.shape[0], SUBLANE, n_tiles, LANE).transpose(2, 0, 1, 3)
    pw_re = tile_major(mag * jnp.cos(ph))
    pw_im = tile_major(mag * jnp.sin(ph))
    fold = jnp.arange(S5_FOLD, dtype=F32)[:, None, None]
    fd_mag = jnp.exp(lr * dt * fold)
    fd_re = (fd_mag * jnp.cos(li * dt * fold))[..., None]
    fd_im = (fd_mag * jnp.sin(li * dt * fold))[..., None]
    b_stack_re = jnp.concatenate([bd_b(fd_re[d] * bb_re - fd_im[d] * bb_im) for d in range(S5_FOLD)], axis=1)
    b_stack_im = jnp.concatenate([bd_b(fd_re[d] * bb_im + fd_im[d] * bb_re) for d in range(S5_FOLD)], axis=1)
    return (b_stack_re, b_stack_im, pw_re, pw_im,
            bd_c(c_re.astype(F32)), bd_c(c_im.astype(F32)))


def _rwkv_kernel(has_vres, *refs):
    if has_vres:
        (z_ref, vf_ref, mu_ref, wl_ref, w0_ref, al_ref, a0_ref, gl_ref, kk_ref, ka_ref, rk_ref,
         lw_ref, lb_ref, seg_ref, segt_ref, vb_ref, vbias_ref, o_ref, prev_ref, h_ref) = refs
    else:
        (z_ref, mu_ref, wl_ref, w0_ref, al_ref, a0_ref, gl_ref, kk_ref, ka_ref, rk_ref,
         lw_ref, lb_ref, seg_ref, segt_ref, o_ref, vo_ref, prev_ref, h_ref) = refs
    tb = z_ref.shape[0]
    t = RWKV_T
    n_chunk = tb // t
    width = o_ref.shape[1]
    n_pair = width // LANE

    @pl.when(pl.program_id(0) == 0)
    def _():
        prev_ref[...] = jnp.zeros_like(prev_ref)
        h_ref[...] = jnp.zeros_like(h_ref)

    z = z_ref[...]
    rowz = lax.broadcasted_iota(jnp.int32, z.shape, 0)
    prev = jnp.where(rowz == 0, prev_ref[...], pltpu.roll(z, 1, 0))
    prev_ref[...] = z[tb - 1:tb]
    zs = z + (prev - z) * mu_ref[...]

    o1 = 3 * width
    r = zs[:, 0:width]
    k = zs[:, width:2 * width]
    v = zs[:, 2 * width:o1]
    w_in = zs[:, o1:o1 + LANE]
    a_in = zs[:, o1 + LANE:o1 + 2 * LANE]
    g_in = zs[:, o1 + 2 * LANE:o1 + 2 * LANE + RWKV_GATE_LORA]

    wpre = w0_ref[...] + _dot_hp(jnp.tanh(w_in), wl_ref)
    logw = -jnp.exp(-_softplus(-wpre) - 0.5)
    a = _sigmoid(a0_ref[...] + _dot(a_in, al_ref[...]))
    g = _dot(_sigmoid(g_in), gl_ref[...])
    if has_vres:
        vr = zs[:, o1 + 2 * LANE + RWKV_GATE_LORA:]
        vg = _sigmoid(vbias_ref[...] + _dot(vr, vb_ref[...]))
        v = v + (vf_ref[...] - v) * vg
    else:
        vo_ref[...] = v

    seg = seg_ref[...]
    seg_t = segt_ref[...]
    kk = k * kk_ref[...]
    kk = kk * lax.rsqrt(jnp.maximum(_seg_sum(kk * kk, seg, seg_t), 1e-24))
    k2 = k * (1.0 + (a - 1.0) * ka_ref[...])
    kka = kk * a

    shift = t.bit_length() - 1
    ri = lax.broadcasted_iota(jnp.int32, (tb, tb), 0)
    ci = lax.broadcasted_iota(jnp.int32, (tb, tb), 1)
    tri = jnp.where((ci <= ri) & ((ri >> shift) == (ci >> shift)), 1.0, 0.0).astype(BF16)
    lc = _dot_exact_lhs(tri, logw)

    gi = lax.broadcasted_iota(jnp.int32, (4 * t, 4 * t), 0)
    gj = lax.broadcasted_iota(jnp.int32, (4 * t, 4 * t), 1)
    ti = gi & (t - 1)
    sj = gj & (t - 1)
    keep = sj + jnp.where(gi < 2 * t, 1, 0) <= ti
    lane = lax.broadcasted_iota(jnp.int32, (t, LANE), 1)
    head0 = lane < RWKV_HEAD

    def stack2(x):
        return jnp.concatenate([jnp.where(head0, x, 0.0), jnp.where(head0, 0.0, x)], axis=0)

    e_pos = jnp.exp(lc)
    e_neg = jnp.exp(-lc)
    e_prev = jnp.exp(lc - logw)
    rt_all = r * e_pos
    at_all = -kk * e_prev
    bt_all = kka * e_neg
    kt_all = k2 * e_neg

    items = [(c, p) for c in range(n_chunk) for p in range(n_pair)]
    tile = lambda x, c, p: x[c * t:(c + 1) * t, p * LANE:(p + 1) * LANE]
    at = [tile(at_all, c, p) for c, p in items]
    rt = [tile(rt_all, c, p) for c, p in items]
    vbd = [stack2(tile(v, c, p)) for c, p in items]
    gm = [jnp.where(keep, _dot_nt(jnp.concatenate([stack2(a_), stack2(r_)], axis=0),
                                  jnp.concatenate([stack2(tile(bt_all, c, p)), stack2(tile(kt_all, c, p))],
                                                  axis=0)), 0.0)
          for a_, r_, (c, p) in zip(at, rt, items)]
    n1 = [g_[0:2 * t, 0:2 * t] for g_ in gm]
    rhs0 = [_dot(g_[0:2 * t, 2 * t:4 * t], v_) for g_, v_ in zip(gm, vbd)]
    qi = lax.broadcasted_iota(jnp.int32, (2 * t, 2 * t), 0)
    qj = lax.broadcasted_iota(jnp.int32, (2 * t, 2 * t), 1)
    xinv = [jnp.where(qi == qj, 1.0, jnp.where((qi ^ qj) == 1, n_, 0.0)) for n_ in n1]
    for lvl in range(1, shift):
        couple = ((qi >> lvl) ^ (qj >> lvl)) == 1
        xm = [_dot(x_, jnp.where(couple, n_, 0.0)) for x_, n_ in zip(xinv, n1)]
        xinv = [x_ + _dot(m_, x_) for x_, m_ in zip(xinv, xm)]

    hts = [h_ref[p] for p in range(n_pair)]
    out_rows = []
    for c in range(n_chunk):
        idx = [c * n_pair + p for p in range(n_pair)]
        lc_c = lc[c * t:(c + 1) * t]
        e_rem = jnp.exp(lc_c[t - 1:t] - lc_c)
        bh_c = kka[c * t:(c + 1) * t] * e_rem
        kh_c = k2[c * t:(c + 1) * t] * e_rem
        e_last = e_pos[(c + 1) * t - 1:(c + 1) * t]
        ph = [_dot_nt(jnp.concatenate([at[i], rt[i]], axis=0), h_) for i, h_ in zip(idx, hts)]
        u = [_dot(xinv[i], stack2(p_[0:t]) + rhs0[i]) for i, p_ in zip(idx, ph)]
        uv = [jnp.concatenate([u_, vbd[i]], axis=0) for i, u_ in zip(idx, u)]
        opk = [stack2(p_[t:2 * t]) + _dot(gm[i][2 * t:4 * t, :], uv_) for i, p_, uv_ in zip(idx, ph, uv)]
        out_rows.append(jnp.concatenate([o_[0:t] + o_[t:2 * t] for o_ in opk], axis=1))
        hts = [h_ * e_last[:, p * LANE:(p + 1) * LANE]
               + _dot_tn(uv_, jnp.concatenate([stack2(bh_c[:, p * LANE:(p + 1) * LANE]),
                                               stack2(kh_c[:, p * LANE:(p + 1) * LANE])], axis=0))
               for p, (h_, uv_) in enumerate(zip(hts, uv))]
    for p in range(n_pair):
        h_ref[p] = hts[p]

    y = jnp.concatenate(out_rows, axis=0)
    inv_n = 1.0 / RWKV_HEAD
    mean = _seg_sum(y, seg, seg_t) * inv_n
    yc = y - mean
    var = _seg_sum(yc * yc, seg, seg_t) * inv_n
    yn = yc * lax.rsqrt(var + RWKV_LNX_EPS) * lw_ref[...] + lb_ref[...]
    bonus = _seg_sum(r * k2 * rk_ref[...], seg, seg_t) * v
    o_ref[...] = ((yn + bonus) * g).astype(o_ref.dtype)


def _rwkv_call(z, v_first, mu, wl, w0, al, a0, gl, k_k, k_a, r_k, lnx_w, lnx_b, seg, seg_t, vb, vbias):
    length, zc = z.shape
    width = w0.shape[1]
    has_vres = v_first is not None
    full = lambda a: pl.BlockSpec(a.shape, lambda i: (0,) * a.ndim)
    rows = lambda c: pl.BlockSpec((RWKV_BLOCK, c), lambda i: (i, 0))
    common = [mu, wl, w0, al, a0, gl, k_k, k_a, r_k, lnx_w, lnx_b, seg, seg_t]
    if has_vres:
        args = [z, v_first] + common + [vb, vbias]
        in_specs = [rows(zc), rows(width)] + [full(a) for a in common + [vb, vbias]]
        out_shape = jax.ShapeDtypeStruct((length, width), BF16)
        out_specs = rows(width)
    else:
        args = [z] + common
        in_specs = [rows(zc)] + [full(a) for a in common]
        out_shape = (jax.ShapeDtypeStruct((length, width), BF16),
                     jax.ShapeDtypeStruct((length, width), F32))
        out_specs = (rows(width), rows(width))
    return pl.pallas_call(
        functools.partial(_rwkv_kernel, has_vres),
        out_shape=out_shape,
        grid=(length // RWKV_BLOCK,),
        in_specs=in_specs,
        out_specs=out_specs,
        scratch_shapes=[pltpu.VMEM((1, zc), F32),
                        pltpu.VMEM((width // LANE, LANE, LANE), F32)],
        compiler_params=_params(("arbitrary",)),
        name="rwkv7_mixer",
    )(*args)


def _gla_kernel(z_ref, al_ref, ab_ref, ng_ref, o_ref, st_ref):
    tb = z_ref.shape[0]
    t = GLA_T
    n_chunk = tb // t
    width = o_ref.shape[1]
    n_head = width // GLA_DV

    @pl.when(pl.program_id(0) == 0)
    def _():
        st_ref[...] = jnp.zeros_like(st_ref)

    z = z_ref[...]
    q = z[:, 0:width] * (GLA_DK ** -0.5)
    k = z[:, width:2 * width]
    v = z[:, 2 * width:3 * width]
    g = z[:, 3 * width:4 * width]
    a_in = z[:, 4 * width:]
    x = _dot_hp(a_in, al_ref) + ab_ref[...]
    log_a = -_softplus(-x) * (1.0 / GLA_TAU)

    shift = t.bit_length() - 1
    ri = lax.broadcasted_iota(jnp.int32, (tb, tb), 0)
    ci = lax.broadcasted_iota(jnp.int32, (tb, tb), 1)
    tri = jnp.where((ci <= ri) & ((ri >> shift) == (ci >> shift)), 1.0, 0.0).astype(BF16)
    b = _dot_exact_lhs(tri, log_a)
    causal = (lax.broadcasted_iota(jnp.int32, (t, t), 1) <= lax.broadcasted_iota(jnp.int32, (t, t), 0))

    q_in = q * jnp.exp(b)
    heads = range(n_head)
    head = lambda x, h: x[:, h * GLA_DV:(h + 1) * GLA_DV]
    chunk = lambda x, c: x[c * t:(c + 1) * t]
    v_c = [chunk(v, c) for c in range(n_chunk)]
    k_rem = []
    e_last = []
    intra = []
    for c in range(n_chunk):
        b_c = chunk(b, c)
        b_mid = b_c[t // 2:t // 2 + 1]
        b_last = b_c[t - 1:t]
        q_mid = chunk(q, c) * jnp.exp(b_c - b_mid)
        k_mid = chunk(k, c) * jnp.exp(b_mid - b_c)
        k_rem.append(chunk(k, c) * jnp.exp(b_last - b_c))
        e_last.append(jnp.exp(b_last))
        intra.append([_dot(jnp.where(causal, _dot_nt(head(q_mid, h), head(k_mid, h)), 0.0), head(v_c[c], h))
                      for h in heads])
    sts = [st_ref[h] for h in heads]
    out_rows = []
    for c in range(n_chunk):
        q_c = chunk(q_in, c)
        o = [i_ + _dot_nt(head(q_c, h), s_) for h, (i_, s_) in enumerate(zip(intra[c], sts))]
        sts = [s_ * head(e_last[c], h) + _dot_tn(head(v_c[c], h), head(k_rem[c], h))
               for h, s_ in enumerate(sts)]
        out_rows.append(jnp.concatenate(
            [o_ * lax.rsqrt(jnp.mean(o_ * o_, axis=-1, keepdims=True) + NORM_EPS) for o_ in o], axis=1))
    for h in heads:
        st_ref[h] = sts[h]
    o = jnp.concatenate(out_rows, axis=0)
    o_ref[...] = (o * ng_ref[...] * (g * _sigmoid(g))).astype(o_ref.dtype)


def _gla_call(z, al, ab, ng):
    length, zc = z.shape
    width = ng.shape[1]
    full = lambda a: pl.BlockSpec(a.shape, lambda i: (0,) * a.ndim)
    return pl.pallas_call(
        _gla_kernel,
        out_shape=jax.ShapeDtypeStruct((length, width), BF16),
        grid=(length // GLA_BLOCK,),
        in_specs=[pl.BlockSpec((GLA_BLOCK, zc), lambda i: (i, 0)), full(al), full(ab), full(ng)],
        out_specs=pl.BlockSpec((GLA_BLOCK, width), lambda i: (i, 0)),
        scratch_shapes=[pltpu.VMEM((width // GLA_DV, GLA_DV, LANE), F32)],
        compiler_params=_params(("arbitrary",)),
        name="gla_mixer",
    )(z, al, ab, ng)


def _merge_kernel(ya_ref, yb_ref, yc_ref, gate_ref, x_ref, wua_ref, wub_ref, wuc_ref, wo_ref, g_ref,
                  xo_ref, h_ref):
    d = x_ref.shape[1]
    gates = gate_ref[...].astype(F32)
    merged = (gates[:, 0:d] * jnp.dot(ya_ref[...], wua_ref[...], preferred_element_type=F32)
              + gates[:, d:2 * d] * jnp.dot(yb_ref[...], wub_ref[...], preferred_element_type=F32)
              + gates[:, 2 * d:3 * d] * jnp.dot(yc_ref[...], wuc_ref[...], preferred_element_type=F32))
    x = x_ref[...] + _dot(merged, wo_ref[...])
    xo_ref[...] = x
    h_ref[...] = _rms_rows(x, g_ref[...]).astype(h_ref.dtype)


def _merge_call(ya, yb, yc, gates, x, wua, wub, wuc, wo, g, tm=256):
    m, d = x.shape
    rows = lambda a: pl.BlockSpec((tm, a.shape[1]), lambda i: (i, 0))
    const = lambda a: pl.BlockSpec(a.shape, lambda i: (0,) * a.ndim, pipeline_mode=pl.Buffered(1))
    return pl.pallas_call(
        _merge_kernel,
        out_shape=(jax.ShapeDtypeStruct((m, d), F32), jax.ShapeDtypeStruct((m, d), BF16)),
        grid=(m // tm,),
        in_specs=[rows(ya), rows(yb), rows(yc), rows(gates), rows(x),
                  const(wua), const(wub), const(wuc), const(wo), const(g)],
        out_specs=(pl.BlockSpec((tm, d), lambda i: (i, 0)), pl.BlockSpec((tm, d), lambda i: (i, 0))),
        compiler_params=_params(("parallel",)),
        name="merge_out_proj",
    )(ya, yb, yc, gates, x, wua, wub, wuc, wo, g)


def _mlp_kernel(final, h_ref, w1_ref, w2_ref, x_ref, g_ref, xo_ref, *rest):
    f = pl.program_id(1)

    @pl.when(f == 0)
    def _():
        xo_ref[...] = x_ref[...]

    hid = jnp.maximum(jnp.dot(h_ref[...], w1_ref[...], preferred_element_type=F32), 0.0)
    xo_ref[...] += _dot(hid * hid, w2_ref[...])

    @pl.when(f == pl.num_programs(1) - 1)
    def _():
        y = _rms_rows(xo_ref[...], g_ref[...])
        if final:
            xo_ref[...] = y
        else:
            rest[0][...] = y.astype(rest[0].dtype)


def _mlp_call(h, w1, w2, x, g_next, final, tm=512, tf=1024):
    m, d = x.shape
    ff = w1.shape[1]
    row_blk = pl.BlockSpec((tm, d), lambda i, f: (i, 0))
    out_shape = [jax.ShapeDtypeStruct((m, d), F32)]
    out_specs = [row_blk]
    if not final:
        out_shape.append(jax.ShapeDtypeStruct((m, d), BF16))
        out_specs.append(row_blk)
    return pl.pallas_call(
        functools.partial(_mlp_kernel, final),
        out_shape=tuple(out_shape),
        grid=(m // tm, ff // tf),
        in_specs=[row_blk,
                  pl.BlockSpec((d, tf), lambda i, f: (0, f)),
                  pl.BlockSpec((tf, d), lambda i, f: (f, 0)),
                  row_blk,
                  pl.BlockSpec((1, d), lambda i, f: (0, 0))],
        out_specs=tuple(out_specs),
        compiler_params=_params(("parallel", "arbitrary")),
        name="mlp",
    )(h, w1, w2, x, g_next)


def _pad_cols(a, n):
    return jnp.pad(a, [(0, 0)] * (a.ndim - 1) + [(0, n - a.shape[-1])])


def _pad_rows(a, n):
    return jnp.pad(a, [(0, 0)] * (a.ndim - 2) + [(0, n - a.shape[-2]), (0, 0)])


def _head_pad(a, heads, dh, to):
    lead = a.shape[:-1]
    a = a.reshape(lead + (heads, dh))
    a = jnp.pad(a, [(0, 0)] * len(lead) + [(0, 0), (0, to - dh)])
    return a.reshape(lead + (heads * to,))


def kernel(x, norm_mix, w_in, gate_bias, s5_lambda_re, s5_lambda_im, s5_log_step, s5_b_re, s5_b_im, s5_c_re, s5_c_im, s5_d, s5_glu_w, s5_glu_b, rwkv_mu, rwkv_w_lora, rwkv_w0, rwkv_a_lora, rwkv_a0, rwkv_g_lora, rwkv_k_k, rwkv_k_a, rwkv_r_k, rwkv_lnx_w, rwkv_lnx_b, rwkv_vres_a, rwkv_vres_mu, rwkv_vres_b, rwkv_vres_bias, gla_alpha_lora, gla_alpha_bias, gla_norm_g, w_up, w_out, norm_mlp, mlp_w1, mlp_w2, final_norm):
    bsz, length, d = x.shape
    depth = w_in.shape[0]
    s5_w = s5_d.shape[1]
    rw_w = rwkv_w0.shape[1]
    gla_v = gla_norm_g.shape[1]
    gla_heads = gla_v // GLA_DV
    gla_k = gla_heads * GLA_DK
    rw_cols = 3 * rw_w + RWKV_DECAY_LORA + RWKV_AAA_LORA + RWKV_GATE_LORA
    gla_cols = 2 * gla_k + 2 * gla_v + GLA_LORA
    o_rw = s5_w
    o_gla = o_rw + rw_cols
    o_gate = o_gla + gla_cols
    o1 = 3 * rw_w

    head_id = jnp.arange(rw_w) // RWKV_HEAD
    seg = (head_id[:, None] == jnp.arange(LANE)[None, :]).astype(BF16)
    seg_t = seg.T
    rows1 = lambda a: a.reshape(a.shape[0], 1, -1).astype(F32)

    w_in_bf = w_in.astype(BF16)
    w_s5 = w_in_bf[:, :, :o_rw]
    wr = w_in_bf[:, :, o_rw:o_gla]
    o2 = o1 + RWKV_DECAY_LORA
    vres_a = jnp.concatenate([jnp.zeros((1,) + rwkv_vres_a.shape[1:], BF16), rwkv_vres_a.astype(BF16)], axis=0)
    vres_mu = jnp.concatenate([jnp.zeros((1,) + rwkv_vres_mu.shape[1:], F32), rwkv_vres_mu.astype(F32)], axis=0)
    w_rw = jnp.concatenate([wr[:, :, :o1], _pad_cols(wr[:, :, o1:o2], LANE), wr[:, :, o2:],
                            _pad_cols(vres_a, LANE)], axis=2)
    mu_rw = rows1(jnp.concatenate([rwkv_mu[:, :o1], _pad_cols(rwkv_mu[:, o1:o2], LANE), rwkv_mu[:, o2:],
                                   _pad_cols(vres_mu, LANE)], axis=1))
    wg = w_in_bf[:, :, o_gla:o_gate]
    w_gla = jnp.concatenate([_head_pad(wg[:, :, :gla_k], gla_heads, GLA_DK, GLA_DV),
                             _head_pad(wg[:, :, gla_k:2 * gla_k], gla_heads, GLA_DK, GLA_DV),
                             wg[:, :, 2 * gla_k:2 * gla_k + 2 * gla_v],
                             _pad_cols(wg[:, :, 2 * gla_k + 2 * gla_v:], LANE)], axis=2)
    w_gate = w_in_bf[:, :, o_gate:]
    gate_b = rows1(gate_bias)
    s5_tabs = jax.vmap(_s5_tables)(s5_lambda_re, s5_lambda_im, s5_log_step, s5_b_re.astype(F32),
                                   s5_b_im.astype(F32), s5_c_re, s5_c_im)
    s5_dd, s5_gw, s5_gb = rows1(s5_d), s5_glu_w.astype(BF16), rows1(s5_glu_b)
    rw_wl = jax.vmap(_split_weight)(_pad_rows(rwkv_w_lora, LANE))
    rw_vb = _pad_rows(rwkv_vres_b, LANE).astype(BF16)
    rw_rows = [rows1(a) for a in (rwkv_w0, rwkv_a0, rwkv_k_k, rwkv_k_a, rwkv_r_k, rwkv_lnx_w, rwkv_lnx_b)]
    rw_al, rw_gl, rw_vbias = rwkv_a_lora.astype(BF16), rwkv_g_lora.astype(BF16), rows1(rwkv_vres_bias)
    gla_al = jax.vmap(_split_weight)(_pad_rows(_head_pad(gla_alpha_lora, gla_heads, GLA_DK, GLA_DV), LANE))
    gla_ab = rows1(_head_pad(gla_alpha_bias, gla_heads, GLA_DK, GLA_DV))
    gla_ng = rows1(gla_norm_g)
    wu_bf, wo_bf = w_up.astype(BF16), w_out.astype(BF16)
    w1_bf, w2_bf = mlp_w1.astype(BF16), mlp_w2.astype(BF16)
    n_mix, n_mlp, n_fin = rows1(norm_mix), rows1(norm_mlp), final_norm.reshape(1, -1).astype(F32)

    outs = []
    for b in range(bsz):
        xb = x[b].astype(F32)
        u = _norm_call(xb, n_mix[0])
        v_first = None
        for l in range(depth):
            z_s5 = _mm_call(u, w_s5[l], tn=s5_w, name="in_proj_s5")
            z_rw = _mm_call(u, w_rw[l], tn=w_rw.shape[2] // 2, name="in_proj_rwkv")
            z_gla = _mm_call(u, w_gla[l], tn=w_gla.shape[2] // 3, name="in_proj_gla")
            gates = _mm_call(u, w_gate[l], tn=1024, bias=gate_b[l], name="in_proj_gate")

            y_a = _s5_call(z_s5, *[t_[l] for t_ in s5_tabs], s5_dd[l], s5_gw[l], s5_gb[l])

            w0, a0, k_k, k_a, r_k, lnx_w, lnx_b = [a[l] for a in rw_rows]
            vb, vbias = (rw_vb[l - 1], rw_vbias[l - 1]) if l > 0 else (None, None)
            res = _rwkv_call(z_rw, v_first, mu_rw[l], rw_wl[l], w0, rw_al[l], a0, rw_gl[l],
                             k_k, k_a, r_k, lnx_w, lnx_b, seg, seg_t, vb, vbias)
            if l == 0:
                y_b, v_first = res
            else:
                y_b = res

            y_c = _gla_call(z_gla, gla_al[l], gla_ab[l], gla_ng[l])

            wu = wu_bf[l]
            x_mid, h = _merge_call(y_a, y_b, y_c, gates, xb,
                                   wu[:s5_w], wu[s5_w:s5_w + rw_w], wu[s5_w + rw_w:],
                                   wo_bf[l], n_mlp[l])
            final = l == depth - 1
            res = _mlp_call(h, w1_bf[l], w2_bf[l], x_mid, n_fin if final else n_mix[l + 1], final)
            if final:
                xb = res[0]
            else:
                xb, u = res
        outs.append(xb.astype(x.dtype))
    return jnp.stack(outs, axis=0)
```

```python
import functools
import math

import jax
import jax.numpy as jnp
from jax import lax
from jax.experimental import pallas as pl
from jax.experimental.pallas import tpu as pltpu

F32 = jnp.float32
BF16 = jnp.bfloat16

LANE = 128
SUBLANE = 8
NORM_EPS = 1e-6

S5_GROUP = 16
S5_STATE = 64
S5_SLAB = 128
S5_SLAB_STATES = (S5_SLAB // S5_GROUP) * S5_STATE
S5_T = 256
S5_BLOCK_LEVELS = 3
S5_FOLD = 4

RWKV_HEAD = 64
RWKV_T = 64
RWKV_BLOCK = 256
RWKV_DECAY_LORA = 96
RWKV_AAA_LORA = 128
RWKV_GATE_LORA = 256
RWKV_MV_LORA = 64
RWKV_LNX_EPS = 64e-5

GLA_DK = 64
GLA_DV = 128
GLA_LORA = 16
GLA_TAU = 16.0
GLA_T = 64
GLA_BLOCK = 256

VMEM_LIMIT = 56 * 1024 * 1024


def _dot(a, b):
    return jnp.dot(a.astype(BF16), b.astype(BF16), preferred_element_type=F32)


def _dot_nt(a, b):
    return lax.dot_general(a.astype(BF16), b.astype(BF16), (((1,), (1,)), ((), ())),
                           preferred_element_type=F32)


def _dot_tn(a, b):
    return lax.dot_general(a.astype(BF16), b.astype(BF16), (((0,), (0,)), ((), ())),
                           preferred_element_type=F32)


def _split3(x):
    hi = x.astype(BF16)
    r1 = x - hi.astype(F32)
    mid = r1.astype(BF16)
    lo = (r1 - mid.astype(F32)).astype(BF16)
    return hi, mid, lo


def _dot_exact_lhs(m_bf16, x):
    hi, mid, lo = _split3(x)
    return (jnp.dot(m_bf16, hi, preferred_element_type=F32)
            + jnp.dot(m_bf16, mid, preferred_element_type=F32)
            + jnp.dot(m_bf16, lo, preferred_element_type=F32))


def _hi_lo(x):
    hi = x.astype(BF16)
    return hi, (x - hi.astype(F32)).astype(BF16)


def _seg_sum(x, seg, seg_t):
    s = jnp.dot(x.astype(BF16), seg, preferred_element_type=F32)
    hi, lo = _hi_lo(s)
    return jnp.dot(hi, seg_t, preferred_element_type=F32) + jnp.dot(lo, seg_t, preferred_element_type=F32)


def _split_weight(w):
    hi, lo = _hi_lo(w.astype(F32))
    return jnp.stack([hi, lo])


def _dot_hp(a, w_ref):
    a_hi, a_lo = _hi_lo(a)
    return (jnp.dot(a_hi, w_ref[0], preferred_element_type=F32)
            + jnp.dot(a_lo, w_ref[0], preferred_element_type=F32)
            + jnp.dot(a_hi, w_ref[1], preferred_element_type=F32))


def _sigmoid(x):
    return 1.0 / (1.0 + jnp.exp(-x))


def _softplus(x):
    return jnp.maximum(x, 0.0) + jnp.log(1.0 + jnp.exp(-jnp.abs(x)))


def _rms_rows(x, g):
    return x * lax.rsqrt(jnp.mean(x * x, axis=-1, keepdims=True) + NORM_EPS) * g


def _params(sem):
    return pltpu.CompilerParams(dimension_semantics=sem, vmem_limit_bytes=VMEM_LIMIT)


def _norm_kernel(x_ref, g_ref, o_ref):
    o_ref[...] = _rms_rows(x_ref[...], g_ref[...]).astype(o_ref.dtype)


def _norm_call(x, g, tm=512):
    m, d = x.shape
    return pl.pallas_call(
        _norm_kernel,
        out_shape=jax.ShapeDtypeStruct((m, d), BF16),
        grid=(m // tm,),
        in_specs=[pl.BlockSpec((tm, d), lambda i: (i, 0)),
                  pl.BlockSpec((1, d), lambda i: (0, 0))],
        out_specs=pl.BlockSpec((tm, d), lambda i: (i, 0)),
        compiler_params=_params(("parallel",)),
        name="rmsnorm",
    )(x, g)


def _mm_kernel(a_ref, w_ref, o_ref):
    o_ref[...] = jnp.dot(a_ref[...], w_ref[...], preferred_element_type=F32).astype(o_ref.dtype)


def _mm_gate_kernel(a_ref, w_ref, b_ref, o_ref):
    z = jnp.dot(a_ref[...], w_ref[...], preferred_element_type=F32) + b_ref[...]
    o_ref[...] = _sigmoid(z).astype(o_ref.dtype)


def _mm_call(a, w, tn, bias=None, tm=1024, name="in_proj"):
    m, k = a.shape
    n = w.shape[1]
    in_specs = [pl.BlockSpec((tm, k), lambda i, j: (i, 0)),
                pl.BlockSpec((k, tn), lambda i, j: (0, j))]
    args = [a, w]
    if bias is None:
        body, out_dtype = _mm_kernel, F32
    else:
        body, out_dtype = _mm_gate_kernel, BF16
        in_specs.append(pl.BlockSpec((1, tn), lambda i, j: (0, j)))
        args.append(bias)
    return pl.pallas_call(
        body,
        out_shape=jax.ShapeDtypeStruct((m, n), out_dtype),
        grid=(m // tm, n // tn),
        in_specs=in_specs,
        out_specs=pl.BlockSpec((tm, tn), lambda i, j: (i, j)),
        compiler_params=_params(("parallel", "parallel")),
        name=name,
    )(*args)


def _s5_kernel(z_ref, bre_ref, bim_ref, pre_ref, pim_ref, cre_ref, cim_ref, d_ref, gw_ref, gb_ref,
               o_ref, carry_re, carry_im):
    n_slab = bre_ref.shape[0]
    tiles_per_slab = S5_SLAB_STATES // LANE
    t = z_ref.shape[0]

    @pl.when(pl.program_id(0) == 0)
    def _():
        carry_re[...] = jnp.zeros_like(carry_re)
        carry_im[...] = jnp.zeros_like(carry_im)

    u = z_ref[...]
    ub = u.astype(BF16)
    n_blk = t // SUBLANE
    row_in_block = lax.broadcasted_iota(jnp.int32, u.shape, 0) & (SUBLANE - 1)
    shifted = [ub] + [jnp.where(row_in_block >= d, pltpu.roll(u, d, 0), 0.0).astype(BF16)
                      for d in range(1, S5_FOLD)]
    ys = []
    for j in range(n_slab):
        uj = jnp.concatenate([s_[:, j * S5_SLAB:(j + 1) * S5_SLAB] for s_ in shifted], axis=1)
        bu_re = jnp.dot(uj, bre_ref[j], preferred_element_type=F32)
        bu_im = jnp.dot(uj, bim_ref[j], preferred_element_type=F32)
        tiles_re = []
        tiles_im = []
        for c in range(tiles_per_slab):
            idx = j * tiles_per_slab + c
            sr = bu_re[:, c * LANE:(c + 1) * LANE].reshape(n_blk, SUBLANE, LANE)
            si = bu_im[:, c * LANE:(c + 1) * LANE].reshape(n_blk, SUBLANE, LANE)
            pr = pre_ref[idx]
            pi = pim_ref[idx]
            for k in range(S5_FOLD.bit_length() - 1, S5_BLOCK_LEVELS):
                qr = pltpu.roll(sr, 1 << k, 1)
                qi = pltpu.roll(si, 1 << k, 1)
                sr, si = sr + (pr[k] * qr - pi[k] * qi), si + (pr[k] * qi + pi[k] * qr)
            cr = jnp.broadcast_to(carry_re[idx], (SUBLANE, LANE))
            ci = jnp.broadcast_to(carry_im[idx], (SUBLANE, LANE))
            ar = pr[S5_BLOCK_LEVELS]
            ai = pi[S5_BLOCK_LEVELS]
            blocks_re = []
            blocks_im = []
            for b in range(n_blk):
                br = sr[b] + (ar * cr - ai * ci)
                bi = si[b] + (ar * ci + ai * cr)
                blocks_re.append(br)
                blocks_im.append(bi)
                cr = jnp.broadcast_to(br[SUBLANE - 1:SUBLANE], (SUBLANE, LANE))
                ci = jnp.broadcast_to(bi[SUBLANE - 1:SUBLANE], (SUBLANE, LANE))
            carry_re[idx] = cr[0:1]
            carry_im[idx] = ci[0:1]
            tiles_re.append(jnp.concatenate(blocks_re, axis=0).astype(BF16))
            tiles_im.append(jnp.concatenate(blocks_im, axis=0).astype(BF16))
        s_re = jnp.concatenate(tiles_re, axis=1)
        s_im = jnp.concatenate(tiles_im, axis=1)
        ys.append(jnp.dot(s_re, cre_ref[j], preferred_element_type=F32)
                  - jnp.dot(s_im, cim_ref[j], preferred_element_type=F32))
    y = jnp.concatenate(ys, axis=1) + d_ref[...] * u
    y = y * (0.5 * (1.0 + jnp.tanh(math.sqrt(2.0 / math.pi) * (y + 0.044715 * (y * y * y)))))
    y = y * _sigmoid(_dot(y, gw_ref[...]) + gb_ref[...])
    o_ref[...] = y.astype(o_ref.dtype)


def _s5_call(z, bre, bim, pre, pim, cre, cim, d, gw, gb):
    length, width = z.shape
    n_slab = bre.shape[0]
    n_tiles = n_slab * (S5_SLAB_STATES // LANE)
    full = lambda a: pl.BlockSpec(a.shape, lambda i: (0,) * a.ndim, pipeline_mode=pl.Buffered(1))
    return pl.pallas_call(
        _s5_kernel,
        out_shape=jax.ShapeDtypeStruct((length, width), BF16),
        grid=(length // S5_T,),
        in_specs=[pl.BlockSpec((S5_T, width), lambda i: (i, 0)),
                  full(bre), full(bim), full(pre), full(pim), full(cre), full(cim),
                  full(d), full(gw), full(gb)],
        out_specs=pl.BlockSpec((S5_T, width), lambda i: (i, 0)),
        scratch_shapes=[pltpu.VMEM((n_tiles, 1, LANE), F32),
                        pltpu.VMEM((n_tiles, 1, LANE), F32)],
        compiler_params=_params(("arbitrary",)),
        name="s5_mixer",
    )(z, bre, bim, pre, pim, cre, cim, d, gw, gb)


def _s5_tables(lam_re, lam_im, log_step, b_re, b_im, c_re, c_im):
    groups = lam_re.shape[0]
    n_slab = groups * S5_GROUP // S5_SLAB
    gps = S5_SLAB // S5_GROUP
    lr = jnp.minimum(lam_re.astype(F32), -1e-4)
    li = lam_im.astype(F32)
    dt = jnp.exp(log_step.astype(F32))[:, None]
    e = jnp.exp(lr * dt)
    lb_re = e * jnp.cos(li * dt)
    lb_im = e * jnp.sin(li * dt)
    den = lr * lr + li * li
    f_re = ((lb_re - 1.0) * lr + lb_im * li) / den
    f_im = (lb_im * lr - (lb_re - 1.0) * li) / den
    bb_re = f_re[..., None] * b_re - f_im[..., None] * b_im
    bb_im = f_re[..., None] * b_im + f_im[..., None] * b_re
    eye = jnp.eye(gps, dtype=F32)

    def bd_b(m):
        m = m.reshape(n_slab, gps, S5_STATE, S5_GROUP)
        return jnp.einsum('jgpi,gh->jgihp', m, eye).reshape(n_slab, S5_SLAB, S5_SLAB_STATES).astype(BF16)

    def bd_c(m):
        m = m.reshape(n_slab, gps, S5_GROUP, S5_STATE)
        return jnp.einsum('jgip,gh->jgphi', m, eye).reshape(n_slab, S5_SLAB_STATES, S5_SLAB).astype(BF16)

    r_idx = jnp.arange(SUBLANE)
    steps = 2 ** jnp.arange(S5_BLOCK_LEVELS)
    expo = jnp.concatenate([jnp.broadcast_to(steps[:, None], (S5_BLOCK_LEVELS, SUBLANE)),
                            (r_idx + 1)[None, :]], axis=0).astype(F32)
    keep = jnp.concatenate([r_idx[None, :] >= steps[:, None],
                            jnp.ones((1, SUBLANE), bool)], axis=0).astype(F32)
    ph = (li * dt).reshape(-1) * expo[..., None]
    mag = keep[..., None] * jnp.exp((lr * dt).reshape(-1) * expo[..., None])
    n_tiles = ph.shape[-1] // LANE
    tile_major = lambda a: a.reshape(a.shape[0], SUBLANE, n_tiles, LANE).transpose(2, 0, 1, 3)
    pw_re = tile_major(mag * jnp.cos(ph))
    pw_im = tile_major(mag * jnp.sin(ph))
    fold = jnp.arange(S5_FOLD, dtype=F32)[:, None, None]
    fd_mag = jnp.exp(lr * dt * fold)
    fd_re = (fd_mag * jnp.cos(li * dt * fold))[..., None]
    fd_im = (fd_mag * jnp.sin(li * dt * fold))[..., None]
    b_stack_re = jnp.concatenate([bd_b(fd_re[d] * bb_re - fd_im[d] * bb_im) for d in range(S5_FOLD)], axis=1)
    b_stack_im = jnp.concatenate([bd_b(fd_re[d] * bb_im + fd_im[d] * bb_re) for d in range(S5_FOLD)], axis=1)
    return (b_stack_re, b_stack_im, pw_re, pw_im,
            bd_c(c_re.astype(F32)), bd_c(c_im.astype(F32)))


def _rwkv_kernel(has_vres, *refs):
    if has_vres:
        (z_ref, vf_ref, mu_ref, wl_ref, w0_ref, al_ref, a0_ref, gl_ref, kk_ref, ka_ref, rk_ref,
         lw_ref, lb_ref, seg_ref, segt_ref, vb_ref, vbias_ref, o_ref, prev_ref, h_ref) = refs
    else:
        (z_ref, mu_ref, wl_ref, w0_ref, al_ref, a0_ref, gl_ref, kk_ref, ka_ref, rk_ref,
         lw_ref, lb_ref, seg_ref, segt_ref, o_ref, vo_ref, prev_ref, h_ref) = refs
    tb = z_ref.shape[0]
    t = RWKV_T
    n_chunk = tb // t
    width = o_ref.shape[1]
    n_pair = width // LANE

    @pl.when(pl.program_id(0) == 0)
    def _():
        prev_ref[...] = jnp.zeros_like(prev_ref)
        h_ref[...] = jnp.zeros_like(h_ref)

    z = z_ref[...]
    rowz = lax.broadcasted_iota(jnp.int32, z.shape, 0)
    prev = jnp.where(rowz == 0, prev_ref[...], pltpu.roll(z, 1, 0))
    prev_ref[...] = z[tb - 1:tb]
    zs = z + (prev - z) * mu_ref[...]

    o1 = 3 * width
    r = zs[:, 0:width]
    k = zs[:, width:2 * width]
    v = zs[:, 2 * width:o1]
    w_in = zs[:, o1:o1 + LANE]
    a_in = zs[:, o1 + LANE:o1 + 2 * LANE]
    g_in = zs[:, o1 + 2 * LANE:o1 + 2 * LANE + RWKV_GATE_LORA]

    wpre = w0_ref[...] + _dot_hp(jnp.tanh(w_in), wl_ref)
    logw = -jnp.exp(-_softplus(-wpre) - 0.5)
    a = _sigmoid(a0_ref[...] + _dot(a_in, al_ref[...]))
    g = _dot(_sigmoid(g_in), gl_ref[...])
    if has_vres:
        vr = zs[:, o1 + 2 * LANE + RWKV_GATE_LORA:]
        vg = _sigmoid(vbias_ref[...] + _dot(vr, vb_ref[...]))
        v = v + (vf_ref[...] - v) * vg
    else:
        vo_ref[...] = v

    seg = seg_ref[...]
    seg_t = segt_ref[...]
    kk = k * kk_ref[...]
    kk = kk * lax.rsqrt(jnp.maximum(_seg_sum(kk * kk, seg, seg_t), 1e-24))
    k2 = k * (1.0 + (a - 1.0) * ka_ref[...])
    kka = kk * a

    shift = t.bit_length() - 1
    ri = lax.broadcasted_iota(jnp.int32, (tb, tb), 0)
    ci = lax.broadcasted_iota(jnp.int32, (tb, tb), 1)
    tri = jnp.where((ci <= ri) & ((ri >> shift) == (ci >> shift)), 1.0, 0.0).astype(BF16)
    lc = _dot_exact_lhs(tri, logw)

    gi = lax.broadcasted_iota(jnp.int32, (4 * t, 4 * t), 0)
    gj = lax.broadcasted_iota(jnp.int32, (4 * t, 4 * t), 1)
    ti = gi & (t - 1)
    sj = gj & (t - 1)
    keep = sj + jnp.where(gi < 2 * t, 1, 0) <= ti
    lane = lax.broadcasted_iota(jnp.int32, (t, LANE), 1)
    head0 = lane < RWKV_HEAD

    def stack2(x):
        return jnp.concatenate([jnp.where(head0, x, 0.0), jnp.where(head0, 0.0, x)], axis=0)

    e_pos = jnp.exp(lc)
    e_neg = jnp.exp(-lc)
    e_prev = jnp.exp(lc - logw)
    rt_all = r * e_pos
    at_all = -kk * e_prev
    bt_all = kka * e_neg
    kt_all = k2 * e_neg

    items = [(c, p) for c in range(n_chunk) for p in range(n_pair)]
    tile = lambda x, c, p: x[c * t:(c + 1) * t, p * LANE:(p + 1) * LANE]
    at = [tile(at_all, c, p) for c, p in items]
    rt = [tile(rt_all, c, p) for c, p in items]
    vbd = [stack2(tile(v, c, p)) for c, p in items]
    gm = [jnp.where(keep, _dot_nt(jnp.concatenate([stack2(a_), stack2(r_)], axis=0),
                                  jnp.concatenate([stack2(tile(bt_all, c, p)), stack2(tile(kt_all, c, p))],
                                                  axis=0)), 0.0)
          for a_, r_, (c, p) in zip(at, rt, items)]
    n1 = [g_[0:2 * t, 0:2 * t] for g_ in gm]
    rhs0 = [_dot(g_[0:2 * t, 2 * t:4 * t], v_) for g_, v_ in zip(gm, vbd)]
    qi = lax.broadcasted_iota(jnp.int32, (2 * t, 2 * t), 0)
    qj = lax.broadcasted_iota(jnp.int32, (2 * t, 2 * t), 1)
    xinv = [jnp.where(qi == qj, 1.0, jnp.where((qi ^ qj) == 1, n_, 0.0)) for n_ in n1]
    for lvl in range(1, shift):
        couple = ((qi >> lvl) ^ (qj >> lvl)) == 1
        xm = [_dot(x_, jnp.where(couple, n_, 0.0)) for x_, n_ in zip(xinv, n1)]
        xinv = [x_ + _dot(m_, x_) for x_, m_ in zip(xinv, xm)]

    hts = [h_ref[p] for p in range(n_pair)]
    out_rows = []
    for c in range(n_chunk):
        idx = [c * n_pair + p for p in range(n_pair)]
        lc_c = lc[c * t:(c + 1) * t]
        e_rem = jnp.exp(lc_c[t - 1:t] - lc_c)
        bh_c = kka[c * t:(c + 1) * t] * e_rem
        kh_c = k2[c * t:(c + 1) * t] * e_rem
        e_last = e_pos[(c + 1) * t - 1:(c + 1) * t]
        ph = [_dot_nt(jnp.concatenate([at[i], rt[i]], axis=0), h_) for i, h_ in zip(idx, hts)]
        u = [_dot(xinv[i], stack2(p_[0:t]) + rhs0[i]) for i, p_ in zip(idx, ph)]
        uv = [jnp.concatenate([u_, vbd[i]], axis=0) for i, u_ in zip(idx, u)]
        opk = [stack2(p_[t:2 * t]) + _dot(gm[i][2 * t:4 * t, :], uv_) for i, p_, uv_ in zip(idx, ph, uv)]
        out_rows.append(jnp.concatenate([o_[0:t] + o_[t:2 * t] for o_ in opk], axis=1))
        hts = [h_ * e_last[:, p * LANE:(p + 1) * LANE]
               + _dot_tn(uv_, jnp.concatenate([stack2(bh_c[:, p * LANE:(p + 1) * LANE]),
                                               stack2(kh_c[:, p * LANE:(p + 1) * LANE])], axis=0))
               for p, (h_, uv_) in enumerate(zip(hts, uv))]
    for p in range(n_pair):
        h_ref[p] = hts[p]

    y = jnp.concatenate(out_rows, axis=0)
    inv_n = 1.0 / RWKV_HEAD
    mean = _seg_sum(y, seg, seg_t) * inv_n
    yc = y - mean
    var = _seg_sum(yc * yc, seg, seg_t) * inv_n
    yn = yc * lax.rsqrt(var + RWKV_LNX_EPS) * lw_ref[...] + lb_ref[...]
    bonus = _seg_sum(r * k2 * rk_ref[...], seg, seg_t) * v
    o_ref[...] = ((yn + bonus) * g).astype(o_ref.dtype)


def _rwkv_call(z, v_first, mu, wl, w0, al, a0, gl, k_k, k_a, r_k, lnx_w, lnx_b, seg, seg_t, vb, vbias):
    length, zc = z.shape
    width = w0.shape[1]
    has_vres = v_first is not None
    full = lambda a: pl.BlockSpec(a.shape, lambda i: (0,) * a.ndim)
    rows = lambda c: pl.BlockSpec((RWKV_BLOCK, c), lambda i: (i, 0))
    common = [mu, wl, w0, al, a0, gl, k_k, k_a, r_k, lnx_w, lnx_b, seg, seg_t]
    if has_vres:
        args = [z, v_first] + common + [vb, vbias]
        in_specs = [rows(zc), rows(width)] + [full(a) for a in common + [vb, vbias]]
        out_shape = jax.ShapeDtypeStruct((length, width), BF16)
        out_specs = rows(width)
    else:
        args = [z] + common
        in_specs = [rows(zc)] + [full(a) for a in common]
        out_shape = (jax.ShapeDtypeStruct((length, width), BF16),
                     jax.ShapeDtypeStruct((length, width), F32))
        out_specs = (rows(width), rows(width))
    return pl.pallas_call(
        functools.partial(_rwkv_kernel, has_vres),
        out_shape=out_shape,
        grid=(length // RWKV_BLOCK,),
        in_specs=in_specs,
        out_specs=out_specs,
        scratch_shapes=[pltpu.VMEM((1, zc), F32),
                        pltpu.VMEM((width // LANE, LANE, LANE), F32)],
        compiler_params=_params(("arbitrary",)),
        name="rwkv7_mixer",
    )(*args)


def _gla_kernel(z_ref, al_ref, ab_ref, ng_ref, o_ref, st_ref):
    tb = z_ref.shape[0]
    t = GLA_T
    n_chunk = tb // t
    width = o_ref.shape[1]
    n_head = width // GLA_DV

    @pl.when(pl.program_id(0) == 0)
    def _():
        st_ref[...] = jnp.zeros_like(st_ref)

    z = z_ref[...]
    kw = n_head * GLA_DK
    q = z[:, 0:kw] * (GLA_DK ** -0.5)
    k = z[:, kw:2 * kw]
    v = z[:, 2 * kw:2 * kw + width]
    g = z[:, 2 * kw + width:2 * kw + 2 * width]
    a_in = z[:, 2 * kw + 2 * width:]
    x = _dot_hp(a_in, al_ref) + ab_ref[...]
    log_a = -_softplus(-x) * (1.0 / GLA_TAU)

    shift = t.bit_length() - 1
    ri = lax.broadcasted_iota(jnp.int32, (tb, tb), 0)
    ci = lax.broadcasted_iota(jnp.int32, (tb, tb), 1)
    tri = jnp.where((ci <= ri) & ((ri >> shift) == (ci >> shift)), 1.0, 0.0).astype(BF16)
    b = _dot_exact_lhs(tri, log_a)
    causal = (lax.broadcasted_iota(jnp.int32, (t, t), 1) <= lax.broadcasted_iota(jnp.int32, (t, t), 0))

    q_in = q * jnp.exp(b)
    heads = range(n_head)
    head = lambda x, h: x[:, h * GLA_DV:(h + 1) * GLA_DV]
    head_k = lambda x, h: x[:, h * GLA_DK:(h + 1) * GLA_DK]
    chunk = lambda x, c: x[c * t:(c + 1) * t]
    v_c = [chunk(v, c) for c in range(n_chunk)]
    k_rem = []
    e_last = []
    intra = []
    for c in range(n_chunk):
        b_c = chunk(b, c)
        b_mid = b_c[t // 2:t // 2 + 1]
        b_last = b_c[t - 1:t]
        q_mid = chunk(q, c) * jnp.exp(b_c - b_mid)
        k_mid = chunk(k, c) * jnp.exp(b_mid - b_c)
        k_rem.append(chunk(k, c) * jnp.exp(b_last - b_c))
        e_last.append(jnp.exp(b_last))
        intra.append([_dot(jnp.where(causal, _dot_nt(head_k(q_mid, h), head_k(k_mid, h)), 0.0),
                           head(v_c[c], h)) for h in heads])
    sts = [st_ref[h] for h in heads]
    out_rows = []
    for c in range(n_chunk):
        q_c = chunk(q_in, c)
        o = [i_ + _dot_nt(head_k(q_c, h), s_) for h, (i_, s_) in enumerate(zip(intra[c], sts))]
        sts = [s_ * head_k(e_last[c], h) + _dot_tn(head(v_c[c], h), head_k(k_rem[c], h))
               for h, s_ in enumerate(sts)]
        out_rows.append(jnp.concatenate(
            [o_ * lax.rsqrt(jnp.mean(o_ * o_, axis=-1, keepdims=True) + NORM_EPS) for o_ in o], axis=1))
    for h in heads:
        st_ref[h] = sts[h]
    o = jnp.concatenate(out_rows, axis=0)
    o_ref[...] = (o * ng_ref[...] * (g * _sigmoid(g))).astype(o_ref.dtype)


def _gla_call(z, al, ab, ng):
    length, zc = z.shape
    width = ng.shape[1]
    full = lambda a: pl.BlockSpec(a.shape, lambda i: (0,) * a.ndim)
    return pl.pallas_call(
        _gla_kernel,
        out_shape=jax.ShapeDtypeStruct((length, width), BF16),
        grid=(length // GLA_BLOCK,),
        in_specs=[pl.BlockSpec((GLA_BLOCK, zc), lambda i: (i, 0)), full(al), full(ab), full(ng)],
        out_specs=pl.BlockSpec((GLA_BLOCK, width), lambda i: (i, 0)),
        scratch_shapes=[pltpu.VMEM((width // GLA_DV, GLA_DV, GLA_DK), F32)],
        compiler_params=_params(("arbitrary",)),
        name="gla_mixer",
    )(z, al, ab, ng)


def _merge_kernel(ya_ref, yb_ref, yc_ref, gate_ref, x_ref, wua_ref, wub_ref, wuc_ref, wo_ref, g_ref,
                  xo_ref, h_ref):
    d = x_ref.shape[1]
    gates = gate_ref[...].astype(F32)
    merged = (gates[:, 0:d] * jnp.dot(ya_ref[...], wua_ref[...], preferred_element_type=F32)
              + gates[:, d:2 * d] * jnp.dot(yb_ref[...], wub_ref[...], preferred_element_type=F32)
              + gates[:, 2 * d:3 * d] * jnp.dot(yc_ref[...], wuc_ref[...], preferred_element_type=F32))
    x = x_ref[...] + _dot(merged, wo_ref[...])
    xo_ref[...] = x
    h_ref[...] = _rms_rows(x, g_ref[...]).astype(h_ref.dtype)


def _merge_call(ya, yb, yc, gates, x, wua, wub, wuc, wo, g, tm=256):
    m, d = x.shape
    rows = lambda a: pl.BlockSpec((tm, a.shape[1]), lambda i: (i, 0))
    const = lambda a: pl.BlockSpec(a.shape, lambda i: (0,) * a.ndim, pipeline_mode=pl.Buffered(1))
    return pl.pallas_call(
        _merge_kernel,
        out_shape=(jax.ShapeDtypeStruct((m, d), F32), jax.ShapeDtypeStruct((m, d), BF16)),
        grid=(m // tm,),
        in_specs=[rows(ya), rows(yb), rows(yc), rows(gates), rows(x),
                  const(wua), const(wub), const(wuc), const(wo), const(g)],
        out_specs=(pl.BlockSpec((tm, d), lambda i: (i, 0)), pl.BlockSpec((tm, d), lambda i: (i, 0))),
        compiler_params=_params(("parallel",)),
        name="merge_out_proj",
    )(ya, yb, yc, gates, x, wua, wub, wuc, wo, g)


def _mlp_kernel(final, h_ref, w1_ref, w2_ref, x_ref, g_ref, xo_ref, *rest):
    f = pl.program_id(1)

    @pl.when(f == 0)
    def _():
        xo_ref[...] = x_ref[...]

    hid = jnp.maximum(jnp.dot(h_ref[...], w1_ref[...], preferred_element_type=F32), 0.0)
    xo_ref[...] += _dot(hid * hid, w2_ref[...])

    @pl.when(f == pl.num_programs(1) - 1)
    def _():
        y = _rms_rows(xo_ref[...], g_ref[...])
        if final:
            xo_ref[...] = y
        else:
            rest[0][...] = y.astype(rest[0].dtype)


def _mlp_call(h, w1, w2, x, g_next, final, tm=512, tf=1024):
    m, d = x.shape
    ff = w1.shape[1]
    row_blk = pl.BlockSpec((tm, d), lambda i, f: (i, 0))
    out_shape = [jax.ShapeDtypeStruct((m, d), F32)]
    out_specs = [row_blk]
    if not final:
        out_shape.append(jax.ShapeDtypeStruct((m, d), BF16))
        out_specs.append(row_blk)
    return pl.pallas_call(
        functools.partial(_mlp_kernel, final),
        out_shape=tuple(out_shape),
        grid=(m // tm, ff // tf),
        in_specs=[row_blk,
                  pl.BlockSpec((d, tf), lambda i, f: (0, f)),
                  pl.BlockSpec((tf, d), lambda i, f: (f, 0)),
                  row_blk,
                  pl.BlockSpec((1, d), lambda i, f: (0, 0))],
        out_specs=tuple(out_specs),
        compiler_params=_params(("parallel", "arbitrary")),
        name="mlp",
    )(h, w1, w2, x, g_next)


def _pad_cols(a, n):
    return jnp.pad(a, [(0, 0)] * (a.ndim - 1) + [(0, n - a.shape[-1])])


def _pad_rows(a, n):
    return jnp.pad(a, [(0, 0)] * (a.ndim - 2) + [(0, n - a.shape[-2]), (0, 0)])


def kernel(x, norm_mix, w_in, gate_bias, s5_lambda_re, s5_lambda_im, s5_log_step, s5_b_re, s5_b_im, s5_c_re, s5_c_im, s5_d, s5_glu_w, s5_glu_b, rwkv_mu, rwkv_w_lora, rwkv_w0, rwkv_a_lora, rwkv_a0, rwkv_g_lora, rwkv_k_k, rwkv_k_a, rwkv_r_k, rwkv_lnx_w, rwkv_lnx_b, rwkv_vres_a, rwkv_vres_mu, rwkv_vres_b, rwkv_vres_bias, gla_alpha_lora, gla_alpha_bias, gla_norm_g, w_up, w_out, norm_mlp, mlp_w1, mlp_w2, final_norm):
    bsz, length, d = x.shape
    depth = w_in.shape[0]
    s5_w = s5_d.shape[1]
    rw_w = rwkv_w0.shape[1]
    gla_v = gla_norm_g.shape[1]
    gla_heads = gla_v // GLA_DV
    gla_k = gla_heads * GLA_DK
    rw_cols = 3 * rw_w + RWKV_DECAY_LORA + RWKV_AAA_LORA + RWKV_GATE_LORA
    gla_cols = 2 * gla_k + 2 * gla_v + GLA_LORA
    o_rw = s5_w
    o_gla = o_rw + rw_cols
    o_gate = o_gla + gla_cols
    o1 = 3 * rw_w

    head_id = jnp.arange(rw_w) // RWKV_HEAD
    seg = (head_id[:, None] == jnp.arange(LANE)[None, :]).astype(BF16)
    seg_t = seg.T
    rows1 = lambda a: a.reshape(a.shape[0], 1, -1).astype(F32)

    w_in_bf = w_in.astype(BF16)
    w_s5 = w_in_bf[:, :, :o_rw]
    wr = w_in_bf[:, :, o_rw:o_gla]
    o2 = o1 + RWKV_DECAY_LORA
    vres_a = jnp.concatenate([jnp.zeros((1,) + rwkv_vres_a.shape[1:], BF16), rwkv_vres_a.astype(BF16)], axis=0)
    vres_mu = jnp.concatenate([jnp.zeros((1,) + rwkv_vres_mu.shape[1:], F32), rwkv_vres_mu.astype(F32)], axis=0)
    w_rw = jnp.concatenate([wr[:, :, :o1], _pad_cols(wr[:, :, o1:o2], LANE), wr[:, :, o2:],
                            _pad_cols(vres_a, LANE)], axis=2)
    mu_rw = rows1(jnp.concatenate([rwkv_mu[:, :o1], _pad_cols(rwkv_mu[:, o1:o2], LANE), rwkv_mu[:, o2:],
                                   _pad_cols(vres_mu, LANE)], axis=1))
    wg = w_in_bf[:, :, o_gla:o_gate]
    w_gla = jnp.concatenate([wg[:, :, :2 * gla_k + 2 * gla_v],
                             _pad_cols(wg[:, :, 2 * gla_k + 2 * gla_v:], LANE)], axis=2)
    w_gate = w_in_bf[:, :, o_gate:]
    gate_b = rows1(gate_bias)
    s5_tabs = jax.vmap(_s5_tables)(s5_lambda_re, s5_lambda_im, s5_log_step, s5_b_re.astype(F32),
                                   s5_b_im.astype(F32), s5_c_re, s5_c_im)
    s5_dd, s5_gw, s5_gb = rows1(s5_d), s5_glu_w.astype(BF16), rows1(s5_glu_b)
    rw_wl = jax.vmap(_split_weight)(_pad_rows(rwkv_w_lora, LANE))
    rw_vb = _pad_rows(rwkv_vres_b, LANE).astype(BF16)
    rw_rows = [rows1(a) for a in (rwkv_w0, rwkv_a0, rwkv_k_k, rwkv_k_a, rwkv_r_k, rwkv_lnx_w, rwkv_lnx_b)]
    rw_al, rw_gl, rw_vbias = rwkv_a_lora.astype(BF16), rwkv_g_lora.astype(BF16), rows1(rwkv_vres_bias)
    gla_al = jax.vmap(_split_weight)(_pad_rows(gla_alpha_lora, LANE))
    gla_ab = rows1(gla_alpha_bias)
    gla_ng = rows1(gla_norm_g)
    wu_bf, wo_bf = w_up.astype(BF16), w_out.astype(BF16)
    w1_bf, w2_bf = mlp_w1.astype(BF16), mlp_w2.astype(BF16)
    n_mix, n_mlp, n_fin = rows1(norm_mix), rows1(norm_mlp), final_norm.reshape(1, -1).astype(F32)

    outs = []
    for b in range(bsz):
        xb = x[b].astype(F32)
        u = _norm_call(xb, n_mix[0])
        v_first = None
        for l in range(depth):
            z_s5 = _mm_call(u, w_s5[l], tn=s5_w, name="in_proj_s5")
            z_rw = _mm_call(u, w_rw[l], tn=w_rw.shape[2] // 2, name="in_proj_rwkv")
            z_gla = _mm_call(u, w_gla[l], tn=w_gla.shape[2] // 2, name="in_proj_gla")
            gates = _mm_call(u, w_gate[l], tn=1024, bias=gate_b[l], name="in_proj_gate")

            y_a = _s5_call(z_s5, *[t_[l] for t_ in s5_tabs], s5_dd[l], s5_gw[l], s5_gb[l])

            w0, a0, k_k, k_a, r_k, lnx_w, lnx_b = [a[l] for a in rw_rows]
            vb, vbias = (rw_vb[l - 1], rw_vbias[l - 1]) if l > 0 else (None, None)
            res = _rwkv_call(z_rw, v_first, mu_rw[l], rw_wl[l], w0, rw_al[l], a0, rw_gl[l],
                             k_k, k_a, r_k, lnx_w, lnx_b, seg, seg_t, vb, vbias)
            if l == 0:
                y_b, v_first = res
            else:
                y_b = res

            y_c = _gla_call(z_gla, gla_al[l], gla_ab[l], gla_ng[l])

            wu = wu_bf[l]
            x_mid, h = _merge_call(y_a, y_b, y_c, gates, xb,
                                   wu[:s5_w], wu[s5_w:s5_w + rw_w], wu[s5_w + rw_w:],
                                   wo_bf[l], n_mlp[l])
            final = l == depth - 1
            res = _mlp_call(h, w1_bf[l], w2_bf[l], x_mid, n_fin if final else n_mix[l + 1], final)
            if final:
                xb = res[0]
            else:
                xb, u = res
        outs.append(xb.astype(x.dtype))
    return jnp.stack(outs, axis=0)
```

```python
import functools
import math

import jax
import jax.numpy as jnp
from jax import lax
from jax.experimental import pallas as pl
from jax.experimental.pallas import tpu as pltpu

F32 = jnp.float32
BF16 = jnp.bfloat16

LANE = 128
SUBLANE = 8
NORM_EPS = 1e-6

S5_GROUP = 16
S5_STATE = 64
S5_SLAB = 128
S5_SLAB_STATES = (S5_SLAB // S5_GROUP) * S5_STATE
S5_T = 256
S5_BLOCK_LEVELS = 3
S5_FOLD = 4

RWKV_HEAD = 64
RWKV_T = 64
RWKV_BLOCK = 256
RWKV_DECAY_LORA = 96
RWKV_AAA_LORA = 128
RWKV_GATE_LORA = 256
RWKV_LNX_EPS = 64e-5

GLA_DK = 64
GLA_DV = 128
GLA_LORA = 16
GLA_TAU = 16.0
GLA_T = 64
GLA_BLOCK = 256

VMEM_LIMIT = 56 * 1024 * 1024


def _dot(a, b):
    return jnp.dot(a.astype(BF16), b.astype(BF16), preferred_element_type=F32)


def _dot_nt(a, b):
    return lax.dot_general(a.astype(BF16), b.astype(BF16), (((1,), (1,)), ((), ())),
                           preferred_element_type=F32)


def _dot_tn(a, b):
    return lax.dot_general(a.astype(BF16), b.astype(BF16), (((0,), (0,)), ((), ())),
                           preferred_element_type=F32)


def _split3(x):
    hi = x.astype(BF16)
    r1 = x - hi.astype(F32)
    mid = r1.astype(BF16)
    lo = (r1 - mid.astype(F32)).astype(BF16)
    return hi, mid, lo


def _dot_exact_lhs(m_bf16, x):
    hi, mid, lo = _split3(x)
    return (jnp.dot(m_bf16, hi, preferred_element_type=F32)
            + jnp.dot(m_bf16, mid, preferred_element_type=F32)
            + jnp.dot(m_bf16, lo, preferred_element_type=F32))


def _hi_lo(x):
    hi = x.astype(BF16)
    return hi, (x - hi.astype(F32)).astype(BF16)


def _seg_sum(x, seg, seg_t):
    s = jnp.dot(x.astype(BF16), seg, preferred_element_type=F32)
    hi, lo = _hi_lo(s)
    return jnp.dot(hi, seg_t, preferred_element_type=F32) + jnp.dot(lo, seg_t, preferred_element_type=F32)


def _split_weight(w):
    hi, lo = _hi_lo(w.astype(F32))
    return jnp.stack([hi, lo])


def _dot_hp(a, w_ref):
    a_hi, a_lo = _hi_lo(a)
    return (jnp.dot(a_hi, w_ref[0], preferred_element_type=F32)
            + jnp.dot(a_lo, w_ref[0], preferred_element_type=F32)
            + jnp.dot(a_hi, w_ref[1], preferred_element_type=F32))


def _sigmoid(x):
    return 1.0 / (1.0 + jnp.exp(-x))


def _softplus(x):
    return jnp.maximum(x, 0.0) + jnp.log(1.0 + jnp.exp(-jnp.abs(x)))


def _rms_rows(x, g):
    return x * lax.rsqrt(jnp.mean(x * x, axis=-1, keepdims=True) + NORM_EPS) * g


def _params(sem):
    return pltpu.CompilerParams(dimension_semantics=sem, vmem_limit_bytes=VMEM_LIMIT)


def _norm_kernel(x_ref, g_ref, o_ref):
    o_ref[...] = _rms_rows(x_ref[...], g_ref[...]).astype(o_ref.dtype)


def _norm_call(x, g, tm=512):
    m, d = x.shape
    return pl.pallas_call(
        _norm_kernel,
        out_shape=jax.ShapeDtypeStruct((m, d), BF16),
        grid=(m // tm,),
        in_specs=[pl.BlockSpec((tm, d), lambda i: (i, 0)),
                  pl.BlockSpec((1, d), lambda i: (0, 0))],
        out_specs=pl.BlockSpec((tm, d), lambda i: (i, 0)),
        compiler_params=_params(("parallel",)),
        name="rmsnorm",
    )(x, g)


def _mm_kernel(a_ref, w_ref, o_ref):
    o_ref[...] = jnp.dot(a_ref[...], w_ref[...], preferred_element_type=F32).astype(o_ref.dtype)


def _mm_gate_kernel(a_ref, w_ref, b_ref, o_ref):
    z = jnp.dot(a_ref[...], w_ref[...], preferred_element_type=F32) + b_ref[...]
    o_ref[...] = _sigmoid(z).astype(o_ref.dtype)


def _mm_call(a, w, tn, bias=None, tm=1024, name="in_proj"):
    m, k = a.shape
    n = w.shape[1]
    in_specs = [pl.BlockSpec((tm, k), lambda i, j: (i, 0)),
                pl.BlockSpec((k, tn), lambda i, j: (0, j))]
    args = [a, w]
    if bias is None:
        body, out_dtype = _mm_kernel, F32
    else:
        body, out_dtype = _mm_gate_kernel, BF16
        in_specs.append(pl.BlockSpec((1, tn), lambda i, j: (0, j)))
        args.append(bias)
    return pl.pallas_call(
        body,
        out_shape=jax.ShapeDtypeStruct((m, n), out_dtype),
        grid=(m // tm, n // tn),
        in_specs=in_specs,
        out_specs=pl.BlockSpec((tm, tn), lambda i, j: (i, j)),
        compiler_params=_params(("parallel", "parallel")),
        name=name,
    )(*args)


def _s5_kernel(z_ref, bre_ref, bim_ref, pre_ref, pim_ref, cre_ref, cim_ref, d_ref, gw_ref, gb_ref,
               o_ref, carry_re, carry_im):
    n_slab = bre_ref.shape[0]
    tiles_per_slab = S5_SLAB_STATES // LANE
    t = z_ref.shape[0]

    @pl.when(pl.program_id(0) == 0)
    def _():
        carry_re[...] = jnp.zeros_like(carry_re)
        carry_im[...] = jnp.zeros_like(carry_im)

    u = z_ref[...]
    ub = u.astype(BF16)
    n_blk = t // SUBLANE
    row_in_block = lax.broadcasted_iota(jnp.int32, u.shape, 0) & (SUBLANE - 1)
    shifted = [ub] + [jnp.where(row_in_block >= d, pltpu.roll(u, d, 0), 0.0).astype(BF16)
                      for d in range(1, S5_FOLD)]
    ys = []
    for j in range(n_slab):
        uj = jnp.concatenate([s_[:, j * S5_SLAB:(j + 1) * S5_SLAB] for s_ in shifted], axis=1)
        bu_re = jnp.dot(uj, bre_ref[j], preferred_element_type=F32)
        bu_im = jnp.dot(uj, bim_ref[j], preferred_element_type=F32)
        tiles_re = []
        tiles_im = []
        for c in range(tiles_per_slab):
            idx = j * tiles_per_slab + c
            sr = bu_re[:, c * LANE:(c + 1) * LANE].reshape(n_blk, SUBLANE, LANE)
            si = bu_im[:, c * LANE:(c + 1) * LANE].reshape(n_blk, SUBLANE, LANE)
            pr = pre_ref[idx]
            pi = pim_ref[idx]
            for k in range(S5_FOLD.bit_length() - 1, S5_BLOCK_LEVELS):
                qr = pltpu.roll(sr, 1 << k, 1)
                qi = pltpu.roll(si, 1 << k, 1)
                sr, si = sr + (pr[k] * qr - pi[k] * qi), si + (pr[k] * qi + pi[k] * qr)
            cr = jnp.broadcast_to(carry_re[idx], (SUBLANE, LANE))
            ci = jnp.broadcast_to(carry_im[idx], (SUBLANE, LANE))
            ar = pr[S5_BLOCK_LEVELS]
            ai = pi[S5_BLOCK_LEVELS]
            blocks_re = []
            blocks_im = []
            for b in range(n_blk):
                br = sr[b] + (ar * cr - ai * ci)
                bi = si[b] + (ar * ci + ai * cr)
                blocks_re.append(br)
                blocks_im.append(bi)
                cr = jnp.broadcast_to(br[SUBLANE - 1:SUBLANE], (SUBLANE, LANE))
                ci = jnp.broadcast_to(bi[SUBLANE - 1:SUBLANE], (SUBLANE, LANE))
            carry_re[idx] = cr[0:1]
            carry_im[idx] = ci[0:1]
            tiles_re.append(jnp.concatenate(blocks_re, axis=0).astype(BF16))
            tiles_im.append(jnp.concatenate(blocks_im, axis=0).astype(BF16))
        s_re = jnp.concatenate(tiles_re, axis=1)
        s_im = jnp.concatenate(tiles_im, axis=1)
        ys.append(jnp.dot(s_re, cre_ref[j], preferred_element_type=F32)
                  - jnp.dot(s_im, cim_ref[j], preferred_element_type=F32))
    y = jnp.concatenate(ys, axis=1) + d_ref[...] * u
    y = y * (0.5 * (1.0 + jnp.tanh(math.sqrt(2.0 / math.pi) * (y + 0.044715 * (y * y * y)))))
    y = y * _sigmoid(_dot(y, gw_ref[...]) + gb_ref[...])
    o_ref[...] = y.astype(o_ref.dtype)


def _s5_call(z, bre, bim, pre, pim, cre, cim, d, gw, gb):
    length, width = z.shape
    n_slab = bre.shape[0]
    n_tiles = n_slab * (S5_SLAB_STATES // LANE)
    full = lambda a: pl.BlockSpec(a.shape, lambda i: (0,) * a.ndim, pipeline_mode=pl.Buffered(1))
    return pl.pallas_call(
        _s5_kernel,
        out_shape=jax.ShapeDtypeStruct((length, width), BF16),
        grid=(length // S5_T,),
        in_specs=[pl.BlockSpec((S5_T, width), lambda i: (i, 0)),
                  full(bre), full(bim), full(pre), full(pim), full(cre), full(cim),
                  full(d), full(gw), full(gb)],
        out_specs=pl.BlockSpec((S5_T, width), lambda i: (i, 0)),
        scratch_shapes=[pltpu.VMEM((n_tiles, 1, LANE), F32),
                        pltpu.VMEM((n_tiles, 1, LANE), F32)],
        compiler_params=_params(("arbitrary",)),
        name="s5_mixer",
    )(z, bre, bim, pre, pim, cre, cim, d, gw, gb)


def _s5_tables(lam_re, lam_im, log_step, b_re, b_im, c_re, c_im):
    groups = lam_re.shape[0]
    n_slab = groups * S5_GROUP // S5_SLAB
    gps = S5_SLAB // S5_GROUP
    lr = jnp.minimum(lam_re.astype(F32), -1e-4)
    li = lam_im.astype(F32)
    dt = jnp.exp(log_step.astype(F32))[:, None]
    e = jnp.exp(lr * dt)
    lb_re = e * jnp.cos(li * dt)
    lb_im = e * jnp.sin(li * dt)
    den = lr * lr + li * li
    f_re = ((lb_re - 1.0) * lr + lb_im * li) / den
    f_im = (lb_im * lr - (lb_re - 1.0) * li) / den
    bb_re = f_re[..., None] * b_re - f_im[..., None] * b_im
    bb_im = f_re[..., None] * b_im + f_im[..., None] * b_re
    eye = jnp.eye(gps, dtype=F32)

    def bd_b(m):
        m = m.reshape(n_slab, gps, S5_STATE, S5_GROUP)
        return jnp.einsum('jgpi,gh->jgihp', m, eye).reshape(n_slab, S5_SLAB, S5_SLAB_STATES).astype(BF16)

    def bd_c(m):
        m = m.reshape(n_slab, gps, S5_GROUP, S5_STATE)
        return jnp.einsum('jgip,gh->jgphi', m, eye).reshape(n_slab, S5_SLAB_STATES, S5_SLAB).astype(BF16)

    r_idx = jnp.arange(SUBLANE)
    steps = 2 ** jnp.arange(S5_BLOCK_LEVELS)
    expo = jnp.concatenate([jnp.broadcast_to(steps[:, None], (S5_BLOCK_LEVELS, SUBLANE)),
                            (r_idx + 1)[None, :]], axis=0).astype(F32)
    keep = jnp.concatenate([r_idx[None, :] >= steps[:, None],
                            jnp.ones((1, SUBLANE), bool)], axis=0).astype(F32)
    ph = (li * dt).reshape(-1) * expo[..., None]
    mag = keep[..., None] * jnp.exp((lr * dt).reshape(-1) * expo[..., None])
    n_tiles = ph.shape[-1] // LANE
    tile_major = lambda a: a.reshape(a.shape[0], SUBLANE, n_tiles, LANE).transpose(2, 0, 1, 3)
    pw_re = tile_major(mag * jnp.cos(ph))
    pw_im = tile_major(mag * jnp.sin(ph))
    fold = jnp.arange(S5_FOLD, dtype=F32)[:, None, None]
    fd_mag = jnp.exp(lr * dt * fold)
    fd_re = (fd_mag * jnp.cos(li * dt * fold))[..., None]
    fd_im = (fd_mag * jnp.sin(li * dt * fold))[..., None]
    b_stack_re = jnp.concatenate([bd_b(fd_re[d] * bb_re - fd_im[d] * bb_im) for d in range(S5_FOLD)], axis=1)
    b_stack_im = jnp.concatenate([bd_b(fd_re[d] * bb_im + fd_im[d] * bb_re) for d in range(S5_FOLD)], axis=1)
    return (b_stack_re, b_stack_im, pw_re, pw_im,
            bd_c(c_re.astype(F32)), bd_c(c_im.astype(F32)))


def _rwkv_kernel(has_vres, *refs):
    if has_vres:
        (z_ref, vf_ref, mu_ref, wl_ref, w0_ref, al_ref, a0_ref, gl_ref, kk_ref, ka_ref, rk_ref,
         lw_ref, lb_ref, seg_ref, segt_ref, vb_ref, vbias_ref, o_ref, prev_ref, h_ref) = refs
    else:
        (z_ref, mu_ref, wl_ref, w0_ref, al_ref, a0_ref, gl_ref, kk_ref, ka_ref, rk_ref,
         lw_ref, lb_ref, seg_ref, segt_ref, o_ref, vo_ref, prev_ref, h_ref) = refs
    tb = z_ref.shape[0]
    t = RWKV_T
    n_chunk = tb // t
    width = o_ref.shape[1]
    n_pair = width // LANE

    @pl.when(pl.program_id(0) == 0)
    def _():
        prev_ref[...] = jnp.zeros_like(prev_ref)
        h_ref[...] = jnp.zeros_like(h_ref)

    z = z_ref[...]
    rowz = lax.broadcasted_iota(jnp.int32, z.shape, 0)
    prev = jnp.where(rowz == 0, prev_ref[...], pltpu.roll(z, 1, 0))
    prev_ref[...] = z[tb - 1:tb]
    zs = z + (prev - z) * mu_ref[...]

    o1 = 3 * width
    r = zs[:, 0:width]
    k = zs[:, width:2 * width]
    v = zs[:, 2 * width:o1]
    w_in = zs[:, o1:o1 + LANE]
    a_in = zs[:, o1 + LANE:o1 + 2 * LANE]
    g_in = zs[:, o1 + 2 * LANE:o1 + 2 * LANE + RWKV_GATE_LORA]

    wpre = w0_ref[...] + _dot_hp(jnp.tanh(w_in), wl_ref)
    logw = -jnp.exp(-_softplus(-wpre) - 0.5)
    a = _sigmoid(a0_ref[...] + _dot(a_in, al_ref[...]))
    g = _dot(_sigmoid(g_in), gl_ref[...])
    if has_vres:
        vr = zs[:, o1 + 2 * LANE + RWKV_GATE_LORA:]
        vg = _sigmoid(vbias_ref[...] + _dot(vr, vb_ref[...]))
        v = v + (vf_ref[...] - v) * vg
    else:
        vo_ref[...] = v

    seg = seg_ref[...]
    seg_t = segt_ref[...]
    kk = k * kk_ref[...]
    kk = kk * lax.rsqrt(jnp.maximum(_seg_sum(kk * kk, seg, seg_t), 1e-24))
    k2 = k * (1.0 + (a - 1.0) * ka_ref[...])
    kka = kk * a

    shift = t.bit_length() - 1
    ri = lax.broadcasted_iota(jnp.int32, (tb, tb), 0)
    ci = lax.broadcasted_iota(jnp.int32, (tb, tb), 1)
    tri = jnp.where((ci <= ri) & ((ri >> shift) == (ci >> shift)), 1.0, 0.0).astype(BF16)
    lc = _dot_exact_lhs(tri, logw)

    gi = lax.broadcasted_iota(jnp.int32, (4 * t, 4 * t), 0)
    gj = lax.broadcasted_iota(jnp.int32, (4 * t, 4 * t), 1)
    ti = gi & (t - 1)
    sj = gj & (t - 1)
    keep = sj + jnp.where(gi < 2 * t, 1, 0) <= ti
    lane = lax.broadcasted_iota(jnp.int32, (t, LANE), 1)
    head0 = lane < RWKV_HEAD

    def stack2(x):
        return jnp.concatenate([jnp.where(head0, x, 0.0), jnp.where(head0, 0.0, x)], axis=0)

    e_pos = jnp.exp(lc)
    e_neg = jnp.exp(-lc)
    e_prev = jnp.exp(lc - logw)
    rt_all = r * e_pos
    at_all = -kk * e_prev
    bt_all = kka * e_neg
    kt_all = k2 * e_neg

    items = [(c, p) for c in range(n_chunk) for p in range(n_pair)]
    tile = lambda x, c, p: x[c * t:(c + 1) * t, p * LANE:(p + 1) * LANE]
    at = [tile(at_all, c, p) for c, p in items]
    rt = [tile(rt_all, c, p) for c, p in items]
    vbd = [stack2(tile(v, c, p)) for c, p in items]
    gm = [jnp.where(keep, _dot_nt(jnp.concatenate([stack2(a_), stack2(r_)], axis=0),
                                  jnp.concatenate([stack2(tile(bt_all, c, p)), stack2(tile(kt_all, c, p))],
                                                  axis=0)), 0.0)
          for a_, r_, (c, p) in zip(at, rt, items)]
    n1 = [g_[0:2 * t, 0:2 * t] for g_ in gm]
    rhs0 = [_dot(g_[0:2 * t, 2 * t:4 * t], v_) for g_, v_ in zip(gm, vbd)]
    qi = lax.broadcasted_iota(jnp.int32, (2 * t, 2 * t), 0)
    qj = lax.broadcasted_iota(jnp.int32, (2 * t, 2 * t), 1)
    xinv = [jnp.where(qi == qj, 1.0, jnp.where((qi ^ qj) == 1, n_, 0.0)) for n_ in n1]
    for lvl in range(1, shift):
        couple = ((qi >> lvl) ^ (qj >> lvl)) == 1
        xm = [_dot(x_, jnp.where(couple, n_, 0.0)) for x_, n_ in zip(xinv, n1)]
        xinv = [x_ + _dot(m_, x_) for x_, m_ in zip(xinv, xm)]

    hts = [h_ref[p] for p in range(n_pair)]
    out_rows = []
    for c in range(n_chunk):
        idx = [c * n_pair + p for p in range(n_pair)]
        lc_c = lc[c * t:(c + 1) * t]
        e_rem = jnp.exp(lc_c[t - 1:t] - lc_c)
        bh_c = kka[c * t:(c + 1) * t] * e_rem
        kh_c = k2[c * t:(c + 1) * t] * e_rem
        e_last = e_pos[(c + 1) * t - 1:(c + 1) * t]
        ph = [_dot_nt(jnp.concatenate([at[i], rt[i]], axis=0), h_) for i, h_ in zip(idx, hts)]
        u = [_dot(xinv[i], stack2(p_[0:t]) + rhs0[i]) for i, p_ in zip(idx, ph)]
        uv = [jnp.concatenate([u_, vbd[i]], axis=0) for i, u_ in zip(idx, u)]
        opk = [stack2(p_[t:2 * t]) + _dot(gm[i][2 * t:4 * t, :], uv_) for i, p_, uv_ in zip(idx, ph, uv)]
        out_rows.append(jnp.concatenate([o_[0:t] + o_[t:2 * t] for o_ in opk], axis=1))
        hts = [h_ * e_last[:, p * LANE:(p + 1) * LANE]
               + _dot_tn(uv_, jnp.concatenate([stack2(bh_c[:, p * LANE:(p + 1) * LANE]),
                                               stack2(kh_c[:, p * LANE:(p + 1) * LANE])], axis=0))
               for p, (h_, uv_) in enumerate(zip(hts, uv))]
    for p in range(n_pair):
        h_ref[p] = hts[p]

    y = jnp.concatenate(out_rows, axis=0)
    inv_n = 1.0 / RWKV_HEAD
    mean = _seg_sum(y, seg, seg_t) * inv_n
    yc = y - mean
    var = _seg_sum(yc * yc, seg, seg_t) * inv_n
    yn = yc * lax.rsqrt(var + RWKV_LNX_EPS) * lw_ref[...] + lb_ref[...]
    bonus = _seg_sum(r * k2 * rk_ref[...], seg, seg_t) * v
    o_ref[...] = ((yn + bonus) * g).astype(o_ref.dtype)


def _rwkv_call(z, v_first, mu, wl, w0, al, a0, gl, k_k, k_a, r_k, lnx_w, lnx_b, seg, seg_t, vb, vbias):
    length, zc = z.shape
    width = w0.shape[1]
    has_vres = v_first is not None
    full = lambda a: pl.BlockSpec(a.shape, lambda i: (0,) * a.ndim)
    rows = lambda c: pl.BlockSpec((RWKV_BLOCK, c), lambda i: (i, 0))
    common = [mu, wl, w0, al, a0, gl, k_k, k_a, r_k, lnx_w, lnx_b, seg, seg_t]
    if has_vres:
        args = [z, v_first] + common + [vb, vbias]
        in_specs = [rows(zc), rows(width)] + [full(a) for a in common + [vb, vbias]]
        out_shape = jax.ShapeDtypeStruct((length, width), BF16)
        out_specs = rows(width)
    else:
        args = [z] + common
        in_specs = [rows(zc)] + [full(a) for a in common]
        out_shape = (jax.ShapeDtypeStruct((length, width), BF16),
                     jax.ShapeDtypeStruct((length, width), F32))
        out_specs = (rows(width), rows(width))
    return pl.pallas_call(
        functools.partial(_rwkv_kernel, has_vres),
        out_shape=out_shape,
        grid=(length // RWKV_BLOCK,),
        in_specs=in_specs,
        out_specs=out_specs,
        scratch_shapes=[pltpu.VMEM((1, zc), F32),
                        pltpu.VMEM((width // LANE, LANE, LANE), F32)],
        compiler_params=_params(("arbitrary",)),
        name="rwkv7_mixer",
    )(*args)


def _gla_kernel(z_ref, al_ref, ab_ref, ng_ref, o_ref, st_ref):
    tb = z_ref.shape[0]
    t = GLA_T
    n_chunk = tb // t
    width = o_ref.shape[1]
    n_head = width // GLA_DV

    @pl.when(pl.program_id(0) == 0)
    def _():
        st_ref[...] = jnp.zeros_like(st_ref)

    z = z_ref[...]
    kw = n_head * GLA_DK
    q = z[:, 0:kw] * (GLA_DK ** -0.5)
    k = z[:, kw:2 * kw]
    v = z[:, 2 * kw:2 * kw + width]
    g = z[:, 2 * kw + width:2 * kw + 2 * width]
    a_in = z[:, 2 * kw + 2 * width:]
    x = _dot_hp(a_in, al_ref) + ab_ref[...]
    log_a = -_softplus(-x) * (1.0 / GLA_TAU)

    shift = t.bit_length() - 1
    ri = lax.broadcasted_iota(jnp.int32, (tb, tb), 0)
    ci = lax.broadcasted_iota(jnp.int32, (tb, tb), 1)
    tri = jnp.where((ci <= ri) & ((ri >> shift) == (ci >> shift)), 1.0, 0.0).astype(BF16)
    b = _dot_exact_lhs(tri, log_a)
    causal = (lax.broadcasted_iota(jnp.int32, (t, t), 1) <= lax.broadcasted_iota(jnp.int32, (t, t), 0))

    q_in = q * jnp.exp(b)
    heads = range(n_head)
    head = lambda x, h: x[:, h * GLA_DV:(h + 1) * GLA_DV]
    head_k = lambda x, h: x[:, h * GLA_DK:(h + 1) * GLA_DK]
    chunk = lambda x, c: x[c * t:(c + 1) * t]
    v_c = [chunk(v, c) for c in range(n_chunk)]
    k_rem = []
    e_last = []
    intra = []
    for c in range(n_chunk):
        b_c = chunk(b, c)
        b_mid = b_c[t // 2:t // 2 + 1]
        b_last = b_c[t - 1:t]
        q_mid = chunk(q, c) * jnp.exp(b_c - b_mid)
        k_mid = chunk(k, c) * jnp.exp(b_mid - b_c)
        k_rem.append(chunk(k, c) * jnp.exp(b_last - b_c))
        e_last.append(jnp.exp(b_last))
        intra.append([_dot(jnp.where(causal, _dot_nt(head_k(q_mid, h), head_k(k_mid, h)), 0.0),
                           head(v_c[c], h)) for h in heads])
    sts = [st_ref[h] for h in heads]
    out_rows = []
    for c in range(n_chunk):
        q_c = chunk(q_in, c)
        o = [i_ + _dot_nt(head_k(q_c, h), s_) for h, (i_, s_) in enumerate(zip(intra[c], sts))]
        sts = [s_ * head_k(e_last[c], h) + _dot_tn(head(v_c[c], h), head_k(k_rem[c], h))
               for h, s_ in enumerate(sts)]
        out_rows.append(jnp.concatenate(
            [o_ * lax.rsqrt(jnp.mean(o_ * o_, axis=-1, keepdims=True) + NORM_EPS) for o_ in o], axis=1))
    for h in heads:
        st_ref[h] = sts[h]
    o = jnp.concatenate(out_rows, axis=0)
    o_ref[...] = (o * ng_ref[...] * (g * _sigmoid(g))).astype(o_ref.dtype)


def _gla_call(z, al, ab, ng):
    length, zc = z.shape
    width = ng.shape[1]
    full = lambda a: pl.BlockSpec(a.shape, lambda i: (0,) * a.ndim)
    return pl.pallas_call(
        _gla_kernel,
        out_shape=jax.ShapeDtypeStruct((length, width), BF16),
        grid=(length // GLA_BLOCK,),
        in_specs=[pl.BlockSpec((GLA_BLOCK, zc), lambda i: (i, 0)), full(al), full(ab), full(ng)],
        out_specs=pl.BlockSpec((GLA_BLOCK, width), lambda i: (i, 0)),
        scratch_shapes=[pltpu.VMEM((width // GLA_DV, GLA_DV, GLA_DK), F32)],
        compiler_params=_params(("arbitrary",)),
        name="gla_mixer",
    )(z, al, ab, ng)


def _merge_kernel(ya_ref, yb_ref, yc_ref, gate_ref, x_ref, wua_ref, wub_ref, wuc_ref, wo_ref, g_ref,
                  xo_ref, h_ref):
    d = x_ref.shape[1]
    gates = gate_ref[...].astype(F32)
    merged = (gates[:, 0:d] * jnp.dot(ya_ref[...], wua_ref[...], preferred_element_type=F32)
              + gates[:, d:2 * d] * jnp.dot(yb_ref[...], wub_ref[...], preferred_element_type=F32)
              + gates[:, 2 * d:3 * d] * jnp.dot(yc_ref[...], wuc_ref[...], preferred_element_type=F32))
    x = x_ref[...] + _dot(merged, wo_ref[...])
    xo_ref[...] = x
    h_ref[...] = _rms_rows(x, g_ref[...]).astype(h_ref.dtype)


def _merge_call(ya, yb, yc, gates, x, wua, wub, wuc, wo, g, tm=256):
    m, d = x.shape
    rows = lambda a: pl.BlockSpec((tm, a.shape[1]), lambda i: (i, 0))
    const = lambda a: pl.BlockSpec(a.shape, lambda i: (0,) * a.ndim, pipeline_mode=pl.Buffered(1))
    return pl.pallas_call(
        _merge_kernel,
        out_shape=(jax.ShapeDtypeStruct((m, d), F32), jax.ShapeDtypeStruct((m, d), BF16)),
        grid=(m // tm,),
        in_specs=[rows(ya), rows(yb), rows(yc), rows(gates), rows(x),
                  const(wua), const(wub), const(wuc), const(wo), const(g)],
        out_specs=(pl.BlockSpec((tm, d), lambda i: (i, 0)), pl.BlockSpec((tm, d), lambda i: (i, 0))),
        compiler_params=_params(("parallel",)),
        name="merge_out_proj",
    )(ya, yb, yc, gates, x, wua, wub, wuc, wo, g)


def _mlp_kernel(final, h_ref, w1_ref, w2_ref, x_ref, g_ref, xo_ref, *rest):
    f = pl.program_id(1)

    @pl.when(f == 0)
    def _():
        xo_ref[...] = x_ref[...]

    hid = jnp.maximum(jnp.dot(h_ref[...], w1_ref[...], preferred_element_type=F32), 0.0)
    xo_ref[...] += _dot(hid * hid, w2_ref[...])

    @pl.when(f == pl.num_programs(1) - 1)
    def _():
        y = _rms_rows(xo_ref[...], g_ref[...])
        if final:
            xo_ref[...] = y
        else:
            rest[0][...] = y.astype(rest[0].dtype)


def _mlp_call(h, w1, w2, x, g_next, final, tm=512, tf=1024):
    m, d = x.shape
    ff = w1.shape[1]
    row_blk = pl.BlockSpec((tm, d), lambda i, f: (i, 0))
    out_shape = [jax.ShapeDtypeStruct((m, d), F32)]
    out_specs = [row_blk]
    if not final:
        out_shape.append(jax.ShapeDtypeStruct((m, d), BF16))
        out_specs.append(row_blk)
    return pl.pallas_call(
        functools.partial(_mlp_kernel, final),
        out_shape=tuple(out_shape),
        grid=(m // tm, ff // tf),
        in_specs=[row_blk,
                  pl.BlockSpec((d, tf), lambda i, f: (0, f)),
                  pl.BlockSpec((tf, d), lambda i, f: (f, 0)),
                  row_blk,
                  pl.BlockSpec((1, d), lambda i, f: (0, 0))],
        out_specs=tuple(out_specs),
        compiler_params=_params(("parallel", "arbitrary")),
        name="mlp",
    )(h, w1, w2, x, g_next)


def _pad_cols(a, n):
    return jnp.pad(a, [(0, 0)] * (a.ndim - 1) + [(0, n - a.shape[-1])])


def _pad_rows(a, n):
    return jnp.pad(a, [(0, 0)] * (a.ndim - 2) + [(0, n - a.shape[-2]), (0, 0)])


def kernel(x, norm_mix, w_in, gate_bias, s5_lambda_re, s5_lambda_im, s5_log_step, s5_b_re, s5_b_im, s5_c_re, s5_c_im, s5_d, s5_glu_w, s5_glu_b, rwkv_mu, rwkv_w_lora, rwkv_w0, rwkv_a_lora, rwkv_a0, rwkv_g_lora, rwkv_k_k, rwkv_k_a, rwkv_r_k, rwkv_lnx_w, rwkv_lnx_b, rwkv_vres_a, rwkv_vres_mu, rwkv_vres_b, rwkv_vres_bias, gla_alpha_lora, gla_alpha_bias, gla_norm_g, w_up, w_out, norm_mlp, mlp_w1, mlp_w2, final_norm):
    bsz, length, d = x.shape
    depth = w_in.shape[0]
    s5_w = s5_d.shape[1]
    rw_w = rwkv_w0.shape[1]
    gla_v = gla_norm_g.shape[1]
    gla_heads = gla_v // GLA_DV
    gla_k = gla_heads * GLA_DK
    rw_cols = 3 * rw_w + RWKV_DECAY_LORA + RWKV_AAA_LORA + RWKV_GATE_LORA
    gla_cols = 2 * gla_k + 2 * gla_v + GLA_LORA
    o_rw = s5_w
    o_gla = o_rw + rw_cols
    o_gate = o_gla + gla_cols
    o1 = 3 * rw_w

    head_id = jnp.arange(rw_w) // RWKV_HEAD
    seg = (head_id[:, None] == jnp.arange(LANE)[None, :]).astype(BF16)
    seg_t = seg.T
    rows1 = lambda a: a.reshape(a.shape[0], 1, -1).astype(F32)

    w_in_bf = w_in.astype(BF16)
    w_s5 = w_in_bf[:, :, :o_rw]
    wr = w_in_bf[:, :, o_rw:o_gla]
    o2 = o1 + RWKV_DECAY_LORA
    vres_a = jnp.concatenate([jnp.zeros((1,) + rwkv_vres_a.shape[1:], BF16), rwkv_vres_a.astype(BF16)], axis=0)
    vres_mu = jnp.concatenate([jnp.zeros((1,) + rwkv_vres_mu.shape[1:], F32), rwkv_vres_mu.astype(F32)], axis=0)
    w_rw = jnp.concatenate([wr[:, :, :o1], _pad_cols(wr[:, :, o1:o2], LANE), wr[:, :, o2:],
                            _pad_cols(vres_a, LANE)], axis=2)
    mu_rw = rows1(jnp.concatenate([rwkv_mu[:, :o1], _pad_cols(rwkv_mu[:, o1:o2], LANE), rwkv_mu[:, o2:],
                                   _pad_cols(vres_mu, LANE)], axis=1))
    wg = w_in_bf[:, :, o_gla:o_gate]
    w_gla = jnp.concatenate([wg[:, :, :2 * gla_k + 2 * gla_v],
                             _pad_cols(wg[:, :, 2 * gla_k + 2 * gla_v:], LANE)], axis=2)
    w_gate = w_in_bf[:, :, o_gate:]
    gate_b = rows1(gate_bias)
    s5_tabs = jax.vmap(_s5_tables)(s5_lambda_re, s5_lambda_im, s5_log_step, s5_b_re.astype(F32),
                                   s5_b_im.astype(F32), s5_c_re, s5_c_im)
    s5_dd, s5_gw, s5_gb = rows1(s5_d), s5_glu_w.astype(BF16), rows1(s5_glu_b)
    rw_wl = jax.vmap(_split_weight)(_pad_rows(rwkv_w_lora, LANE))
    rw_vb = _pad_rows(rwkv_vres_b, LANE).astype(BF16)
    rw_rows = [rows1(a) for a in (rwkv_w0, rwkv_a0, rwkv_k_k, rwkv_k_a, rwkv_r_k, rwkv_lnx_w, rwkv_lnx_b)]
    rw_al, rw_gl, rw_vbias = rwkv_a_lora.astype(BF16), rwkv_g_lora.astype(BF16), rows1(rwkv_vres_bias)
    gla_al = jax.vmap(_split_weight)(_pad_rows(gla_alpha_lora, LANE))
    gla_ab = rows1(gla_alpha_bias)
    gla_ng = rows1(gla_norm_g)
    wu_bf, wo_bf = w_up.astype(BF16), w_out.astype(BF16)
    w1_bf, w2_bf = mlp_w1.astype(BF16), mlp_w2.astype(BF16)
    n_mix, n_mlp, n_fin = rows1(norm_mix), rows1(norm_mlp), final_norm.reshape(1, -1).astype(F32)

    outs = []
    for b in range(bsz):
        xb = x[b].astype(F32)
        u = _norm_call(xb, n_mix[0])
        v_first = None
        for l in range(depth):
            z_s5 = _mm_call(u, w_s5[l], tn=s5_w, name="in_proj_s5")
            z_rw = _mm_call(u, w_rw[l], tn=w_rw.shape[2] // 2, name="in_proj_rwkv")
            z_gla = _mm_call(u, w_gla[l], tn=w_gla.shape[2] // 2, name="in_proj_gla")
            gates = _mm_call(u, w_gate[l], tn=1024, bias=gate_b[l], tm=2048, name="in_proj_gate")

            y_a = _s5_call(z_s5, *[t_[l] for t_ in s5_tabs], s5_dd[l], s5_gw[l], s5_gb[l])

            w0, a0, k_k, k_a, r_k, lnx_w, lnx_b = [a[l] for a in rw_rows]
            vb, vbias = (rw_vb[l - 1], rw_vbias[l - 1]) if l > 0 else (None, None)
            res = _rwkv_call(z_rw, v_first, mu_rw[l], rw_wl[l], w0, rw_al[l], a0, rw_gl[l],
                             k_k, k_a, r_k, lnx_w, lnx_b, seg, seg_t, vb, vbias)
            if l == 0:
                y_b, v_first = res
            else:
                y_b = res

            y_c = _gla_call(z_gla, gla_al[l], gla_ab[l], gla_ng[l])

            wu = wu_bf[l]
            x_mid, h = _merge_call(y_a, y_b, y_c, gates, xb,
                                   wu[:s5_w], wu[s5_w:s5_w + rw_w], wu[s5_w + rw_w:],
                                   wo_bf[l], n_mlp[l])
            final = l == depth - 1
            res = _mlp_call(h, w1_bf[l], w2_bf[l], x_mid, n_fin if final else n_mix[l + 1], final)
            if final:
                xb = res[0]
            else:
                xb, u = res
        outs.append(xb.astype(x.dtype))
    return jnp.stack(outs, axis=0)
```

```python
import functools
import math

import jax
import jax.numpy as jnp
from jax import lax
from jax.experimental import pallas as pl
from jax.experimental.pallas import tpu as pltpu

F32 = jnp.float32
BF16 = jnp.bfloat16

LANE = 128
SUBLANE = 8
NORM_EPS = 1e-6

S5_GROUP = 16
S5_STATE = 64
S5_SLAB = 128
S5_SLAB_STATES = (S5_SLAB // S5_GROUP) * S5_STATE
S5_T = 256
S5_BLOCK_LEVELS = 3
S5_FOLD = 4

RWKV_HEAD = 64
RWKV_T = 64
RWKV_BLOCK = 256
RWKV_DECAY_LORA = 96
RWKV_AAA_LORA = 128
RWKV_GATE_LORA = 256
RWKV_LNX_EPS = 64e-5

GLA_DK = 64
GLA_DV = 128
GLA_LORA = 16
GLA_TAU = 16.0
GLA_T = 64
GLA_BLOCK = 256

VMEM_LIMIT = 56 * 1024 * 1024


def _dot(a, b):
    return jnp.dot(a.astype(BF16), b.astype(BF16), preferred_element_type=F32)


def _dot_nt(a, b):
    return lax.dot_general(a.astype(BF16), b.astype(BF16), (((1,), (1,)), ((), ())),
                           preferred_element_type=F32)


def _dot_tn(a, b):
    return lax.dot_general(a.astype(BF16), b.astype(BF16), (((0,), (0,)), ((), ())),
                           preferred_element_type=F32)


def _split3(x):
    hi = x.astype(BF16)
    r1 = x - hi.astype(F32)
    mid = r1.astype(BF16)
    lo = (r1 - mid.astype(F32)).astype(BF16)
    return hi, mid, lo


def _chunk_cumsum(x, t):
    tri = jnp.where(lax.broadcasted_iota(jnp.int32, (t, t), 1) <= lax.broadcasted_iota(jnp.int32, (t, t), 0),
                    1.0, 0.0).astype(BF16)
    parts = _split3(x)
    return jnp.concatenate(
        [sum(jnp.dot(tri, p_[c * t:(c + 1) * t], preferred_element_type=F32) for p_ in parts)
         for c in range(x.shape[0] // t)], axis=0)


def _hi_lo(x):
    hi = x.astype(BF16)
    return hi, (x - hi.astype(F32)).astype(BF16)


def _seg_sum(x, seg, seg_t):
    s = jnp.dot(x.astype(BF16), seg, preferred_element_type=F32)
    hi, lo = _hi_lo(s)
    return jnp.dot(hi, seg_t, preferred_element_type=F32) + jnp.dot(lo, seg_t, preferred_element_type=F32)


def _split_weight(w):
    hi, lo = _hi_lo(w.astype(F32))
    return jnp.stack([hi, lo])


def _dot_hp(a, w_ref):
    a_hi, a_lo = _hi_lo(a)
    return (jnp.dot(a_hi, w_ref[0], preferred_element_type=F32)
            + jnp.dot(a_lo, w_ref[0], preferred_element_type=F32)
            + jnp.dot(a_hi, w_ref[1], preferred_element_type=F32))


def _sigmoid(x):
    return 1.0 / (1.0 + jnp.exp(-x))


def _softplus(x):
    return jnp.maximum(x, 0.0) + jnp.log(1.0 + jnp.exp(-jnp.abs(x)))


def _rms_rows(x, g):
    return x * lax.rsqrt(jnp.mean(x * x, axis=-1, keepdims=True) + NORM_EPS) * g


def _params(sem):
    return pltpu.CompilerParams(dimension_semantics=sem, vmem_limit_bytes=VMEM_LIMIT)


def _norm_kernel(x_ref, g_ref, o_ref):
    o_ref[...] = _rms_rows(x_ref[...], g_ref[...]).astype(o_ref.dtype)


def _norm_call(x, g, tm=512):
    m, d = x.shape
    assert m % tm == 0, (m, tm)
    return pl.pallas_call(
        _norm_kernel,
        out_shape=jax.ShapeDtypeStruct((m, d), BF16),
        grid=(m // tm,),
        in_specs=[pl.BlockSpec((tm, d), lambda i: (i, 0)),
                  pl.BlockSpec((1, d), lambda i: (0, 0))],
        out_specs=pl.BlockSpec((tm, d), lambda i: (i, 0)),
        compiler_params=_params(("parallel",)),
        name="rmsnorm",
    )(x, g)


def _mm_kernel(a_ref, w_ref, o_ref):
    o_ref[...] = jnp.dot(a_ref[...], w_ref[...], preferred_element_type=F32).astype(o_ref.dtype)


def _mm_gate_kernel(a_ref, w_ref, b_ref, o_ref):
    z = jnp.dot(a_ref[...], w_ref[...], preferred_element_type=F32) + b_ref[...]
    o_ref[...] = _sigmoid(z).astype(o_ref.dtype)


def _mm_call(a, w, tn, bias=None, tm=1024, name="in_proj"):
    m, k = a.shape
    n = w.shape[1]
    assert m % tm == 0 and n % tn == 0, (m, tm, n, tn)
    in_specs = [pl.BlockSpec((tm, k), lambda i, j: (i, 0)),
                pl.BlockSpec((k, tn), lambda i, j: (0, j))]
    args = [a, w]
    if bias is None:
        body, out_dtype = _mm_kernel, F32
    else:
        body, out_dtype = _mm_gate_kernel, BF16
        in_specs.append(pl.BlockSpec((1, tn), lambda i, j: (0, j)))
        args.append(bias)
    return pl.pallas_call(
        body,
        out_shape=jax.ShapeDtypeStruct((m, n), out_dtype),
        grid=(m // tm, n // tn),
        in_specs=in_specs,
        out_specs=pl.BlockSpec((tm, tn), lambda i, j: (i, j)),
        compiler_params=_params(("parallel", "parallel")),
        name=name,
    )(*args)


def _s5_kernel(z_ref, bre_ref, bim_ref, pre_ref, pim_ref, cre_ref, cim_ref, d_ref, gw_ref, gb_ref,
               o_ref, carry_re, carry_im):
    n_slab = bre_ref.shape[0]
    tiles_per_slab = S5_SLAB_STATES // LANE
    t = z_ref.shape[0]

    @pl.when(pl.program_id(0) == 0)
    def _():
        carry_re[...] = jnp.zeros_like(carry_re)
        carry_im[...] = jnp.zeros_like(carry_im)

    u = z_ref[...]
    ub = u.astype(BF16)
    n_blk = t // SUBLANE
    row_in_block = lax.broadcasted_iota(jnp.int32, u.shape, 0) & (SUBLANE - 1)
    shifted = [ub] + [jnp.where(row_in_block >= d, pltpu.roll(u, d, 0), 0.0).astype(BF16)
                      for d in range(1, S5_FOLD)]
    ys = []
    for j in range(n_slab):
        uj = jnp.concatenate([s_[:, j * S5_SLAB:(j + 1) * S5_SLAB] for s_ in shifted], axis=1)
        bu_re = jnp.dot(uj, bre_ref[j], preferred_element_type=F32)
        bu_im = jnp.dot(uj, bim_ref[j], preferred_element_type=F32)
        tiles_re = []
        tiles_im = []
        for c in range(tiles_per_slab):
            idx = j * tiles_per_slab + c
            sr = bu_re[:, c * LANE:(c + 1) * LANE].reshape(n_blk, SUBLANE, LANE)
            si = bu_im[:, c * LANE:(c + 1) * LANE].reshape(n_blk, SUBLANE, LANE)
            pr = pre_ref[idx]
            pi = pim_ref[idx]
            for k in range(S5_FOLD.bit_length() - 1, S5_BLOCK_LEVELS):
                qr = pltpu.roll(sr, 1 << k, 1)
                qi = pltpu.roll(si, 1 << k, 1)
                sr, si = sr + (pr[k] * qr - pi[k] * qi), si + (pr[k] * qi + pi[k] * qr)
            cr = jnp.broadcast_to(carry_re[idx], (SUBLANE, LANE))
            ci = jnp.broadcast_to(carry_im[idx], (SUBLANE, LANE))
            ar = pr[S5_BLOCK_LEVELS]
            ai = pi[S5_BLOCK_LEVELS]
            blocks_re = []
            blocks_im = []
            for b in range(n_blk):
                br = sr[b] + (ar * cr - ai * ci)
                bi = si[b] + (ar * ci + ai * cr)
                blocks_re.append(br)
                blocks_im.append(bi)
                cr = jnp.broadcast_to(br[SUBLANE - 1:SUBLANE], (SUBLANE, LANE))
                ci = jnp.broadcast_to(bi[SUBLANE - 1:SUBLANE], (SUBLANE, LANE))
            carry_re[idx] = cr[0:1]
            carry_im[idx] = ci[0:1]
            tiles_re.append(jnp.concatenate(blocks_re, axis=0).astype(BF16))
            tiles_im.append(jnp.concatenate(blocks_im, axis=0).astype(BF16))
        s_re = jnp.concatenate(tiles_re, axis=1)
        s_im = jnp.concatenate(tiles_im, axis=1)
        ys.append(jnp.dot(s_re, cre_ref[j], preferred_element_type=F32)
                  - jnp.dot(s_im, cim_ref[j], preferred_element_type=F32))
    y = jnp.concatenate(ys, axis=1) + d_ref[...] * u
    y = y * (0.5 * (1.0 + jnp.tanh(math.sqrt(2.0 / math.pi) * (y + 0.044715 * (y * y * y)))))
    y = y * _sigmoid(_dot(y, gw_ref[...]) + gb_ref[...])
    o_ref[...] = y.astype(o_ref.dtype)


def _s5_call(z, bre, bim, pre, pim, cre, cim, d, gw, gb):
    length, width = z.shape
    assert length % S5_T == 0 and width % S5_SLAB == 0, (length, width)
    n_slab = bre.shape[0]
    n_tiles = n_slab * (S5_SLAB_STATES // LANE)
    full = lambda a: pl.BlockSpec(a.shape, lambda i: (0,) * a.ndim, pipeline_mode=pl.Buffered(1))
    return pl.pallas_call(
        _s5_kernel,
        out_shape=jax.ShapeDtypeStruct((length, width), BF16),
        grid=(length // S5_T,),
        in_specs=[pl.BlockSpec((S5_T, width), lambda i: (i, 0)),
                  full(bre), full(bim), full(pre), full(pim), full(cre), full(cim),
                  full(d), full(gw), full(gb)],
        out_specs=pl.BlockSpec((S5_T, width), lambda i: (i, 0)),
        scratch_shapes=[pltpu.VMEM((n_tiles, 1, LANE), F32),
                        pltpu.VMEM((n_tiles, 1, LANE), F32)],
        compiler_params=_params(("arbitrary",)),
        name="s5_mixer",
    )(z, bre, bim, pre, pim, cre, cim, d, gw, gb)


def _s5_tables(lam_re, lam_im, log_step, b_re, b_im, c_re, c_im):
    groups = lam_re.shape[0]
    n_slab = groups * S5_GROUP // S5_SLAB
    gps = S5_SLAB // S5_GROUP
    lr = jnp.minimum(lam_re.astype(F32), -1e-4)
    li = lam_im.astype(F32)
    dt = jnp.exp(log_step.astype(F32))[:, None]
    e = jnp.exp(lr * dt)
    lb_re = e * jnp.cos(li * dt)
    lb_im = e * jnp.sin(li * dt)
    den = lr * lr + li * li
    f_re = ((lb_re - 1.0) * lr + lb_im * li) / den
    f_im = (lb_im * lr - (lb_re - 1.0) * li) / den
    bb_re = f_re[..., None] * b_re - f_im[..., None] * b_im
    bb_im = f_re[..., None] * b_im + f_im[..., None] * b_re
    eye = jnp.eye(gps, dtype=F32)

    def bd_b(m):
        m = m.reshape(n_slab, gps, S5_STATE, S5_GROUP)
        return jnp.einsum('jgpi,gh->jgihp', m, eye).reshape(n_slab, S5_SLAB, S5_SLAB_STATES).astype(BF16)

    def bd_c(m):
        m = m.reshape(n_slab, gps, S5_GROUP, S5_STATE)
        return jnp.einsum('jgip,gh->jgphi', m, eye).reshape(n_slab, S5_SLAB_STATES, S5_SLAB).astype(BF16)

    r_idx = jnp.arange(SUBLANE)
    steps = 2 ** jnp.arange(S5_BLOCK_LEVELS)
    expo = jnp.concatenate([jnp.broadcast_to(steps[:, None], (S5_BLOCK_LEVELS, SUBLANE)),
                            (r_idx + 1)[None, :]], axis=0).astype(F32)
    keep = jnp.concatenate([r_idx[None, :] >= steps[:, None],
                            jnp.ones((1, SUBLANE), bool)], axis=0).astype(F32)
    ph = (li * dt).reshape(-1) * expo[..., None]
    mag = keep[..., None] * jnp.exp((lr * dt).reshape(-1) * expo[..., None])
    n_tiles = ph.shape[-1] // LANE
    tile_major = lambda a: a.reshape(a.shape[0], SUBLANE, n_tiles, LANE).transpose(2, 0, 1, 3)
    pw_re = tile_major(mag * jnp.cos(ph))
    pw_im = tile_major(mag * jnp.sin(ph))
    fold = jnp.arange(S5_FOLD, dtype=F32)[:, None, None]
    fd_mag = jnp.exp(lr * dt * fold)
    fd_re = (fd_mag * jnp.cos(li * dt * fold))[..., None]
    fd_im = (fd_mag * jnp.sin(li * dt * fold))[..., None]
    b_stack_re = jnp.concatenate([bd_b(fd_re[d] * bb_re - fd_im[d] * bb_im) for d in range(S5_FOLD)], axis=1)
    b_stack_im = jnp.concatenate([bd_b(fd_re[d] * bb_im + fd_im[d] * bb_re) for d in range(S5_FOLD)], axis=1)
    return (b_stack_re, b_stack_im, pw_re, pw_im,
            bd_c(c_re.astype(F32)), bd_c(c_im.astype(F32)))


def _rwkv_kernel(has_vres, *refs):
    if has_vres:
        (z_ref, vf_ref, mu_ref, wl_ref, w0_ref, al_ref, a0_ref, gl_ref, kk_ref, ka_ref, rk_ref,
         lw_ref, lb_ref, seg_ref, segt_ref, vb_ref, vbias_ref, o_ref, prev_ref, h_ref) = refs
    else:
        (z_ref, mu_ref, wl_ref, w0_ref, al_ref, a0_ref, gl_ref, kk_ref, ka_ref, rk_ref,
         lw_ref, lb_ref, seg_ref, segt_ref, o_ref, vo_ref, prev_ref, h_ref) = refs
    tb = z_ref.shape[0]
    t = RWKV_T
    n_chunk = tb // t
    width = o_ref.shape[1]
    n_pair = width // LANE

    @pl.when(pl.program_id(0) == 0)
    def _():
        prev_ref[...] = jnp.zeros_like(prev_ref)
        h_ref[...] = jnp.zeros_like(h_ref)

    z = z_ref[...]
    rowz = lax.broadcasted_iota(jnp.int32, z.shape, 0)
    prev = jnp.where(rowz == 0, prev_ref[...], pltpu.roll(z, 1, 0))
    prev_ref[...] = z[tb - 1:tb]
    zs = z + (prev - z) * mu_ref[...]

    o1 = 3 * width
    r = zs[:, 0:width]
    k = zs[:, width:2 * width]
    v = zs[:, 2 * width:o1]
    w_in = zs[:, o1:o1 + LANE]
    a_in = zs[:, o1 + LANE:o1 + 2 * LANE]
    g_in = zs[:, o1 + 2 * LANE:o1 + 2 * LANE + RWKV_GATE_LORA]

    wpre = w0_ref[...] + _dot_hp(jnp.tanh(w_in), wl_ref)
    logw = -jnp.exp(-_softplus(-wpre) - 0.5)
    a = _sigmoid(a0_ref[...] + _dot(a_in, al_ref[...]))
    g = _dot(_sigmoid(g_in), gl_ref[...])
    if has_vres:
        vr = zs[:, o1 + 2 * LANE + RWKV_GATE_LORA:]
        vg = _sigmoid(vbias_ref[...] + _dot(vr, vb_ref[...]))
        v = v + (vf_ref[...] - v) * vg
    else:
        vo_ref[...] = v

    seg = seg_ref[...]
    seg_t = segt_ref[...]
    kk = k * kk_ref[...]
    kk = kk * lax.rsqrt(jnp.maximum(_seg_sum(kk * kk, seg, seg_t), 1e-24))
    k2 = k * (1.0 + (a - 1.0) * ka_ref[...])
    kka = kk * a

    shift = t.bit_length() - 1
    lc = _chunk_cumsum(logw, t)

    gi = lax.broadcasted_iota(jnp.int32, (4 * t, 4 * t), 0)
    gj = lax.broadcasted_iota(jnp.int32, (4 * t, 4 * t), 1)
    ti = gi & (t - 1)
    sj = gj & (t - 1)
    keep = sj + jnp.where(gi < 2 * t, 1, 0) <= ti
    lane = lax.broadcasted_iota(jnp.int32, (t, LANE), 1)
    head0 = lane < RWKV_HEAD

    def stack2(x):
        return jnp.concatenate([jnp.where(head0, x, 0.0), jnp.where(head0, 0.0, x)], axis=0)

    e_pos = jnp.exp(lc)
    e_neg = jnp.exp(-lc)
    e_prev = jnp.exp(lc - logw)
    rt_all = r * e_pos
    at_all = -kk * e_prev
    bt_all = kka * e_neg
    kt_all = k2 * e_neg

    items = [(c, p) for c in range(n_chunk) for p in range(n_pair)]
    tile = lambda x, c, p: x[c * t:(c + 1) * t, p * LANE:(p + 1) * LANE]
    at = [tile(at_all, c, p) for c, p in items]
    rt = [tile(rt_all, c, p) for c, p in items]
    vbd = [stack2(tile(v, c, p)) for c, p in items]
    gm = [jnp.where(keep, _dot_nt(jnp.concatenate([stack2(a_), stack2(r_)], axis=0),
                                  jnp.concatenate([stack2(tile(bt_all, c, p)), stack2(tile(kt_all, c, p))],
                                                  axis=0)), 0.0)
          for a_, r_, (c, p) in zip(at, rt, items)]
    n1 = [g_[0:2 * t, 0:2 * t] for g_ in gm]
    rhs0 = [_dot(g_[0:2 * t, 2 * t:4 * t], v_) for g_, v_ in zip(gm, vbd)]
    qi = lax.broadcasted_iota(jnp.int32, (2 * t, 2 * t), 0)
    qj = lax.broadcasted_iota(jnp.int32, (2 * t, 2 * t), 1)
    xinv = [jnp.where(qi == qj, 1.0, jnp.where((qi ^ qj) == 1, n_, 0.0)) for n_ in n1]
    for lvl in range(1, shift):
        couple = ((qi >> lvl) ^ (qj >> lvl)) == 1
        xm = [_dot(x_, jnp.where(couple, n_, 0.0)) for x_, n_ in zip(xinv, n1)]
        xinv = [x_ + _dot(m_, x_) for x_, m_ in zip(xinv, xm)]

    hts = [h_ref[p] for p in range(n_pair)]
    out_rows = []
    for c in range(n_chunk):
        idx = [c * n_pair + p for p in range(n_pair)]
        lc_c = lc[c * t:(c + 1) * t]
        e_rem = jnp.exp(lc_c[t - 1:t] - lc_c)
        bh_c = kka[c * t:(c + 1) * t] * e_rem
        kh_c = k2[c * t:(c + 1) * t] * e_rem
        e_last = e_pos[(c + 1) * t - 1:(c + 1) * t]
        ph = [_dot_nt(jnp.concatenate([at[i], rt[i]], axis=0), h_) for i, h_ in zip(idx, hts)]
        u = [_dot(xinv[i], stack2(p_[0:t]) + rhs0[i]) for i, p_ in zip(idx, ph)]
        uv = [jnp.concatenate([u_, vbd[i]], axis=0) for i, u_ in zip(idx, u)]
        opk = [stack2(p_[t:2 * t]) + _dot(gm[i][2 * t:4 * t, :], uv_) for i, p_, uv_ in zip(idx, ph, uv)]
        out_rows.append(jnp.concatenate([o_[0:t] + o_[t:2 * t] for o_ in opk], axis=1))
        hts = [h_ * e_last[:, p * LANE:(p + 1) * LANE]
               + _dot_tn(uv_, jnp.concatenate([stack2(bh_c[:, p * LANE:(p + 1) * LANE]),
                                               stack2(kh_c[:, p * LANE:(p + 1) * LANE])], axis=0))
               for p, (h_, uv_) in enumerate(zip(hts, uv))]
    for p in range(n_pair):
        h_ref[p] = hts[p]

    y = jnp.concatenate(out_rows, axis=0)
    inv_n = 1.0 / RWKV_HEAD
    mean = _seg_sum(y, seg, seg_t) * inv_n
    yc = y - mean
    var = _seg_sum(yc * yc, seg, seg_t) * inv_n
    yn = yc * lax.rsqrt(var + RWKV_LNX_EPS) * lw_ref[...] + lb_ref[...]
    bonus = _seg_sum(r * k2 * rk_ref[...], seg, seg_t) * v
    o_ref[...] = ((yn + bonus) * g).astype(o_ref.dtype)


def _rwkv_call(z, v_first, mu, wl, w0, al, a0, gl, k_k, k_a, r_k, lnx_w, lnx_b, seg, seg_t, vb, vbias):
    length, zc = z.shape
    width = w0.shape[1]
    assert length % RWKV_BLOCK == 0 and width % (2 * RWKV_HEAD) == 0, (length, width)
    has_vres = v_first is not None
    full = lambda a: pl.BlockSpec(a.shape, lambda i: (0,) * a.ndim)
    rows = lambda c: pl.BlockSpec((RWKV_BLOCK, c), lambda i: (i, 0))
    common = [mu, wl, w0, al, a0, gl, k_k, k_a, r_k, lnx_w, lnx_b, seg, seg_t]
    if has_vres:
        args = [z, v_first] + common + [vb, vbias]
        in_specs = [rows(zc), rows(width)] + [full(a) for a in common + [vb, vbias]]
        out_shape = jax.ShapeDtypeStruct((length, width), BF16)
        out_specs = rows(width)
    else:
        args = [z] + common
        in_specs = [rows(zc)] + [full(a) for a in common]
        out_shape = (jax.ShapeDtypeStruct((length, width), BF16),
                     jax.ShapeDtypeStruct((length, width), F32))
        out_specs = (rows(width), rows(width))
    return pl.pallas_call(
        functools.partial(_rwkv_kernel, has_vres),
        out_shape=out_shape,
        grid=(length // RWKV_BLOCK,),
        in_specs=in_specs,
        out_specs=out_specs,
        scratch_shapes=[pltpu.VMEM((1, zc), F32),
                        pltpu.VMEM((width // LANE, LANE, LANE), F32)],
        compiler_params=_params(("arbitrary",)),
        name="rwkv7_mixer",
    )(*args)


def _gla_kernel(z_ref, al_ref, ab_ref, ng_ref, o_ref, st_ref):
    tb = z_ref.shape[0]
    t = GLA_T
    n_chunk = tb // t
    width = o_ref.shape[1]
    n_head = width // GLA_DV

    @pl.when(pl.program_id(0) == 0)
    def _():
        st_ref[...] = jnp.zeros_like(st_ref)

    z = z_ref[...]
    kw = n_head * GLA_DK
    q = z[:, 0:kw] * (GLA_DK ** -0.5)
    k = z[:, kw:2 * kw]
    v = z[:, 2 * kw:2 * kw + width]
    g = z[:, 2 * kw + width:2 * kw + 2 * width]
    a_in = z[:, 2 * kw + 2 * width:]
    x = _dot_hp(a_in, al_ref) + ab_ref[...]
    log_a = -_softplus(-x) * (1.0 / GLA_TAU)

    b = _chunk_cumsum(log_a, t)
    causal =(lax.broadcasted_iota(jnp.int32, (t, t), 1) <= lax.broadcasted_iota(jnp.int32, (t, t), 0))

    q_in = q * jnp.exp(b)
    heads = range(n_head)
    head = lambda x, h: x[:, h * GLA_DV:(h + 1) * GLA_DV]
    head_k = lambda x, h: x[:, h * GLA_DK:(h + 1) * GLA_DK]
    chunk = lambda x, c: x[c * t:(c + 1) * t]
    v_c = [chunk(v, c) for c in range(n_chunk)]
    k_rem = []
    e_last = []
    intra = []
    for c in range(n_chunk):
        b_c = chunk(b, c)
        b_mid = b_c[t // 2:t // 2 + 1]
        b_last = b_c[t - 1:t]
        q_mid = chunk(q, c) * jnp.exp(b_c - b_mid)
        k_mid = chunk(k, c) * jnp.exp(b_mid - b_c)
        k_rem.append(chunk(k, c) * jnp.exp(b_last - b_c))
        e_last.append(jnp.exp(b_last))
        intra.append([_dot(jnp.where(causal, _dot_nt(head_k(q_mid, h), head_k(k_mid, h)), 0.0),
                           head(v_c[c], h)) for h in heads])
    sts = [st_ref[h] for h in heads]
    out_rows = []
    for c in range(n_chunk):
        q_c = chunk(q_in, c)
        o = [i_ + _dot_nt(head_k(q_c, h), s_) for h, (i_, s_) in enumerate(zip(intra[c], sts))]
        sts = [s_ * head_k(e_last[c], h) + _dot_tn(head(v_c[c], h), head_k(k_rem[c], h))
               for h, s_ in enumerate(sts)]
        out_rows.append(jnp.concatenate(
            [o_ * lax.rsqrt(jnp.mean(o_ * o_, axis=-1, keepdims=True) + NORM_EPS) for o_ in o], axis=1))
    for h in heads:
        st_ref[h] = sts[h]
    o = jnp.concatenate(out_rows, axis=0)
    o_ref[...] = (o * ng_ref[...] * (g * _sigmoid(g))).astype(o_ref.dtype)


def _gla_call(z, al, ab, ng):
    length, zc = z.shape
    width = ng.shape[1]
    assert length % GLA_BLOCK == 0 and width % GLA_DV == 0, (length, width)
    full = lambda a: pl.BlockSpec(a.shape, lambda i: (0,) * a.ndim)
    return pl.pallas_call(
        _gla_kernel,
        out_shape=jax.ShapeDtypeStruct((length, width), BF16),
        grid=(length // GLA_BLOCK,),
        in_specs=[pl.BlockSpec((GLA_BLOCK, zc), lambda i: (i, 0)), full(al), full(ab), full(ng)],
        out_specs=pl.BlockSpec((GLA_BLOCK, width), lambda i: (i, 0)),
        scratch_shapes=[pltpu.VMEM((width // GLA_DV, GLA_DV, GLA_DK), F32)],
        compiler_params=_params(("arbitrary",)),
        name="gla_mixer",
    )(z, al, ab, ng)


def _merge_kernel(ya_ref, yb_ref, yc_ref, gate_ref, x_ref, wua_ref, wub_ref, wuc_ref, wo_ref, g_ref,
                  xo_ref, h_ref):
    d = x_ref.shape[1]
    gates = gate_ref[...].astype(F32)
    merged = (gates[:, 0:d] * jnp.dot(ya_ref[...], wua_ref[...], preferred_element_type=F32)
              + gates[:, d:2 * d] * jnp.dot(yb_ref[...], wub_ref[...], preferred_element_type=F32)
              + gates[:, 2 * d:3 * d] * jnp.dot(yc_ref[...], wuc_ref[...], preferred_element_type=F32))
    x = x_ref[...] + _dot(merged, wo_ref[...])
    xo_ref[...] = x
    h_ref[...] = _rms_rows(x, g_ref[...]).astype(h_ref.dtype)


def _merge_call(ya, yb, yc, gates, x, wua, wub, wuc, wo, g, tm=256):
    m, d = x.shape
    assert m % tm == 0, (m, tm)
    rows = lambda a: pl.BlockSpec((tm, a.shape[1]), lambda i: (i, 0))
    const = lambda a: pl.BlockSpec(a.shape, lambda i: (0,) * a.ndim, pipeline_mode=pl.Buffered(1))
    return pl.pallas_call(
        _merge_kernel,
        out_shape=(jax.ShapeDtypeStruct((m, d), F32), jax.ShapeDtypeStruct((m, d), BF16)),
        grid=(m // tm,),
        in_specs=[rows(ya), rows(yb), rows(yc), rows(gates), rows(x),
                  const(wua), const(wub), const(wuc), const(wo), const(g)],
        out_specs=(pl.BlockSpec((tm, d), lambda i: (i, 0)), pl.BlockSpec((tm, d), lambda i: (i, 0))),
        compiler_params=_params(("parallel",)),
        name="merge_out_proj",
    )(ya, yb, yc, gates, x, wua, wub, wuc, wo, g)


def _mlp_kernel(final, h_ref, w1_ref, w2_ref, x_ref, g_ref, xo_ref, *rest):
    f = pl.program_id(1)

    @pl.when(f == 0)
    def _():
        xo_ref[...] = x_ref[...]

    hid = jnp.maximum(jnp.dot(h_ref[...], w1_ref[...], preferred_element_type=F32), 0.0)
    xo_ref[...] += _dot(hid * hid, w2_ref[...])

    @pl.when(f == pl.num_programs(1) - 1)
    def _():
        y = _rms_rows(xo_ref[...], g_ref[...])
        if final:
            xo_ref[...] = y
        else:
            rest[0][...] = y.astype(rest[0].dtype)


def _mlp_call(h, w1, w2, x, g_next, final, tm=512, tf=1024):
    m, d = x.shape
    ff = w1.shape[1]
    assert m % tm == 0 and ff % tf == 0, (m, tm, ff, tf)
    row_blk = pl.BlockSpec((tm, d), lambda i, f: (i, 0))
    out_shape = [jax.ShapeDtypeStruct((m, d), F32)]
    out_specs = [row_blk]
    if not final:
        out_shape.append(jax.ShapeDtypeStruct((m, d), BF16))
        out_specs.append(row_blk)
    return pl.pallas_call(
        functools.partial(_mlp_kernel, final),
        out_shape=tuple(out_shape),
        grid=(m // tm, ff // tf),
        in_specs=[row_blk,
                  pl.BlockSpec((d, tf), lambda i, f: (0, f)),
                  pl.BlockSpec((tf, d), lambda i, f: (f, 0)),
                  row_blk,
                  pl.BlockSpec((1, d), lambda i, f: (0, 0))],
        out_specs=tuple(out_specs),
        compiler_params=_params(("parallel", "arbitrary")),
        name="mlp",
    )(h, w1, w2, x, g_next)


def _w_in_slabs_kernel(cuts, w_ref, v_ref, s5_ref, rw_ref, gla_ref, gate_ref):
    c_rw, c_w, c_a, c_gla, c_lora, c_gate = cuts
    x = w_ref[0]
    zeros = lambda n: jnp.zeros((x.shape[0], n), F32)
    s5_ref[0] = x[:, :c_rw].astype(BF16)
    rw_ref[0] = jnp.concatenate(
        [x[:, c_rw:c_w], x[:, c_w:c_a], zeros(LANE - (c_a - c_w)), x[:, c_a:c_gla],
         v_ref[0], zeros(LANE - v_ref.shape[2])], axis=1).astype(BF16)
    gla_ref[0] = jnp.concatenate(
        [x[:, c_gla:c_lora], x[:, c_lora:c_gate], zeros(LANE - (c_gate - c_lora))], axis=1).astype(BF16)
    gate_ref[0] = x[:, c_gate:].astype(BF16)


def _w_in_slabs_call(w_in, vres_a, cuts, tr=256):
    depth, d, cols = w_in.shape
    assert d % tr == 0, (d, tr)
    c_rw, c_w, c_a, c_gla, c_lora, c_gate = cuts
    widths = (c_rw, (c_gla - c_rw) + (LANE - (c_a - c_w)) + LANE, (c_lora - c_gla) + LANE, cols - c_gate)
    blk = lambda n: pl.BlockSpec((1, tr, n), lambda l, i: (l, i, 0))
    return pl.pallas_call(
        functools.partial(_w_in_slabs_kernel, cuts),
        out_shape=tuple(jax.ShapeDtypeStruct((depth, d, n), BF16) for n in widths),
        grid=(depth, d // tr),
        in_specs=[blk(cols), blk(vres_a.shape[2])],
        out_specs=tuple(blk(n) for n in widths),
        compiler_params=_params(("parallel", "parallel")),
        name="w_in_slabs",
    )(w_in, vres_a)


def _pad_cols(a, n):
    return jnp.pad(a, [(0, 0)] * (a.ndim - 1) + [(0, n - a.shape[-1])])


def _pad_rows(a, n):
    return jnp.pad(a, [(0, 0)] * (a.ndim - 2) + [(0, n - a.shape[-2]), (0, 0)])


def kernel(x, norm_mix, w_in, gate_bias, s5_lambda_re, s5_lambda_im, s5_log_step, s5_b_re, s5_b_im, s5_c_re, s5_c_im, s5_d, s5_glu_w, s5_glu_b, rwkv_mu, rwkv_w_lora, rwkv_w0, rwkv_a_lora, rwkv_a0, rwkv_g_lora, rwkv_k_k, rwkv_k_a, rwkv_r_k, rwkv_lnx_w, rwkv_lnx_b, rwkv_vres_a, rwkv_vres_mu, rwkv_vres_b, rwkv_vres_bias, gla_alpha_lora, gla_alpha_bias, gla_norm_g, w_up, w_out, norm_mlp, mlp_w1, mlp_w2, final_norm):
    bsz, length, d = x.shape
    depth = w_in.shape[0]
    s5_w = s5_d.shape[1]
    rw_w = rwkv_w0.shape[1]
    gla_v = gla_norm_g.shape[1]
    gla_heads = gla_v // GLA_DV
    gla_k = gla_heads * GLA_DK
    rw_cols = 3 * rw_w + RWKV_DECAY_LORA + RWKV_AAA_LORA + RWKV_GATE_LORA
    gla_cols = 2 * gla_k + 2 * gla_v + GLA_LORA
    o_rw = s5_w
    o_gla = o_rw + rw_cols
    o_gate = o_gla + gla_cols
    o1 = 3 * rw_w

    head_id = jnp.arange(rw_w) // RWKV_HEAD
    seg = (head_id[:, None] == jnp.arange(LANE)[None, :]).astype(BF16)
    seg_t = seg.T
    rows1 = lambda a: a.reshape(a.shape[0], 1, -1).astype(F32)

    o2 = o1 + RWKV_DECAY_LORA
    vres_a = jnp.concatenate([jnp.zeros((1,) + rwkv_vres_a.shape[1:], F32), rwkv_vres_a.astype(F32)], axis=0)
    vres_mu = jnp.concatenate([jnp.zeros((1,) + rwkv_vres_mu.shape[1:], F32), rwkv_vres_mu.astype(F32)], axis=0)
    w_s5, w_rw, w_gla, w_gate = _w_in_slabs_call(
        w_in, vres_a, (o_rw, o_rw + o1, o_rw + o2, o_gla, o_gla + 2 * gla_k + 2 * gla_v, o_gate))
    mu_rw = rows1(jnp.concatenate([rwkv_mu[:, :o1], _pad_cols(rwkv_mu[:, o1:o2], LANE), rwkv_mu[:, o2:],
                                   _pad_cols(vres_mu, LANE)], axis=1))
    gate_b = rows1(gate_bias)
    s5_tabs = jax.vmap(_s5_tables)(s5_lambda_re, s5_lambda_im, s5_log_step, s5_b_re.astype(F32),
                                   s5_b_im.astype(F32), s5_c_re, s5_c_im)
    s5_dd, s5_gw, s5_gb = rows1(s5_d), s5_glu_w.astype(BF16), rows1(s5_glu_b)
    rw_wl = jax.vmap(_split_weight)(_pad_rows(rwkv_w_lora, LANE))
    rw_vb = _pad_rows(rwkv_vres_b, LANE).astype(BF16)
    rw_rows = [rows1(a) for a in (rwkv_w0, rwkv_a0, rwkv_k_k, rwkv_k_a, rwkv_r_k, rwkv_lnx_w, rwkv_lnx_b)]
    rw_al, rw_gl, rw_vbias = rwkv_a_lora.astype(BF16), rwkv_g_lora.astype(BF16), rows1(rwkv_vres_bias)
    gla_al = jax.vmap(_split_weight)(_pad_rows(gla_alpha_lora, LANE))
    gla_ab = rows1(gla_alpha_bias)
    gla_ng = rows1(gla_norm_g)
    wu_bf, wo_bf = w_up.astype(BF16), w_out.astype(BF16)
    w1_bf, w2_bf = mlp_w1.astype(BF16), mlp_w2.astype(BF16)
    n_mix, n_mlp, n_fin = rows1(norm_mix), rows1(norm_mlp), final_norm.reshape(1, -1).astype(F32)

    outs = []
    for b in range(bsz):
        xb = x[b].astype(F32)
        u = _norm_call(xb, n_mix[0])
        v_first = None
        for l in range(depth):
            z_s5 = _mm_call(u, w_s5[l], tn=s5_w, name="in_proj_s5")
            z_rw = _mm_call(u, w_rw[l], tn=w_rw.shape[2] // 2, name="in_proj_rwkv")
            z_gla = _mm_call(u, w_gla[l], tn=w_gla.shape[2] // 2, name="in_proj_gla")
            gates = _mm_call(u, w_gate[l], tn=1024, bias=gate_b[l], tm=2048, name="in_proj_gate")

            y_a = _s5_call(z_s5, *[t_[l] for t_ in s5_tabs], s5_dd[l], s5_gw[l], s5_gb[l])

            w0, a0, k_k, k_a, r_k, lnx_w, lnx_b = [a[l] for a in rw_rows]
            vb, vbias = (rw_vb[l - 1], rw_vbias[l - 1]) if l > 0 else (None, None)
            res = _rwkv_call(z_rw, v_first, mu_rw[l], rw_wl[l], w0, rw_al[l], a0, rw_gl[l],
                             k_k, k_a, r_k, lnx_w, lnx_b, seg, seg_t, vb, vbias)
            if l == 0:
                y_b, v_first = res
            else:
                y_b = res

            y_c = _gla_call(z_gla, gla_al[l], gla_ab[l], gla_ng[l])

            wu = wu_bf[l]
            x_mid, h = _merge_call(y_a, y_b, y_c, gates, xb,
                                   wu[:s5_w], wu[s5_w:s5_w + rw_w], wu[s5_w + rw_w:],
                                   wo_bf[l], n_mlp[l])
            final = l == depth - 1
            res = _mlp_call(h, w1_bf[l], w2_bf[l], x_mid, n_fin if final else n_mix[l + 1], final)
            if final:
                xb = res[0]
            else:
                xb, u = res
        outs.append(xb.astype(x.dtype))
    return jnp.stack(outs, axis=0)
```

```python
import functools
import math

import jax
import jax.numpy as jnp
from jax import lax
from jax.experimental import pallas as pl
from jax.experimental.pallas import tpu as pltpu

F32 = jnp.float32
BF16 = jnp.bfloat16

LANE = 128
SUBLANE = 8
NORM_EPS = 1e-6

S5_GROUP = 16
S5_STATE = 64
S5_SLAB = 128
S5_SLAB_STATES = (S5_SLAB // S5_GROUP) * S5_STATE
S5_T = 256
S5_BLOCK_LEVELS = 3
S5_FOLD = 4

RWKV_HEAD = 64
RWKV_T = 64
RWKV_BLOCK = 256
RWKV_DECAY_LORA = 96
RWKV_AAA_LORA = 128
RWKV_GATE_LORA = 256
RWKV_LNX_EPS = 64e-5

GLA_DK = 64
GLA_DV = 128
GLA_LORA = 16
GLA_TAU = 16.0
GLA_T = 64
GLA_BLOCK = 256

VMEM_LIMIT = 56 * 1024 * 1024


def _dot(a, b):
    return jnp.dot(a.astype(BF16), b.astype(BF16), preferred_element_type=F32)


def _dot_nt(a, b):
    return lax.dot_general(a.astype(BF16), b.astype(BF16), (((1,), (1,)), ((), ())),
                           preferred_element_type=F32)


def _dot_tn(a, b):
    return lax.dot_general(a.astype(BF16), b.astype(BF16), (((0,), (0,)), ((), ())),
                           preferred_element_type=F32)


def _split3(x):
    hi = x.astype(BF16)
    r1 = x - hi.astype(F32)
    mid = r1.astype(BF16)
    lo = (r1 - mid.astype(F32)).astype(BF16)
    return hi, mid, lo


def _chunk_cumsum(x, t):
    tri = jnp.where(lax.broadcasted_iota(jnp.int32, (t, t), 1) <= lax.broadcasted_iota(jnp.int32, (t, t), 0),
                    1.0, 0.0).astype(BF16)
    parts = _split3(x)
    return jnp.concatenate(
        [sum(jnp.dot(tri, p_[c * t:(c + 1) * t], preferred_element_type=F32) for p_ in parts)
         for c in range(x.shape[0] // t)], axis=0)


def _hi_lo(x):
    hi = x.astype(BF16)
    return hi, (x - hi.astype(F32)).astype(BF16)


def _seg_sum(x, seg, seg_t):
    s = jnp.dot(x.astype(BF16), seg, preferred_element_type=F32)
    hi, lo = _hi_lo(s)
    return jnp.dot(hi, seg_t, preferred_element_type=F32) + jnp.dot(lo, seg_t, preferred_element_type=F32)


def _split_weight(w):
    hi, lo = _hi_lo(w.astype(F32))
    return jnp.stack([hi, lo])


def _dot_hp(a, w_ref):
    a_hi, a_lo = _hi_lo(a)
    return (jnp.dot(a_hi, w_ref[0], preferred_element_type=F32)
            + jnp.dot(a_lo, w_ref[0], preferred_element_type=F32)
            + jnp.dot(a_hi, w_ref[1], preferred_element_type=F32))


def _sigmoid(x):
    return 1.0 / (1.0 + jnp.exp(-x))


def _softplus(x):
    return jnp.maximum(x, 0.0) + jnp.log(1.0 + jnp.exp(-jnp.abs(x)))


def _rms_rows(x, g):
    return x * lax.rsqrt(jnp.mean(x * x, axis=-1, keepdims=True) + NORM_EPS) * g


def _params(sem):
    return pltpu.CompilerParams(dimension_semantics=sem, vmem_limit_bytes=VMEM_LIMIT)


def _norm_kernel(x_ref, g_ref, o_ref):
    o_ref[...] = _rms_rows(x_ref[...], g_ref[...]).astype(o_ref.dtype)


def _norm_call(x, g, tm=512):
    m, d = x.shape
    assert m % tm == 0, (m, tm)
    return pl.pallas_call(
        _norm_kernel,
        out_shape=jax.ShapeDtypeStruct((m, d), BF16),
        grid=(m // tm,),
        in_specs=[pl.BlockSpec((tm, d), lambda i: (i, 0)),
                  pl.BlockSpec((1, d), lambda i: (0, 0))],
        out_specs=pl.BlockSpec((tm, d), lambda i: (i, 0)),
        compiler_params=_params(("parallel",)),
        name="rmsnorm",
    )(x, g)


def _mm_kernel(a_ref, w_ref, o_ref):
    o_ref[...] = jnp.dot(a_ref[...], w_ref[...], preferred_element_type=F32).astype(o_ref.dtype)


def _mm_gate_kernel(a_ref, w_ref, b_ref, o_ref):
    z = jnp.dot(a_ref[...], w_ref[...], preferred_element_type=F32) + b_ref[...]
    o_ref[...] = _sigmoid(z).astype(o_ref.dtype)


def _mm_call(a, w, layer, tn, bias=None, tm=1024, name="in_proj"):
    m, k = a.shape
    n = w.shape[2]
    assert m % tm == 0 and n % tn == 0, (m, tm, n, tn)
    in_specs = [pl.BlockSpec((tm, k), lambda i, j: (i, 0)),
                pl.BlockSpec((None, k, tn), lambda i, j: (layer, 0, j))]
    args = [a, w]
    if bias is None:
        body, out_dtype = _mm_kernel, F32
    else:
        body, out_dtype = _mm_gate_kernel, BF16
        in_specs.append(pl.BlockSpec((1, tn), lambda i, j: (0, j)))
        args.append(bias)
    return pl.pallas_call(
        body,
        out_shape=jax.ShapeDtypeStruct((m, n), out_dtype),
        grid=(m // tm, n // tn),
        in_specs=in_specs,
        out_specs=pl.BlockSpec((tm, tn), lambda i, j: (i, j)),
        compiler_params=_params(("parallel", "parallel")),
        name=name,
    )(*args)


def _s5_kernel(z_ref, bre_ref, bim_ref, pre_ref, pim_ref, cre_ref, cim_ref, d_ref, gw_ref, gb_ref,
               o_ref, carry_re, carry_im):
    n_slab = bre_ref.shape[0]
    tiles_per_slab = S5_SLAB_STATES // LANE
    t = z_ref.shape[0]

    @pl.when(pl.program_id(0) == 0)
    def _():
        carry_re[...] = jnp.zeros_like(carry_re)
        carry_im[...] = jnp.zeros_like(carry_im)

    u = z_ref[...]
    ub = u.astype(BF16)
    n_blk = t // SUBLANE
    row_in_block = lax.broadcasted_iota(jnp.int32, u.shape, 0) & (SUBLANE - 1)
    shifted = [ub] + [jnp.where(row_in_block >= d, pltpu.roll(u, d, 0), 0.0).astype(BF16)
                      for d in range(1, S5_FOLD)]
    ys = []
    for j in range(n_slab):
        uj = jnp.concatenate([s_[:, j * S5_SLAB:(j + 1) * S5_SLAB] for s_ in shifted], axis=1)
        bu_re = jnp.dot(uj, bre_ref[j], preferred_element_type=F32)
        bu_im = jnp.dot(uj, bim_ref[j], preferred_element_type=F32)
        tiles_re = []
        tiles_im = []
        for c in range(tiles_per_slab):
            idx = j * tiles_per_slab + c
            sr = bu_re[:, c * LANE:(c + 1) * LANE].reshape(n_blk, SUBLANE, LANE)
            si = bu_im[:, c * LANE:(c + 1) * LANE].reshape(n_blk, SUBLANE, LANE)
            pr = pre_ref[idx]
            pi = pim_ref[idx]
            for k in range(S5_FOLD.bit_length() - 1, S5_BLOCK_LEVELS):
                qr = pltpu.roll(sr, 1 << k, 1)
                qi = pltpu.roll(si, 1 << k, 1)
                sr, si = sr + (pr[k] * qr - pi[k] * qi), si + (pr[k] * qi + pi[k] * qr)
            cr = jnp.broadcast_to(carry_re[idx], (SUBLANE, LANE))
            ci = jnp.broadcast_to(carry_im[idx], (SUBLANE, LANE))
            ar = pr[S5_BLOCK_LEVELS]
            ai = pi[S5_BLOCK_LEVELS]
            blocks_re = []
            blocks_im = []
            for b in range(n_blk):
                br = sr[b] + (ar * cr - ai * ci)
                bi = si[b] + (ar * ci + ai * cr)
                blocks_re.append(br)
                blocks_im.append(bi)
                cr = jnp.broadcast_to(br[SUBLANE - 1:SUBLANE], (SUBLANE, LANE))
                ci = jnp.broadcast_to(bi[SUBLANE - 1:SUBLANE], (SUBLANE, LANE))
            carry_re[idx] = cr[0:1]
            carry_im[idx] = ci[0:1]
            tiles_re.append(jnp.concatenate(blocks_re, axis=0).astype(BF16))
            tiles_im.append(jnp.concatenate(blocks_im, axis=0).astype(BF16))
        s_re = jnp.concatenate(tiles_re, axis=1)
        s_im = jnp.concatenate(tiles_im, axis=1)
        ys.append(jnp.dot(s_re, cre_ref[j], preferred_element_type=F32)
                  - jnp.dot(s_im, cim_ref[j], preferred_element_type=F32))
    y = jnp.concatenate(ys, axis=1) + d_ref[...] * u
    y = y * (0.5 * (1.0 + jnp.tanh(math.sqrt(2.0 / math.pi) * (y + 0.044715 * (y * y * y)))))
    y = y * _sigmoid(_dot(y, gw_ref[...]) + gb_ref[...])
    o_ref[...] = y.astype(o_ref.dtype)


def _s5_call(z, bre, bim, pre, pim, cre, cim, d, gw, gb):
    length, width = z.shape
    assert length % S5_T == 0 and width % S5_SLAB == 0, (length, width)
    n_slab = bre.shape[0]
    n_tiles = n_slab * (S5_SLAB_STATES // LANE)
    full = lambda a: pl.BlockSpec(a.shape, lambda i: (0,) * a.ndim, pipeline_mode=pl.Buffered(1))
    return pl.pallas_call(
        _s5_kernel,
        out_shape=jax.ShapeDtypeStruct((length, width), BF16),
        grid=(length // S5_T,),
        in_specs=[pl.BlockSpec((S5_T, width), lambda i: (i, 0)),
                  full(bre), full(bim), full(pre), full(pim), full(cre), full(cim),
                  full(d), full(gw), full(gb)],
        out_specs=pl.BlockSpec((S5_T, width), lambda i: (i, 0)),
        scratch_shapes=[pltpu.VMEM((n_tiles, 1, LANE), F32),
                        pltpu.VMEM((n_tiles, 1, LANE), F32)],
        compiler_params=_params(("arbitrary",)),
        name="s5_mixer",
    )(z, bre, bim, pre, pim, cre, cim, d, gw, gb)


def _s5_tables(lam_re, lam_im, log_step, b_re, b_im, c_re, c_im):
    groups = lam_re.shape[0]
    n_slab = groups * S5_GROUP // S5_SLAB
    gps = S5_SLAB // S5_GROUP
    lr = jnp.minimum(lam_re.astype(F32), -1e-4)
    li = lam_im.astype(F32)
    dt = jnp.exp(log_step.astype(F32))[:, None]
    e = jnp.exp(lr * dt)
    lb_re = e * jnp.cos(li * dt)
    lb_im = e * jnp.sin(li * dt)
    den = lr * lr + li * li
    f_re = ((lb_re - 1.0) * lr + lb_im * li) / den
    f_im = (lb_im * lr - (lb_re - 1.0) * li) / den
    bb_re = f_re[..., None] * b_re - f_im[..., None] * b_im
    bb_im = f_re[..., None] * b_im + f_im[..., None] * b_re
    eye = jnp.eye(gps, dtype=F32)

    def bd_b(m):
        m = m.reshape(n_slab, gps, S5_STATE, S5_GROUP)
        return jnp.einsum('jgpi,gh->jgihp', m, eye).reshape(n_slab, S5_SLAB, S5_SLAB_STATES).astype(BF16)

    def bd_c(m):
        m = m.reshape(n_slab, gps, S5_GROUP, S5_STATE)
        return jnp.einsum('jgip,gh->jgphi', m, eye).reshape(n_slab, S5_SLAB_STATES, S5_SLAB).astype(BF16)

    r_idx = jnp.arange(SUBLANE)
    steps = 2 ** jnp.arange(S5_BLOCK_LEVELS)
    expo = jnp.concatenate([jnp.broadcast_to(steps[:, None], (S5_BLOCK_LEVELS, SUBLANE)),
                            (r_idx + 1)[None, :]], axis=0).astype(F32)
    keep = jnp.concatenate([r_idx[None, :] >= steps[:, None],
                            jnp.ones((1, SUBLANE), bool)], axis=0).astype(F32)
    ph = (li * dt).reshape(-1) * expo[..., None]
    mag = keep[..., None] * jnp.exp((lr * dt).reshape(-1) * expo[..., None])
    n_tiles = ph.shape[-1] // LANE
    tile_major = lambda a: a.reshape(a.shape[0], SUBLANE, n_tiles, LANE).transpose(2, 0, 1, 3)
    pw_re = tile_major(mag * jnp.cos(ph))
    pw_im = tile_major(mag * jnp.sin(ph))
    fold = jnp.arange(S5_FOLD, dtype=F32)[:, None, None]
    fd_mag = jnp.exp(lr * dt * fold)
    fd_re = (fd_mag * jnp.cos(li * dt * fold))[..., None]
    fd_im = (fd_mag * jnp.sin(li * dt * fold))[..., None]
    b_stack_re = jnp.concatenate([bd_b(fd_re[d] * bb_re - fd_im[d] * bb_im) for d in range(S5_FOLD)], axis=1)
    b_stack_im = jnp.concatenate([bd_b(fd_re[d] * bb_im + fd_im[d] * bb_re) for d in range(S5_FOLD)], axis=1)
    return (b_stack_re, b_stack_im, pw_re, pw_im,
            bd_c(c_re.astype(F32)), bd_c(c_im.astype(F32)))


def _rwkv_kernel(has_vres, *refs):
    if has_vres:
        (z_ref, vf_ref, mu_ref, wl_ref, w0_ref, al_ref, a0_ref, gl_ref, kk_ref, ka_ref, rk_ref,
         lw_ref, lb_ref, seg_ref, segt_ref, vb_ref, vbias_ref, o_ref, prev_ref, h_ref) = refs
    else:
        (z_ref, mu_ref, wl_ref, w0_ref, al_ref, a0_ref, gl_ref, kk_ref, ka_ref, rk_ref,
         lw_ref, lb_ref, seg_ref, segt_ref, o_ref, vo_ref, prev_ref, h_ref) = refs
    tb = z_ref.shape[0]
    t = RWKV_T
    n_chunk = tb // t
    width = o_ref.shape[1]
    n_pair = width // LANE

    @pl.when(pl.program_id(0) == 0)
    def _():
        prev_ref[...] = jnp.zeros_like(prev_ref)
        h_ref[...] = jnp.zeros_like(h_ref)

    z = z_ref[...]
    rowz = lax.broadcasted_iota(jnp.int32, z.shape, 0)
    prev = jnp.where(rowz == 0, prev_ref[...], pltpu.roll(z, 1, 0))
    prev_ref[...] = z[tb - 1:tb]
    zs = z + (prev - z) * mu_ref[...]

    o1 = 3 * width
    r = zs[:, 0:width]
    k = zs[:, width:2 * width]
    v = zs[:, 2 * width:o1]
    w_in = zs[:, o1:o1 + LANE]
    a_in = zs[:, o1 + LANE:o1 + 2 * LANE]
    g_in = zs[:, o1 + 2 * LANE:o1 + 2 * LANE + RWKV_GATE_LORA]

    wpre = w0_ref[...] + _dot_hp(jnp.tanh(w_in), wl_ref)
    logw = -jnp.exp(-_softplus(-wpre) - 0.5)
    a = _sigmoid(a0_ref[...] + _dot(a_in, al_ref[...]))
    g = _dot(_sigmoid(g_in), gl_ref[...])
    if has_vres:
        vr = zs[:, o1 + 2 * LANE + RWKV_GATE_LORA:]
        vg = _sigmoid(vbias_ref[...] + _dot(vr, vb_ref[...]))
        v = v + (vf_ref[...] - v) * vg
    else:
        vo_ref[...] = v

    seg = seg_ref[...]
    seg_t = segt_ref[...]
    kk = k * kk_ref[...]
    kk = kk * lax.rsqrt(jnp.maximum(_seg_sum(kk * kk, seg, seg_t), 1e-24))
    k2 = k * (1.0 + (a - 1.0) * ka_ref[...])
    kka = kk * a

    shift = t.bit_length() - 1
    lc = _chunk_cumsum(logw, t)

    gi = lax.broadcasted_iota(jnp.int32, (4 * t, 4 * t), 0)
    gj = lax.broadcasted_iota(jnp.int32, (4 * t, 4 * t), 1)
    ti = gi & (t - 1)
    sj = gj & (t - 1)
    keep = sj + jnp.where(gi < 2 * t, 1, 0) <= ti
    lane = lax.broadcasted_iota(jnp.int32, (t, LANE), 1)
    head0 = lane < RWKV_HEAD

    def stack2(x):
        return jnp.concatenate([jnp.where(head0, x, 0.0), jnp.where(head0, 0.0, x)], axis=0)

    e_pos = jnp.exp(lc)
    e_neg = jnp.exp(-lc)
    e_prev = jnp.exp(lc - logw)
    rt_all = r * e_pos
    at_all = -kk * e_prev
    bt_all = kka * e_neg
    kt_all = k2 * e_neg

    items = [(c, p) for c in range(n_chunk) for p in range(n_pair)]
    tile = lambda x, c, p: x[c * t:(c + 1) * t, p * LANE:(p + 1) * LANE]
    at = [tile(at_all, c, p) for c, p in items]
    rt = [tile(rt_all, c, p) for c, p in items]
    vbd = [stack2(tile(v, c, p)) for c, p in items]
    gm = [jnp.where(keep, _dot_nt(jnp.concatenate([stack2(a_), stack2(r_)], axis=0),
                                  jnp.concatenate([stack2(tile(bt_all, c, p)), stack2(tile(kt_all, c, p))],
                                                  axis=0)), 0.0)
          for a_, r_, (c, p) in zip(at, rt, items)]
    n1 = [g_[0:2 * t, 0:2 * t] for g_ in gm]
    rhs0 = [_dot(g_[0:2 * t, 2 * t:4 * t], v_) for g_, v_ in zip(gm, vbd)]
    qi = lax.broadcasted_iota(jnp.int32, (2 * t, 2 * t), 0)
    qj = lax.broadcasted_iota(jnp.int32, (2 * t, 2 * t), 1)
    xinv = [jnp.where(qi == qj, 1.0, jnp.where((qi ^ qj) == 1, n_, 0.0)) for n_ in n1]
    for lvl in range(1, shift):
        couple = ((qi >> lvl) ^ (qj >> lvl)) == 1
        xm = [_dot(x_, jnp.where(couple, n_, 0.0)) for x_, n_ in zip(xinv, n1)]
        xinv = [x_ + _dot(m_, x_) for x_, m_ in zip(xinv, xm)]

    hts = [h_ref[p] for p in range(n_pair)]
    out_rows = []
    for c in range(n_chunk):
        idx = [c * n_pair + p for p in range(n_pair)]
        lc_c = lc[c * t:(c + 1) * t]
        e_rem = jnp.exp(lc_c[t - 1:t] - lc_c)
        bh_c = kka[c * t:(c + 1) * t] * e_rem
        kh_c = k2[c * t:(c + 1) * t] * e_rem
        e_last = e_pos[(c + 1) * t - 1:(c + 1) * t]
        ph = [_dot_nt(jnp.concatenate([at[i], rt[i]], axis=0), h_) for i, h_ in zip(idx, hts)]
        u = [_dot(xinv[i], stack2(p_[0:t]) + rhs0[i]) for i, p_ in zip(idx, ph)]
        uv = [jnp.concatenate([u_, vbd[i]], axis=0) for i, u_ in zip(idx, u)]
        opk = [stack2(p_[t:2 * t]) + _dot(gm[i][2 * t:4 * t, :], uv_) for i, p_, uv_ in zip(idx, ph, uv)]
        out_rows.append(jnp.concatenate([o_[0:t] + o_[t:2 * t] for o_ in opk], axis=1))
        hts = [h_ * e_last[:, p * LANE:(p + 1) * LANE]
               + _dot_tn(uv_, jnp.concatenate([stack2(bh_c[:, p * LANE:(p + 1) * LANE]),
                                               stack2(kh_c[:, p * LANE:(p + 1) * LANE])], axis=0))
               for p, (h_, uv_) in enumerate(zip(hts, uv))]
    for p in range(n_pair):
        h_ref[p] = hts[p]

    y = jnp.concatenate(out_rows, axis=0)
    inv_n = 1.0 / RWKV_HEAD
    mean = _seg_sum(y, seg, seg_t) * inv_n
    yc = y - mean
    var = _seg_sum(yc * yc, seg, seg_t) * inv_n
    yn = yc * lax.rsqrt(var + RWKV_LNX_EPS) * lw_ref[...] + lb_ref[...]
    bonus = _seg_sum(r * k2 * rk_ref[...], seg, seg_t) * v
    o_ref[...] = ((yn + bonus) * g).astype(o_ref.dtype)


def _rwkv_call(z, v_first, mu, wl, w0, al, a0, gl, k_k, k_a, r_k, lnx_w, lnx_b, seg, seg_t, vb, vbias):
    length, zc = z.shape
    width = w0.shape[1]
    assert length % RWKV_BLOCK == 0 and width % (2 * RWKV_HEAD) == 0, (length, width)
    has_vres = v_first is not None
    full = lambda a: pl.BlockSpec(a.shape, lambda i: (0,) * a.ndim)
    rows = lambda c: pl.BlockSpec((RWKV_BLOCK, c), lambda i: (i, 0))
    common = [mu, wl, w0, al, a0, gl, k_k, k_a, r_k, lnx_w, lnx_b, seg, seg_t]
    if has_vres:
        args = [z, v_first] + common + [vb, vbias]
        in_specs = [rows(zc), rows(width)] + [full(a) for a in common + [vb, vbias]]
        out_shape = jax.ShapeDtypeStruct((length, width), BF16)
        out_specs = rows(width)
    else:
        args = [z] + common
        in_specs = [rows(zc)] + [full(a) for a in common]
        out_shape = (jax.ShapeDtypeStruct((length, width), BF16),
                     jax.ShapeDtypeStruct((length, width), F32))
        out_specs = (rows(width), rows(width))
    return pl.pallas_call(
        functools.partial(_rwkv_kernel, has_vres),
        out_shape=out_shape,
        grid=(length // RWKV_BLOCK,),
        in_specs=in_specs,
        out_specs=out_specs,
        scratch_shapes=[pltpu.VMEM((1, zc), F32),
                        pltpu.VMEM((width // LANE, LANE, LANE), F32)],
        compiler_params=_params(("arbitrary",)),
        name="rwkv7_mixer",
    )(*args)


def _gla_kernel(z_ref, al_ref, ab_ref, ng_ref, o_ref, st_ref):
    tb = z_ref.shape[0]
    t = GLA_T
    n_chunk = tb // t
    width = o_ref.shape[1]
    n_head = width // GLA_DV

    @pl.when(pl.program_id(0) == 0)
    def _():
        st_ref[...] = jnp.zeros_like(st_ref)

    z = z_ref[...]
    kw = n_head * GLA_DK
    q = z[:, 0:kw] * (GLA_DK ** -0.5)
    k = z[:, kw:2 * kw]
    v = z[:, 2 * kw:2 * kw + width]
    g = z[:, 2 * kw + width:2 * kw + 2 * width]
    a_in = z[:, 2 * kw + 2 * width:]
    x = _dot_hp(a_in, al_ref) + ab_ref[...]
    log_a = -_softplus(-x) * (1.0 / GLA_TAU)

    b = _chunk_cumsum(log_a, t)
    causal =(lax.broadcasted_iota(jnp.int32, (t, t), 1) <= lax.broadcasted_iota(jnp.int32, (t, t), 0))

    q_in = q * jnp.exp(b)
    heads = range(n_head)
    head = lambda x, h: x[:, h * GLA_DV:(h + 1) * GLA_DV]
    head_k = lambda x, h: x[:, h * GLA_DK:(h + 1) * GLA_DK]
    chunk = lambda x, c: x[c * t:(c + 1) * t]
    v_c = [chunk(v, c) for c in range(n_chunk)]
    k_rem = []
    e_last = []
    intra = []
    for c in range(n_chunk):
        b_c = chunk(b, c)
        b_mid = b_c[t // 2:t // 2 + 1]
        b_last = b_c[t - 1:t]
        q_mid = chunk(q, c) * jnp.exp(b_c - b_mid)
        k_mid = chunk(k, c) * jnp.exp(b_mid - b_c)
        k_rem.append(chunk(k, c) * jnp.exp(b_last - b_c))
        e_last.append(jnp.exp(b_last))
        intra.append([_dot(jnp.where(causal, _dot_nt(head_k(q_mid, h), head_k(k_mid, h)), 0.0),
                           head(v_c[c], h)) for h in heads])
    sts = [st_ref[h] for h in heads]
    out_rows = []
    for c in range(n_chunk):
        q_c = chunk(q_in, c)
        o = [i_ + _dot_nt(head_k(q_c, h), s_) for h, (i_, s_) in enumerate(zip(intra[c], sts))]
        sts = [s_ * head_k(e_last[c], h) + _dot_tn(head(v_c[c], h), head_k(k_rem[c], h))
               for h, s_ in enumerate(sts)]
        out_rows.append(jnp.concatenate(
            [o_ * lax.rsqrt(jnp.mean(o_ * o_, axis=-1, keepdims=True) + NORM_EPS) for o_ in o], axis=1))
    for h in heads:
        st_ref[h] = sts[h]
    o = jnp.concatenate(out_rows, axis=0)
    o_ref[...] = (o * ng_ref[...] * (g * _sigmoid(g))).astype(o_ref.dtype)


def _gla_call(z, al, ab, ng):
    length, zc = z.shape
    width = ng.shape[1]
    assert length % GLA_BLOCK == 0 and width % GLA_DV == 0, (length, width)
    full = lambda a: pl.BlockSpec(a.shape, lambda i: (0,) * a.ndim)
    return pl.pallas_call(
        _gla_kernel,
        out_shape=jax.ShapeDtypeStruct((length, width), BF16),
        grid=(length // GLA_BLOCK,),
        in_specs=[pl.BlockSpec((GLA_BLOCK, zc), lambda i: (i, 0)), full(al), full(ab), full(ng)],
        out_specs=pl.BlockSpec((GLA_BLOCK, width), lambda i: (i, 0)),
        scratch_shapes=[pltpu.VMEM((width // GLA_DV, GLA_DV, GLA_DK), F32)],
        compiler_params=_params(("arbitrary",)),
        name="gla_mixer",
    )(z, al, ab, ng)


def _merge_kernel(ya_ref, yb_ref, yc_ref, gate_ref, x_ref, wua_ref, wub_ref, wuc_ref, wo_ref, g_ref,
                  xo_ref, h_ref):
    d = x_ref.shape[1]
    gates = gate_ref[...].astype(F32)
    merged = (gates[:, 0:d] * jnp.dot(ya_ref[...], wua_ref[...], preferred_element_type=F32)
              + gates[:, d:2 * d] * jnp.dot(yb_ref[...], wub_ref[...], preferred_element_type=F32)
              + gates[:, 2 * d:3 * d] * jnp.dot(yc_ref[...], wuc_ref[...], preferred_element_type=F32))
    x = x_ref[...] + _dot(merged, wo_ref[...])
    xo_ref[...] = x
    h_ref[...] = _rms_rows(x, g_ref[...]).astype(h_ref.dtype)


def _merge_call(ya, yb, yc, gates, x, wua, wub, wuc, wo, layer, g, tm=256):
    m, d = x.shape
    assert m % tm == 0, (m, tm)
    rows = lambda a: pl.BlockSpec((tm, a.shape[1]), lambda i: (i, 0))
    const = lambda a: pl.BlockSpec(a.shape, lambda i: (0,) * a.ndim, pipeline_mode=pl.Buffered(1))
    wo_spec = pl.BlockSpec((None, d, d), lambda i: (layer, 0, 0), pipeline_mode=pl.Buffered(1))
    return pl.pallas_call(
        _merge_kernel,
        out_shape=(jax.ShapeDtypeStruct((m, d), F32), jax.ShapeDtypeStruct((m, d), BF16)),
        grid=(m // tm,),
        in_specs=[rows(ya), rows(yb), rows(yc), rows(gates), rows(x),
                  const(wua), const(wub), const(wuc), wo_spec, const(g)],
        out_specs=(pl.BlockSpec((tm, d), lambda i: (i, 0)), pl.BlockSpec((tm, d), lambda i: (i, 0))),
        compiler_params=_params(("parallel",)),
        name="merge_out_proj",
    )(ya, yb, yc, gates, x, wua, wub, wuc, wo, g)


def _mlp_kernel(final, h_ref, w1_ref, w2_ref, x_ref, g_ref, xo_ref, *rest):
    f = pl.program_id(1)

    @pl.when(f == 0)
    def _():
        xo_ref[...] = x_ref[...]

    hid = jnp.maximum(jnp.dot(h_ref[...], w1_ref[...], preferred_element_type=F32), 0.0)
    xo_ref[...] += _dot(hid * hid, w2_ref[...])

    @pl.when(f == pl.num_programs(1) - 1)
    def _():
        y = _rms_rows(xo_ref[...], g_ref[...])
        if final:
            xo_ref[...] = y
        else:
            rest[0][...] = y.astype(rest[0].dtype)


def _mlp_call(h, w1, w2, layer, x, g_next, final, tm=512, tf=1024):
    m, d = x.shape
    ff = w1.shape[2]
    assert m % tm == 0 and ff % tf == 0, (m, tm, ff, tf)
    row_blk = pl.BlockSpec((tm, d), lambda i, f: (i, 0))
    out_shape = [jax.ShapeDtypeStruct((m, d), F32)]
    out_specs = [row_blk]
    if not final:
        out_shape.append(jax.ShapeDtypeStruct((m, d), BF16))
        out_specs.append(row_blk)
    return pl.pallas_call(
        functools.partial(_mlp_kernel, final),
        out_shape=tuple(out_shape),
        grid=(m // tm, ff // tf),
        in_specs=[row_blk,
                  pl.BlockSpec((None, d, tf), lambda i, f: (layer, 0, f)),
                  pl.BlockSpec((None, tf, d), lambda i, f: (layer, f, 0)),
                  row_blk,
                  pl.BlockSpec((1, d), lambda i, f: (0, 0))],
        out_specs=tuple(out_specs),
        compiler_params=_params(("parallel", "arbitrary")),
        name="mlp",
    )(h, w1, w2, x, g_next)


def _pad_cols(a, n):
    return jnp.pad(a, [(0, 0)] * (a.ndim - 1) + [(0, n - a.shape[-1])])


def _pad_rows(a, n):
    return jnp.pad(a, [(0, 0)] * (a.ndim - 2) + [(0, n - a.shape[-2]), (0, 0)])


def kernel(x, norm_mix, w_in, gate_bias, s5_lambda_re, s5_lambda_im, s5_log_step, s5_b_re, s5_b_im, s5_c_re, s5_c_im, s5_d, s5_glu_w, s5_glu_b, rwkv_mu, rwkv_w_lora, rwkv_w0, rwkv_a_lora, rwkv_a0, rwkv_g_lora, rwkv_k_k, rwkv_k_a, rwkv_r_k, rwkv_lnx_w, rwkv_lnx_b, rwkv_vres_a, rwkv_vres_mu, rwkv_vres_b, rwkv_vres_bias, gla_alpha_lora, gla_alpha_bias, gla_norm_g, w_up, w_out, norm_mlp, mlp_w1, mlp_w2, final_norm):
    bsz, length, d = x.shape
    depth = w_in.shape[0]
    s5_w = s5_d.shape[1]
    rw_w = rwkv_w0.shape[1]
    gla_v = gla_norm_g.shape[1]
    gla_heads = gla_v // GLA_DV
    gla_k = gla_heads * GLA_DK
    rw_cols = 3 * rw_w + RWKV_DECAY_LORA + RWKV_AAA_LORA + RWKV_GATE_LORA
    gla_cols = 2 * gla_k + 2 * gla_v + GLA_LORA
    o_rw = s5_w
    o_gla = o_rw + rw_cols
    o_gate = o_gla + gla_cols
    o1 = 3 * rw_w

    head_id = jnp.arange(rw_w) // RWKV_HEAD
    seg = (head_id[:, None] == jnp.arange(LANE)[None, :]).astype(BF16)
    seg_t = seg.T
    rows1 = lambda a: a.reshape(a.shape[0], 1, -1).astype(F32)

    w_in_bf = w_in.astype(BF16)
    w_s5 = w_in_bf[:, :, :o_rw]
    wr = w_in_bf[:, :, o_rw:o_gla]
    o2 = o1 + RWKV_DECAY_LORA
    vres_a = jnp.concatenate([jnp.zeros((1,) + rwkv_vres_a.shape[1:], BF16), rwkv_vres_a.astype(BF16)], axis=0)
    vres_mu = jnp.concatenate([jnp.zeros((1,) + rwkv_vres_mu.shape[1:], F32), rwkv_vres_mu.astype(F32)], axis=0)
    w_rw = jnp.concatenate([wr[:, :, :o1], _pad_cols(wr[:, :, o1:o2], LANE), wr[:, :, o2:],
                            _pad_cols(vres_a, LANE)], axis=2)
    wg = w_in_bf[:, :, o_gla:o_gate]
    w_gla = jnp.concatenate([wg[:, :, :2 * gla_k + 2 * gla_v],
                             _pad_cols(wg[:, :, 2 * gla_k + 2 * gla_v:], LANE)], axis=2)
    w_gate = w_in_bf[:, :, o_gate:]
    mu_rw = rows1(jnp.concatenate([rwkv_mu[:, :o1], _pad_cols(rwkv_mu[:, o1:o2], LANE), rwkv_mu[:, o2:],
                                   _pad_cols(vres_mu, LANE)], axis=1))
    gate_b = rows1(gate_bias)
    s5_tabs = jax.vmap(_s5_tables)(s5_lambda_re, s5_lambda_im, s5_log_step, s5_b_re.astype(F32),
                                   s5_b_im.astype(F32), s5_c_re, s5_c_im)
    s5_dd, s5_gw, s5_gb = rows1(s5_d), s5_glu_w.astype(BF16), rows1(s5_glu_b)
    rw_wl = jax.vmap(_split_weight)(_pad_rows(rwkv_w_lora, LANE))
    rw_vb = _pad_rows(rwkv_vres_b, LANE).astype(BF16)
    rw_rows = [rows1(a) for a in (rwkv_w0, rwkv_a0, rwkv_k_k, rwkv_k_a, rwkv_r_k, rwkv_lnx_w, rwkv_lnx_b)]
    rw_al, rw_gl, rw_vbias = rwkv_a_lora.astype(BF16), rwkv_g_lora.astype(BF16), rows1(rwkv_vres_bias)
    gla_al = jax.vmap(_split_weight)(_pad_rows(gla_alpha_lora, LANE))
    gla_ab = rows1(gla_alpha_bias)
    gla_ng = rows1(gla_norm_g)
    wu_bf, wo_bf = w_up.astype(BF16), w_out.astype(BF16)
    w1_bf, w2_bf = mlp_w1.astype(BF16), mlp_w2.astype(BF16)
    n_mix, n_mlp, n_fin = rows1(norm_mix), rows1(norm_mlp), final_norm.reshape(1, -1).astype(F32)

    outs = []
    for b in range(bsz):
        xb = x[b].astype(F32)
        u = _norm_call(xb, n_mix[0])
        v_first = None
        for l in range(depth):
            z_s5 = _mm_call(u, w_s5, l, tn=s5_w, name="in_proj_s5")
            z_rw = _mm_call(u, w_rw, l, tn=w_rw.shape[2] // 2, name="in_proj_rwkv")
            z_gla = _mm_call(u, w_gla, l, tn=w_gla.shape[2] // 2, name="in_proj_gla")
            gates = _mm_call(u, w_gate, l, tn=1024, bias=gate_b[l], tm=2048, name="in_proj_gate")

            y_a = _s5_call(z_s5, *[t_[l] for t_ in s5_tabs], s5_dd[l], s5_gw[l], s5_gb[l])

            w0, a0, k_k, k_a, r_k, lnx_w, lnx_b = [a[l] for a in rw_rows]
            vb, vbias = (rw_vb[l - 1], rw_vbias[l - 1]) if l > 0 else (None, None)
            res = _rwkv_call(z_rw, v_first, mu_rw[l], rw_wl[l], w0, rw_al[l], a0, rw_gl[l],
                             k_k, k_a, r_k, lnx_w, lnx_b, seg, seg_t, vb, vbias)
            if l == 0:
                y_b, v_first = res
            else:
                y_b = res

            y_c = _gla_call(z_gla, gla_al[l], gla_ab[l], gla_ng[l])

            wu = wu_bf[l]
            x_mid, h = _merge_call(y_a, y_b, y_c, gates, xb,
                                   wu[:s5_w], wu[s5_w:s5_w + rw_w], wu[s5_w + rw_w:],
                                   wo_bf, l, n_mlp[l])
            final = l == depth - 1
            res = _mlp_call(h, w1_bf, w2_bf, l, x_mid, n_fin if final else n_mix[l + 1], final)
            if final:
                xb = res[0]
            else:
                xb, u = res
        outs.append(xb.astype(x.dtype))
    return jnp.stack(outs, axis=0)
```

```python
import functools
import math

import jax
import jax.numpy as jnp
from jax import lax
from jax.experimental import pallas as pl
from jax.experimental.pallas import tpu as pltpu

F32 = jnp.float32
BF16 = jnp.bfloat16

LANE = 128
SUBLANE = 8
NORM_EPS = 1e-6

S5_GROUP = 16
S5_STATE = 64
S5_SLAB = 128
S5_SLAB_STATES = (S5_SLAB // S5_GROUP) * S5_STATE
S5_T = 256
S5_BLOCK_LEVELS = 3
S5_FOLD = 4

RWKV_HEAD = 64
RWKV_T = 64
RWKV_BLOCK = 256
RWKV_DECAY_LORA = 96
RWKV_AAA_LORA = 128
RWKV_GATE_LORA = 256
RWKV_LNX_EPS = 64e-5

GLA_DK = 64
GLA_DV = 128
GLA_LORA = 16
GLA_TAU = 16.0
GLA_T = 64
GLA_BLOCK = 512

VMEM_LIMIT = 56 * 1024 * 1024


def _dot(a, b):
    return jnp.dot(a.astype(BF16), b.astype(BF16), preferred_element_type=F32)


def _dot_nt(a, b):
    return lax.dot_general(a.astype(BF16), b.astype(BF16), (((1,), (1,)), ((), ())),
                           preferred_element_type=F32)


def _dot_tn(a, b):
    return lax.dot_general(a.astype(BF16), b.astype(BF16), (((0,), (0,)), ((), ())),
                           preferred_element_type=F32)


def _split3(x):
    hi = x.astype(BF16)
    r1 = x - hi.astype(F32)
    mid = r1.astype(BF16)
    lo = (r1 - mid.astype(F32)).astype(BF16)
    return hi, mid, lo


def _chunk_cumsum(x, t):
    tri = jnp.where(lax.broadcasted_iota(jnp.int32, (t, t), 1) <= lax.broadcasted_iota(jnp.int32, (t, t), 0),
                    1.0, 0.0).astype(BF16)
    parts = _split3(x)
    return jnp.concatenate(
        [sum(jnp.dot(tri, p_[c * t:(c + 1) * t], preferred_element_type=F32) for p_ in parts)
         for c in range(x.shape[0] // t)], axis=0)


def _hi_lo(x):
    hi = x.astype(BF16)
    return hi, (x - hi.astype(F32)).astype(BF16)


def _seg_sum(x, seg, seg_t):
    s = jnp.dot(x.astype(BF16), seg, preferred_element_type=F32)
    hi, lo = _hi_lo(s)
    return jnp.dot(hi, seg_t, preferred_element_type=F32) + jnp.dot(lo, seg_t, preferred_element_type=F32)


def _split_weight(w):
    hi, lo = _hi_lo(w.astype(F32))
    return jnp.stack([hi, lo])


def _dot_hp(a, w_ref):
    a_hi, a_lo = _hi_lo(a)
    return (jnp.dot(a_hi, w_ref[0], preferred_element_type=F32)
            + jnp.dot(a_lo, w_ref[0], preferred_element_type=F32)
            + jnp.dot(a_hi, w_ref[1], preferred_element_type=F32))


def _sigmoid(x):
    return 1.0 / (1.0 + jnp.exp(-x))


def _softplus(x):
    return jnp.maximum(x, 0.0) + jnp.log(1.0 + jnp.exp(-jnp.abs(x)))


def _rms_rows(x, g):
    return x * lax.rsqrt(jnp.mean(x * x, axis=-1, keepdims=True) + NORM_EPS) * g


def _params(sem):
    return pltpu.CompilerParams(dimension_semantics=sem, vmem_limit_bytes=VMEM_LIMIT)


def _norm_kernel(x_ref, g_ref, o_ref):
    o_ref[...] = _rms_rows(x_ref[...], g_ref[...]).astype(o_ref.dtype)


def _norm_call(x, g, tm=512):
    m, d = x.shape
    assert m % tm == 0, (m, tm)
    return pl.pallas_call(
        _norm_kernel,
        out_shape=jax.ShapeDtypeStruct((m, d), BF16),
        grid=(m // tm,),
        in_specs=[pl.BlockSpec((tm, d), lambda i: (i, 0)),
                  pl.BlockSpec((1, d), lambda i: (0, 0))],
        out_specs=pl.BlockSpec((tm, d), lambda i: (i, 0)),
        compiler_params=_params(("parallel",)),
        name="rmsnorm",
    )(x, g)


def _mm_kernel(a_ref, w_ref, o_ref):
    o_ref[...] = jnp.dot(a_ref[...], w_ref[...], preferred_element_type=F32).astype(o_ref.dtype)


def _mm_gate_kernel(a_ref, w_ref, b_ref, o_ref):
    z = jnp.dot(a_ref[...], w_ref[...], preferred_element_type=F32) + b_ref[...]
    o_ref[...] = _sigmoid(z).astype(o_ref.dtype)


def _mm_call(a, w, layer, tn, bias=None, tm=1024, name="in_proj"):
    m, k = a.shape
    n = w.shape[2]
    assert m % tm == 0 and n % tn == 0, (m, tm, n, tn)
    in_specs = [pl.BlockSpec((tm, k), lambda i, j: (i, 0)),
                pl.BlockSpec((None, k, tn), lambda i, j: (layer, 0, j))]
    args = [a, w]
    if bias is None:
        body, out_dtype = _mm_kernel, F32
    else:
        body, out_dtype = _mm_gate_kernel, BF16
        in_specs.append(pl.BlockSpec((1, tn), lambda i, j: (0, j)))
        args.append(bias)
    return pl.pallas_call(
        body,
        out_shape=jax.ShapeDtypeStruct((m, n), out_dtype),
        grid=(m // tm, n // tn),
        in_specs=in_specs,
        out_specs=pl.BlockSpec((tm, tn), lambda i, j: (i, j)),
        compiler_params=_params(("parallel", "parallel")),
        name=name,
    )(*args)


def _s5_kernel(z_ref, bre_ref, bim_ref, pre_ref, pim_ref, cre_ref, cim_ref, d_ref, gw_ref, gb_ref,
               o_ref, carry_re, carry_im):
    n_slab = bre_ref.shape[0]
    tiles_per_slab = S5_SLAB_STATES // LANE
    t = z_ref.shape[0]

    @pl.when(pl.program_id(0) == 0)
    def _():
        carry_re[...] = jnp.zeros_like(carry_re)
        carry_im[...] = jnp.zeros_like(carry_im)

    u = z_ref[...]
    ub = u.astype(BF16)
    n_blk = t // SUBLANE
    row_in_block = lax.broadcasted_iota(jnp.int32, u.shape, 0) & (SUBLANE - 1)
    shifted = [ub] + [jnp.where(row_in_block >= d, pltpu.roll(u, d, 0), 0.0).astype(BF16)
                      for d in range(1, S5_FOLD)]
    ys = []
    for j in range(n_slab):
        uj = jnp.concatenate([s_[:, j * S5_SLAB:(j + 1) * S5_SLAB] for s_ in shifted], axis=1)
        bu_re = jnp.dot(uj, bre_ref[j], preferred_element_type=F32)
        bu_im = jnp.dot(uj, bim_ref[j], preferred_element_type=F32)
        tiles_re = []
        tiles_im = []
        for c in range(tiles_per_slab):
            idx = j * tiles_per_slab + c
            sr = bu_re[:, c * LANE:(c + 1) * LANE].reshape(n_blk, SUBLANE, LANE)
            si = bu_im[:, c * LANE:(c + 1) * LANE].reshape(n_blk, SUBLANE, LANE)
            pr = pre_ref[idx]
            pi = pim_ref[idx]
            for k in range(S5_FOLD.bit_length() - 1, S5_BLOCK_LEVELS):
                qr = pltpu.roll(sr, 1 << k, 1)
                qi = pltpu.roll(si, 1 << k, 1)
                sr, si = sr + (pr[k] * qr - pi[k] * qi), si + (pr[k] * qi + pi[k] * qr)
            cr = jnp.broadcast_to(carry_re[idx], (SUBLANE, LANE))
            ci = jnp.broadcast_to(carry_im[idx], (SUBLANE, LANE))
            ar = pr[S5_BLOCK_LEVELS]
            ai = pi[S5_BLOCK_LEVELS]
            blocks_re = []
            blocks_im = []
            for b in range(n_blk):
                br = sr[b] + (ar * cr - ai * ci)
                bi = si[b] + (ar * ci + ai * cr)
                blocks_re.append(br)
                blocks_im.append(bi)
                cr = jnp.broadcast_to(br[SUBLANE - 1:SUBLANE], (SUBLANE, LANE))
                ci = jnp.broadcast_to(bi[SUBLANE - 1:SUBLANE], (SUBLANE, LANE))
            carry_re[idx] = cr[0:1]
            carry_im[idx] = ci[0:1]
            tiles_re.append(jnp.concatenate(blocks_re, axis=0).astype(BF16))
            tiles_im.append(jnp.concatenate(blocks_im, axis=0).astype(BF16))
        s_re = jnp.concatenate(tiles_re, axis=1)
        s_im = jnp.concatenate(tiles_im, axis=1)
        ys.append(jnp.dot(s_re, cre_ref[j], preferred_element_type=F32)
                  - jnp.dot(s_im, cim_ref[j], preferred_element_type=F32))
    y = jnp.concatenate(ys, axis=1) + d_ref[...] * u
    y = y * (0.5 * (1.0 + jnp.tanh(math.sqrt(2.0 / math.pi) * (y + 0.044715 * (y * y * y)))))
    y = y * _sigmoid(_dot(y, gw_ref[...]) + gb_ref[...])
    o_ref[...] = y.astype(o_ref.dtype)


def _s5_call(z, bre, bim, pre, pim, cre, cim, d, gw, gb):
    length, width = z.shape
    assert length % S5_T == 0 and width % S5_SLAB == 0, (length, width)
    n_slab = bre.shape[0]
    n_tiles = n_slab * (S5_SLAB_STATES // LANE)
    full = lambda a: pl.BlockSpec(a.shape, lambda i: (0,) * a.ndim, pipeline_mode=pl.Buffered(1))
    return pl.pallas_call(
        _s5_kernel,
        out_shape=jax.ShapeDtypeStruct((length, width), BF16),
        grid=(length // S5_T,),
        in_specs=[pl.BlockSpec((S5_T, width), lambda i: (i, 0)),
                  full(bre), full(bim), full(pre), full(pim), full(cre), full(cim),
                  full(d), full(gw), full(gb)],
        out_specs=pl.BlockSpec((S5_T, width), lambda i: (i, 0)),
        scratch_shapes=[pltpu.VMEM((n_tiles, 1, LANE), F32),
                        pltpu.VMEM((n_tiles, 1, LANE), F32)],
        compiler_params=_params(("arbitrary",)),
        name="s5_mixer",
    )(z, bre, bim, pre, pim, cre, cim, d, gw, gb)


def _s5_tables(lam_re, lam_im, log_step, b_re, b_im, c_re, c_im):
    groups = lam_re.shape[0]
    n_slab = groups * S5_GROUP // S5_SLAB
    gps = S5_SLAB // S5_GROUP
    lr = jnp.minimum(lam_re.astype(F32), -1e-4)
    li = lam_im.astype(F32)
    dt = jnp.exp(log_step.astype(F32))[:, None]
    e = jnp.exp(lr * dt)
    lb_re = e * jnp.cos(li * dt)
    lb_im = e * jnp.sin(li * dt)
    den = lr * lr + li * li
    f_re = ((lb_re - 1.0) * lr + lb_im * li) / den
    f_im = (lb_im * lr - (lb_re - 1.0) * li) / den
    bb_re = f_re[..., None] * b_re - f_im[..., None] * b_im
    bb_im = f_re[..., None] * b_im + f_im[..., None] * b_re
    eye = jnp.eye(gps, dtype=F32)

    def bd_b(m):
        m = m.reshape(n_slab, gps, S5_STATE, S5_GROUP)
        return jnp.einsum('jgpi,gh->jgihp', m, eye).reshape(n_slab, S5_SLAB, S5_SLAB_STATES).astype(BF16)

    def bd_c(m):
        m = m.reshape(n_slab, gps, S5_GROUP, S5_STATE)
        return jnp.einsum('jgip,gh->jgphi', m, eye).reshape(n_slab, S5_SLAB_STATES, S5_SLAB).astype(BF16)

    r_idx = jnp.arange(SUBLANE)
    steps = 2 ** jnp.arange(S5_BLOCK_LEVELS)
    expo = jnp.concatenate([jnp.broadcast_to(steps[:, None], (S5_BLOCK_LEVELS, SUBLANE)),
                            (r_idx + 1)[None, :]], axis=0).astype(F32)
    keep = jnp.concatenate([r_idx[None, :] >= steps[:, None],
                            jnp.ones((1, SUBLANE), bool)], axis=0).astype(F32)
    ph = (li * dt).reshape(-1) * expo[..., None]
    mag = keep[..., None] * jnp.exp((lr * dt).reshape(-1) * expo[..., None])
    n_tiles = ph.shape[-1] // LANE
    tile_major = lambda a: a.reshape(a.shape[0], SUBLANE, n_tiles, LANE).transpose(2, 0, 1, 3)
    pw_re = tile_major(mag * jnp.cos(ph))
    pw_im = tile_major(mag * jnp.sin(ph))
    fold = jnp.arange(S5_FOLD, dtype=F32)[:, None, None]
    fd_mag = jnp.exp(lr * dt * fold)
    fd_re = (fd_mag * jnp.cos(li * dt * fold))[..., None]
    fd_im = (fd_mag * jnp.sin(li * dt * fold))[..., None]
    b_stack_re = jnp.concatenate([bd_b(fd_re[d] * bb_re - fd_im[d] * bb_im) for d in range(S5_FOLD)], axis=1)
    b_stack_im = jnp.concatenate([bd_b(fd_re[d] * bb_im + fd_im[d] * bb_re) for d in range(S5_FOLD)], axis=1)
    return (b_stack_re, b_stack_im, pw_re, pw_im,
            bd_c(c_re.astype(F32)), bd_c(c_im.astype(F32)))


def _rwkv_kernel(has_vres, *refs):
    if has_vres:
        (z_ref, vf_ref, mu_ref, wl_ref, w0_ref, al_ref, a0_ref, gl_ref, kk_ref, ka_ref, rk_ref,
         lw_ref, lb_ref, seg_ref, segt_ref, vb_ref, vbias_ref, o_ref, prev_ref, h_ref) = refs
    else:
        (z_ref, mu_ref, wl_ref, w0_ref, al_ref, a0_ref, gl_ref, kk_ref, ka_ref, rk_ref,
         lw_ref, lb_ref, seg_ref, segt_ref, o_ref, vo_ref, prev_ref, h_ref) = refs
    tb = z_ref.shape[0]
    t = RWKV_T
    n_chunk = tb // t
    width = o_ref.shape[1]
    n_pair = width // LANE

    @pl.when(pl.program_id(0) == 0)
    def _():
        prev_ref[...] = jnp.zeros_like(prev_ref)
        h_ref[...] = jnp.zeros_like(h_ref)

    z = z_ref[...]
    rowz = lax.broadcasted_iota(jnp.int32, z.shape, 0)
    prev = jnp.where(rowz == 0, prev_ref[...], pltpu.roll(z, 1, 0))
    prev_ref[...] = z[tb - 1:tb]
    zs = z + (prev - z) * mu_ref[...]

    o1 = 3 * width
    r = zs[:, 0:width]
    k = zs[:, width:2 * width]
    v = zs[:, 2 * width:o1]
    w_in = zs[:, o1:o1 + LANE]
    a_in = zs[:, o1 + LANE:o1 + 2 * LANE]
    g_in = zs[:, o1 + 2 * LANE:o1 + 2 * LANE + RWKV_GATE_LORA]

    wpre = w0_ref[...] + _dot_hp(jnp.tanh(w_in), wl_ref)
    logw = -jnp.exp(-_softplus(-wpre) - 0.5)
    a = _sigmoid(a0_ref[...] + _dot(a_in, al_ref[...]))
    g = _dot(_sigmoid(g_in), gl_ref[...])
    if has_vres:
        vr = zs[:, o1 + 2 * LANE + RWKV_GATE_LORA:]
        vg = _sigmoid(vbias_ref[...] + _dot(vr, vb_ref[...]))
        v = v + (vf_ref[...] - v) * vg
    else:
        vo_ref[...] = v

    seg = seg_ref[...]
    seg_t = segt_ref[...]
    kk = k * kk_ref[...]
    kk = kk * lax.rsqrt(jnp.maximum(_seg_sum(kk * kk, seg, seg_t), 1e-24))
    k2 = k * (1.0 + (a - 1.0) * ka_ref[...])
    kka = kk * a

    shift = t.bit_length() - 1
    lc = _chunk_cumsum(logw, t)

    gi = lax.broadcasted_iota(jnp.int32, (4 * t, 4 * t), 0)
    gj = lax.broadcasted_iota(jnp.int32, (4 * t, 4 * t), 1)
    ti = gi & (t - 1)
    sj = gj & (t - 1)
    keep = sj + jnp.where(gi < 2 * t, 1, 0) <= ti
    lane = lax.broadcasted_iota(jnp.int32, (t, LANE), 1)
    head0 = lane < RWKV_HEAD

    def stack2(x):
        return jnp.concatenate([jnp.where(head0, x, 0.0), jnp.where(head0, 0.0, x)], axis=0)

    e_pos = jnp.exp(lc)
    e_neg = jnp.exp(-lc)
    e_prev = jnp.exp(lc - logw)
    rt_all = r * e_pos
    at_all = -kk * e_prev
    bt_all = kka * e_neg
    kt_all = k2 * e_neg

    items = [(c, p) for c in range(n_chunk) for p in range(n_pair)]
    tile = lambda x, c, p: x[c * t:(c + 1) * t, p * LANE:(p + 1) * LANE]
    at = [tile(at_all, c, p) for c, p in items]
    rt = [tile(rt_all, c, p) for c, p in items]
    vbd = [stack2(tile(v, c, p)) for c, p in items]
    gm = [jnp.where(keep, _dot_nt(jnp.concatenate([stack2(a_), stack2(r_)], axis=0),
                                  jnp.concatenate([stack2(tile(bt_all, c, p)), stack2(tile(kt_all, c, p))],
                                                  axis=0)), 0.0)
          for a_, r_, (c, p) in zip(at, rt, items)]
    n1 = [g_[0:2 * t, 0:2 * t] for g_ in gm]
    rhs0 = [_dot(g_[0:2 * t, 2 * t:4 * t], v_) for g_, v_ in zip(gm, vbd)]
    qi = lax.broadcasted_iota(jnp.int32, (2 * t, 2 * t), 0)
    qj = lax.broadcasted_iota(jnp.int32, (2 * t, 2 * t), 1)
    xinv = [jnp.where(qi == qj, 1.0, jnp.where((qi ^ qj) == 1, n_, 0.0)) for n_ in n1]
    for lvl in range(1, shift):
        couple = ((qi >> lvl) ^ (qj >> lvl)) == 1
        xm = [_dot(x_, jnp.where(couple, n_, 0.0)) for x_, n_ in zip(xinv, n1)]
        xinv = [x_ + _dot(m_, x_) for x_, m_ in zip(xinv, xm)]

    hts = [h_ref[p] for p in range(n_pair)]
    out_rows = []
    for c in range(n_chunk):
        idx = [c * n_pair + p for p in range(n_pair)]
        lc_c = lc[c * t:(c + 1) * t]
        e_rem = jnp.exp(lc_c[t - 1:t] - lc_c)
        bh_c = kka[c * t:(c + 1) * t] * e_rem
        kh_c = k2[c * t:(c + 1) * t] * e_rem
        e_last = e_pos[(c + 1) * t - 1:(c + 1) * t]
        ph = [_dot_nt(jnp.concatenate([at[i], rt[i]], axis=0), h_) for i, h_ in zip(idx, hts)]
        u = [_dot(xinv[i], stack2(p_[0:t]) + rhs0[i]) for i, p_ in zip(idx, ph)]
        uv = [jnp.concatenate([u_, vbd[i]], axis=0) for i, u_ in zip(idx, u)]
        opk = [stack2(p_[t:2 * t]) + _dot(gm[i][2 * t:4 * t, :], uv_) for i, p_, uv_ in zip(idx, ph, uv)]
        out_rows.append(jnp.concatenate([o_[0:t] + o_[t:2 * t] for o_ in opk], axis=1))
        hts = [h_ * e_last[:, p * LANE:(p + 1) * LANE]
               + _dot_tn(uv_, jnp.concatenate([stack2(bh_c[:, p * LANE:(p + 1) * LANE]),
                                               stack2(kh_c[:, p * LANE:(p + 1) * LANE])], axis=0))
               for p, (h_, uv_) in enumerate(zip(hts, uv))]
    for p in range(n_pair):
        h_ref[p] = hts[p]

    y = jnp.concatenate(out_rows, axis=0)
    inv_n = 1.0 / RWKV_HEAD
    mean = _seg_sum(y, seg, seg_t) * inv_n
    yc = y - mean
    var = _seg_sum(yc * yc, seg, seg_t) * inv_n
    yn = yc * lax.rsqrt(var + RWKV_LNX_EPS) * lw_ref[...] + lb_ref[...]
    bonus = _seg_sum(r * k2 * rk_ref[...], seg, seg_t) * v
    o_ref[...] = ((yn + bonus) * g).astype(o_ref.dtype)


def _rwkv_call(z, v_first, mu, wl, w0, al, a0, gl, k_k, k_a, r_k, lnx_w, lnx_b, seg, seg_t, vb, vbias):
    length, zc = z.shape
    width = w0.shape[1]
    assert length % RWKV_BLOCK == 0 and width % (2 * RWKV_HEAD) == 0, (length, width)
    has_vres = v_first is not None
    full = lambda a: pl.BlockSpec(a.shape, lambda i: (0,) * a.ndim)
    rows = lambda c: pl.BlockSpec((RWKV_BLOCK, c), lambda i: (i, 0))
    common = [mu, wl, w0, al, a0, gl, k_k, k_a, r_k, lnx_w, lnx_b, seg, seg_t]
    if has_vres:
        args = [z, v_first] + common + [vb, vbias]
        in_specs = [rows(zc), rows(width)] + [full(a) for a in common + [vb, vbias]]
        out_shape = jax.ShapeDtypeStruct((length, width), BF16)
        out_specs = rows(width)
    else:
        args = [z] + common
        in_specs = [rows(zc)] + [full(a) for a in common]
        out_shape = (jax.ShapeDtypeStruct((length, width), BF16),
                     jax.ShapeDtypeStruct((length, width), F32))
        out_specs = (rows(width), rows(width))
    return pl.pallas_call(
        functools.partial(_rwkv_kernel, has_vres),
        out_shape=out_shape,
        grid=(length // RWKV_BLOCK,),
        in_specs=in_specs,
        out_specs=out_specs,
        scratch_shapes=[pltpu.VMEM((1, zc), F32),
                        pltpu.VMEM((width // LANE, LANE, LANE), F32)],
        compiler_params=_params(("arbitrary",)),
        name="rwkv7_mixer",
    )(*args)


def _gla_kernel(z_ref, al_ref, ab_ref, ng_ref, o_ref, st_ref):
    tb = z_ref.shape[0]
    t = GLA_T
    n_chunk = tb // t
    width = o_ref.shape[1]
    n_head = width // GLA_DV

    @pl.when(pl.program_id(0) == 0)
    def _():
        st_ref[...] = jnp.zeros_like(st_ref)

    z = z_ref[...]
    kw = n_head * GLA_DK
    q = z[:, 0:kw] * (GLA_DK ** -0.5)
    k = z[:, kw:2 * kw]
    v = z[:, 2 * kw:2 * kw + width]
    g = z[:, 2 * kw + width:2 * kw + 2 * width]
    a_in = z[:, 2 * kw + 2 * width:]
    x = _dot_hp(a_in, al_ref) + ab_ref[...]
    log_a = -_softplus(-x) * (1.0 / GLA_TAU)

    b = _chunk_cumsum(log_a, t)
    causal =(lax.broadcasted_iota(jnp.int32, (t, t), 1) <= lax.broadcasted_iota(jnp.int32, (t, t), 0))

    q_in = q * jnp.exp(b)
    heads = range(n_head)
    head = lambda x, h: x[:, h * GLA_DV:(h + 1) * GLA_DV]
    head_k = lambda x, h: x[:, h * GLA_DK:(h + 1) * GLA_DK]
    chunk = lambda x, c: x[c * t:(c + 1) * t]
    v_c = [chunk(v, c) for c in range(n_chunk)]
    k_rem = []
    e_last = []
    intra = []
    for c in range(n_chunk):
        b_c = chunk(b, c)
        b_mid = b_c[t // 2:t // 2 + 1]
        b_last = b_c[t - 1:t]
        q_mid = chunk(q, c) * jnp.exp(b_c - b_mid)
        k_mid = chunk(k, c) * jnp.exp(b_mid - b_c)
        k_rem.append(chunk(k, c) * jnp.exp(b_last - b_c))
        e_last.append(jnp.exp(b_last))
        intra.append([_dot(jnp.where(causal, _dot_nt(head_k(q_mid, h), head_k(k_mid, h)), 0.0),
                           head(v_c[c], h)) for h in heads])
    sts = [st_ref[h] for h in heads]
    out_rows = []
    for c in range(n_chunk):
        q_c = chunk(q_in, c)
        o = [i_ + _dot_nt(head_k(q_c, h), s_) for h, (i_, s_) in enumerate(zip(intra[c], sts))]
        sts = [s_ * head_k(e_last[c], h) + _dot_tn(head(v_c[c], h), head_k(k_rem[c], h))
               for h, s_ in enumerate(sts)]
        out_rows.append(jnp.concatenate(
            [o_ * lax.rsqrt(jnp.mean(o_ * o_, axis=-1, keepdims=True) + NORM_EPS) for o_ in o], axis=1))
    for h in heads:
        st_ref[h] = sts[h]
    o = jnp.concatenate(out_rows, axis=0)
    o_ref[...] = (o * ng_ref[...] * (g * _sigmoid(g))).astype(o_ref.dtype)


def _gla_call(z, al, ab, ng):
    length, zc = z.shape
    width = ng.shape[1]
    assert length % GLA_BLOCK == 0 and width % GLA_DV == 0, (length, width)
    full = lambda a: pl.BlockSpec(a.shape, lambda i: (0,) * a.ndim)
    return pl.pallas_call(
        _gla_kernel,
        out_shape=jax.ShapeDtypeStruct((length, width), BF16),
        grid=(length // GLA_BLOCK,),
        in_specs=[pl.BlockSpec((GLA_BLOCK, zc), lambda i: (i, 0)), full(al), full(ab), full(ng)],
        out_specs=pl.BlockSpec((GLA_BLOCK, width), lambda i: (i, 0)),
        scratch_shapes=[pltpu.VMEM((width // GLA_DV, GLA_DV, GLA_DK), F32)],
        compiler_params=_params(("arbitrary",)),
        name="gla_mixer",
    )(z, al, ab, ng)


def _merge_kernel(ya_ref, yb_ref, yc_ref, gate_ref, x_ref, wua_ref, wub_ref, wuc_ref, wo_ref, g_ref,
                  xo_ref, h_ref):
    d = x_ref.shape[1]
    gates = gate_ref[...].astype(F32)
    merged = (gates[:, 0:d] * jnp.dot(ya_ref[...], wua_ref[...], preferred_element_type=F32)
              + gates[:, d:2 * d] * jnp.dot(yb_ref[...], wub_ref[...], preferred_element_type=F32)
              + gates[:, 2 * d:3 * d] * jnp.dot(yc_ref[...], wuc_ref[...], preferred_element_type=F32))
    x = x_ref[...] + _dot(merged, wo_ref[...])
    xo_ref[...] = x
    h_ref[...] = _rms_rows(x, g_ref[...]).astype(h_ref.dtype)


def _merge_call(ya, yb, yc, gates, x, wua, wub, wuc, wo, layer, g, tm=256):
    m, d = x.shape
    assert m % tm == 0, (m, tm)
    rows = lambda a: pl.BlockSpec((tm, a.shape[1]), lambda i: (i, 0))
    const = lambda a: pl.BlockSpec(a.shape, lambda i: (0,) * a.ndim, pipeline_mode=pl.Buffered(1))
    wo_spec = pl.BlockSpec((None, d, d), lambda i: (layer, 0, 0), pipeline_mode=pl.Buffered(1))
    return pl.pallas_call(
        _merge_kernel,
        out_shape=(jax.ShapeDtypeStruct((m, d), F32), jax.ShapeDtypeStruct((m, d), BF16)),
        grid=(m // tm,),
        in_specs=[rows(ya), rows(yb), rows(yc), rows(gates), rows(x),
                  const(wua), const(wub), const(wuc), wo_spec, const(g)],
        out_specs=(pl.BlockSpec((tm, d), lambda i: (i, 0)), pl.BlockSpec((tm, d), lambda i: (i, 0))),
        compiler_params=_params(("parallel",)),
        name="merge_out_proj",
    )(ya, yb, yc, gates, x, wua, wub, wuc, wo, g)


def _mlp_kernel(final, h_ref, w1_ref, w2_ref, x_ref, g_ref, xo_ref, *rest):
    f = pl.program_id(1)

    @pl.when(f == 0)
    def _():
        xo_ref[...] = x_ref[...]

    hid = jnp.maximum(jnp.dot(h_ref[...], w1_ref[...], preferred_element_type=F32), 0.0)
    xo_ref[...] += _dot(hid * hid, w2_ref[...])

    @pl.when(f == pl.num_programs(1) - 1)
    def _():
        y = _rms_rows(xo_ref[...], g_ref[...])
        if final:
            xo_ref[...] = y
        else:
            rest[0][...] = y.astype(rest[0].dtype)


def _mlp_call(h, w1, w2, layer, x, g_next, final, tm=512, tf=1024):
    m, d = x.shape
    ff = w1.shape[2]
    assert m % tm == 0 and ff % tf == 0, (m, tm, ff, tf)
    row_blk = pl.BlockSpec((tm, d), lambda i, f: (i, 0))
    out_shape = [jax.ShapeDtypeStruct((m, d), F32)]
    out_specs = [row_blk]
    if not final:
        out_shape.append(jax.ShapeDtypeStruct((m, d), BF16))
        out_specs.append(row_blk)
    return pl.pallas_call(
        functools.partial(_mlp_kernel, final),
        out_shape=tuple(out_shape),
        grid=(m // tm, ff // tf),
        in_specs=[row_blk,
                  pl.BlockSpec((None, d, tf), lambda i, f: (layer, 0, f)),
                  pl.BlockSpec((None, tf, d), lambda i, f: (layer, f, 0)),
                  row_blk,
                  pl.BlockSpec((1, d), lambda i, f: (0, 0))],
        out_specs=tuple(out_specs),
        compiler_params=_params(("parallel", "arbitrary")),
        name="mlp",
    )(h, w1, w2, x, g_next)


def _pad_cols(a, n):
    return jnp.pad(a, [(0, 0)] * (a.ndim - 1) + [(0, n - a.shape[-1])])


def _pad_rows(a, n):
    return jnp.pad(a, [(0, 0)] * (a.ndim - 2) + [(0, n - a.shape[-2]), (0, 0)])


def kernel(x, norm_mix, w_in, gate_bias, s5_lambda_re, s5_lambda_im, s5_log_step, s5_b_re, s5_b_im, s5_c_re, s5_c_im, s5_d, s5_glu_w, s5_glu_b, rwkv_mu, rwkv_w_lora, rwkv_w0, rwkv_a_lora, rwkv_a0, rwkv_g_lora, rwkv_k_k, rwkv_k_a, rwkv_r_k, rwkv_lnx_w, rwkv_lnx_b, rwkv_vres_a, rwkv_vres_mu, rwkv_vres_b, rwkv_vres_bias, gla_alpha_lora, gla_alpha_bias, gla_norm_g, w_up, w_out, norm_mlp, mlp_w1, mlp_w2, final_norm):
    bsz, length, d = x.shape
    depth = w_in.shape[0]
    s5_w = s5_d.shape[1]
    rw_w = rwkv_w0.shape[1]
    gla_v = gla_norm_g.shape[1]
    gla_heads = gla_v // GLA_DV
    gla_k = gla_heads * GLA_DK
    rw_cols = 3 * rw_w + RWKV_DECAY_LORA + RWKV_AAA_LORA + RWKV_GATE_LORA
    gla_cols = 2 * gla_k + 2 * gla_v + GLA_LORA
    o_rw = s5_w
    o_gla = o_rw + rw_cols
    o_gate = o_gla + gla_cols
    o1 = 3 * rw_w

    head_id = jnp.arange(rw_w) // RWKV_HEAD
    seg = (head_id[:, None] == jnp.arange(LANE)[None, :]).astype(BF16)
    seg_t = seg.T
    rows1 = lambda a: a.reshape(a.shape[0], 1, -1).astype(F32)

    w_in_bf = w_in.astype(BF16)
    w_s5 = w_in_bf[:, :, :o_rw]
    wr = w_in_bf[:, :, o_rw:o_gla]
    o2 = o1 + RWKV_DECAY_LORA
    vres_a = jnp.concatenate([jnp.zeros((1,) + rwkv_vres_a.shape[1:], BF16), rwkv_vres_a.astype(BF16)], axis=0)
    vres_mu = jnp.concatenate([jnp.zeros((1,) + rwkv_vres_mu.shape[1:], F32), rwkv_vres_mu.astype(F32)], axis=0)
    w_rw = jnp.concatenate([wr[:, :, :o1], _pad_cols(wr[:, :, o1:o2], LANE), wr[:, :, o2:],
                            _pad_cols(vres_a, LANE)], axis=2)
    wg = w_in_bf[:, :, o_gla:o_gate]
    w_gla = jnp.concatenate([wg[:, :, :2 * gla_k + 2 * gla_v],
                             _pad_cols(wg[:, :, 2 * gla_k + 2 * gla_v:], LANE)], axis=2)
    w_gate = w_in_bf[:, :, o_gate:]
    mu_rw = rows1(jnp.concatenate([rwkv_mu[:, :o1], _pad_cols(rwkv_mu[:, o1:o2], LANE), rwkv_mu[:, o2:],
                                   _pad_cols(vres_mu, LANE)], axis=1))
    gate_b = rows1(gate_bias)
    s5_tabs = jax.vmap(_s5_tables)(s5_lambda_re, s5_lambda_im, s5_log_step, s5_b_re.astype(F32),
                                   s5_b_im.astype(F32), s5_c_re, s5_c_im)
    s5_dd, s5_gw, s5_gb = rows1(s5_d), s5_glu_w.astype(BF16), rows1(s5_glu_b)
    rw_wl = jax.vmap(_split_weight)(_pad_rows(rwkv_w_lora, LANE))
    rw_vb = _pad_rows(rwkv_vres_b, LANE).astype(BF16)
    rw_rows = [rows1(a) for a in (rwkv_w0, rwkv_a0, rwkv_k_k, rwkv_k_a, rwkv_r_k, rwkv_lnx_w, rwkv_lnx_b)]
    rw_al, rw_gl, rw_vbias = rwkv_a_lora.astype(BF16), rwkv_g_lora.astype(BF16), rows1(rwkv_vres_bias)
    gla_al = jax.vmap(_split_weight)(_pad_rows(gla_alpha_lora, LANE))
    gla_ab = rows1(gla_alpha_bias)
    gla_ng = rows1(gla_norm_g)
    wu_bf, wo_bf = w_up.astype(BF16), w_out.astype(BF16)
    w1_bf, w2_bf = mlp_w1.astype(BF16), mlp_w2.astype(BF16)
    n_mix, n_mlp, n_fin = rows1(norm_mix), rows1(norm_mlp), final_norm.reshape(1, -1).astype(F32)

    outs = []
    for b in range(bsz):
        xb = x[b].astype(F32)
        u = _norm_call(xb, n_mix[0])
        v_first = None
        for l in range(depth):
            z_s5 = _mm_call(u, w_s5, l, tn=s5_w, name="in_proj_s5")
            z_rw = _mm_call(u, w_rw, l, tn=w_rw.shape[2] // 2, name="in_proj_rwkv")
            z_gla = _mm_call(u, w_gla, l, tn=w_gla.shape[2] // 2, name="in_proj_gla")
            gates = _mm_call(u, w_gate, l, tn=1024, bias=gate_b[l], tm=2048, name="in_proj_gate")

            y_a = _s5_call(z_s5, *[t_[l] for t_ in s5_tabs], s5_dd[l], s5_gw[l], s5_gb[l])

            w0, a0, k_k, k_a, r_k, lnx_w, lnx_b = [a[l] for a in rw_rows]
            vb, vbias = (rw_vb[l - 1], rw_vbias[l - 1]) if l > 0 else (None, None)
            res = _rwkv_call(z_rw, v_first, mu_rw[l], rw_wl[l], w0, rw_al[l], a0, rw_gl[l],
                             k_k, k_a, r_k, lnx_w, lnx_b, seg, seg_t, vb, vbias)
            if l == 0:
                y_b, v_first = res
            else:
                y_b = res

            y_c = _gla_call(z_gla, gla_al[l], gla_ab[l], gla_ng[l])

            wu = wu_bf[l]
            x_mid, h = _merge_call(y_a, y_b, y_c, gates, xb,
                                   wu[:s5_w], wu[s5_w:s5_w + rw_w], wu[s5_w + rw_w:],
                                   wo_bf, l, n_mlp[l])
            final = l == depth - 1
            res = _mlp_call(h, w1_bf, w2_bf, l, x_mid, n_fin if final else n_mix[l + 1], final)
            if final:
                xb = res[0]
            else:
                xb, u = res
        outs.append(xb.astype(x.dtype))
    return jnp.stack(outs, axis=0)
```
